```python
import math
import jax, jax.numpy as jnp
from jax import lax
import numpy as np

D_MODEL = 2048
BATCH = 8
SEQ = 2048
DEPTH = 1

GRID_W = 64
CTX_LEN = 256

ATT_HEADS = 8
ATT_DIM = 64
ATT_WIDTH = ATT_HEADS * 2 * ATT_DIM
SSM_GROUP = 16
SSM_GROUPS = 32
SSM_WIDTH = SSM_GROUP * SSM_GROUPS
SSM_STATE = 64

Q_BLOCK = 128
ROPE_BASE = 10000.0
NORM_EPS = 1e-6

SPLITS = [ATT_WIDTH, 2 * ATT_WIDTH, 3 * ATT_WIDTH, 4 * ATT_WIDTH,
          4 * ATT_WIDTH + SSM_WIDTH, 4 * ATT_WIDTH + 2 * SSM_WIDTH]
IN_WIDTH = 4 * ATT_WIDTH + 2 * SSM_WIDTH + 2 * D_MODEL

kernel_name = "hybrid_diffattn_s5_prefix_dit"


def rmsnorm(x, g):
    xf = x.astype(jnp.float32)
    y = xf * lax.rsqrt(jnp.mean(xf * xf, axis=-1, keepdims=True) + NORM_EPS)
    return (y * g.astype(jnp.float32)).astype(x.dtype)


def rope_axis(xp, pos):
    nf = xp.shape[-1] // 2
    inv = ROPE_BASE ** (-jnp.arange(nf, dtype=jnp.float32) / nf)
    ang = pos[:, None] * inv[None, :]
    cos = jnp.cos(ang)[None, :, None, None, :].astype(xp.dtype)
    sin = jnp.sin(ang)[None, :, None, None, :].astype(xp.dtype)
    x1, x2 = xp[..., :nf], xp[..., nf:]
    return jnp.concatenate([x1 * cos - x2 * sin, x2 * cos + x1 * sin], axis=-1)


def rope2d(x, row, col):
    half = x.shape[-1] // 2
    return jnp.concatenate([rope_axis(x[..., :half], row), rope_axis(x[..., half:], col)], axis=-1)


def diff_attend(qb, k, v, lam):
    s = jnp.einsum('bqhsd,bkhsd->bhsqk', qb, k).astype(jnp.float32) * (ATT_DIM ** -0.5)
    p = jax.nn.softmax(s, axis=-1)
    a = p[:, :, 0] - lam * p[:, :, 1]
    return jnp.einsum('bhqk,bkhe->bqhe', a.astype(v.dtype), v)


def attn_post(o, subln_g, lam_init, gate):
    B, L = o.shape[0], o.shape[1]
    o = rmsnorm(o, subln_g) * (1.0 - lam_init)
    return o.reshape(B, L, ATT_WIDTH) * jax.nn.silu(gate)


def discretize(lre, lim, log_step, b_re, b_im):
    lre = lre.astype(jnp.float32); lim = lim.astype(jnp.float32)
    dt = jnp.exp(log_step.astype(jnp.float32))[:, None]
    mag = jnp.exp(lre * dt)
    ar, ai = mag * jnp.cos(lim * dt), mag * jnp.sin(lim * dt)
    den = lre * lre + lim * lim
    fr = ((ar - 1.0) * lre + ai * lim) / den
    fi = (ai * lre - (ar - 1.0) * lim) / den
    br, bi = b_re.astype(jnp.float32), b_im.astype(jnp.float32)
    bbr = fr[..., None] * br - fi[..., None] * bi
    bbi = fr[..., None] * bi + fi[..., None] * br
    return ar, ai, bbr, bbi


def _lin_combine(e1, e2):
    a1r, a1i, b1r, b1i = e1
    a2r, a2i, b2r, b2i = e2
    return (a1r * a2r - a1i * a2i,
            a1r * a2i + a1i * a2r,
            a2r * b1r - a2i * b1i + b2r,
            a2r * b1i + a2i * b1r + b2i)


def ssm_scan(u, disc, h0, reverse):
    ar, ai, bbr, bbi = disc
    bu_r = jnp.einsum('blgc,gpc->blgp', u, bbr)
    bu_i = jnp.einsum('blgc,gpc->blgp', u, bbi)
    if reverse:
        bu_r, bu_i = jnp.flip(bu_r, axis=1), jnp.flip(bu_i, axis=1)
    if h0 is not None:
        h0r, h0i = h0
        bu_r = bu_r.at[:, 0].add(ar * h0r - ai * h0i)
        bu_i = bu_i.at[:, 0].add(ar * h0i + ai * h0r)
    L = u.shape[1]
    a_r = jnp.broadcast_to(ar, (1, L) + ar.shape)
    a_i = jnp.broadcast_to(ai, (1, L) + ai.shape)
    _, _, hr, hi = lax.associative_scan(_lin_combine, (a_r, a_i, bu_r, bu_i), axis=1)
    if reverse:
        hr, hi = jnp.flip(hr, axis=1), jnp.flip(hi, axis=1)
    return hr, hi


def ssm_readout(u, st_f, st_b, c_re, c_im, d, w_glu, b_glu, gate, out_dtype):
    B, L = u.shape[0], u.shape[1]
    cr, ci = c_re.astype(jnp.float32), c_im.astype(jnp.float32)
    y = d.astype(jnp.float32) * u
    for di, (hr, hi) in enumerate((st_f, st_b)):
        y = y + jnp.einsum('gcp,blgp->blgc', cr[di], hr) - jnp.einsum('gcp,blgp->blgc', ci[di], hi)
    y = jax.nn.gelu(y.reshape(B, L, SSM_WIDTH).astype(out_dtype))
    y = y * jax.nn.sigmoid(y @ w_glu + b_glu)
    return y * jax.nn.silu(gate)


def merge_out(a_br, s_br, gm, w_pa, w_ps, w_out):
    g_a, g_s = jnp.split(jax.nn.sigmoid(gm), 2, axis=-1)
    return (g_a * (a_br @ w_pa) + g_s * (s_br @ w_ps)) @ w_out


def _layer(x, xc, c, c_ctx, p, lam_init, update_ctx):
    B, L, _ = x.shape
    Lc = xc.shape[1]
    shift, scale, gate = jnp.split(jax.nn.silu(c) @ p['w_ada'] + p['b_ada'], 3, axis=-1)
    shift_c, scale_c, gate_c = jnp.split(jax.nn.silu(c_ctx) @ p['w_ada'] + p['b_ada'], 3, axis=-1)
    h = rmsnorm(x, p['norm_g']) * (1.0 + scale[:, None]) + shift[:, None]
    hc = rmsnorm(xc, p['norm_g']) * (1.0 + scale_c) + shift_c

    q, k, v, ga, u, gs, gm = jnp.split(h @ p['w_in'], SPLITS, axis=-1)
    qc, kc, vc, gac, uc, gsc, gmc = jnp.split(hc @ p['w_in'], SPLITS, axis=-1)

    rows = L // GRID_W
    row = jnp.repeat(jnp.arange(rows), GRID_W).astype(jnp.float32)
    col = jnp.tile(jnp.arange(GRID_W), rows).astype(jnp.float32)
    q = rope2d(q.reshape(B, L, ATT_HEADS, 2, ATT_DIM), row, col)
    k = rope2d(k.reshape(B, L, ATT_HEADS, 2, ATT_DIM), row, col)
    v = v.reshape(B, L, ATT_HEADS, 2 * ATT_DIM)
    kc = kc.reshape(B, Lc, ATT_HEADS, 2, ATT_DIM)
    vc = vc.reshape(B, Lc, ATT_HEADS, 2 * ATT_DIM)
    lam = (jnp.exp(jnp.sum(p['lambda_q1'].astype(jnp.float32) * p['lambda_k1'].astype(jnp.float32)))
           - jnp.exp(jnp.sum(p['lambda_q2'].astype(jnp.float32) * p['lambda_k2'].astype(jnp.float32)))
           + lam_init)
    k_all = jnp.concatenate([k, kc], axis=1)
    v_all = jnp.concatenate([v, vc], axis=1)
    nblk = L // Q_BLOCK
    q_blocks = jnp.moveaxis(q.reshape(B, nblk, Q_BLOCK, ATT_HEADS, 2, ATT_DIM), 1, 0)
    o = lax.map(lambda qb: diff_attend(qb, k_all, v_all, lam), q_blocks)
    o = jnp.moveaxis(o, 0, 1).reshape(B, L, ATT_HEADS, 2 * ATT_DIM)
    a_br = attn_post(o, p['subln_g'], lam_init, ga)

    u = u.reshape(B, L, SSM_GROUPS, SSM_GROUP).astype(jnp.float32)
    uc = uc.reshape(B, Lc, SSM_GROUPS, SSM_GROUP).astype(jnp.float32)
    disc_f = discretize(p['ssm_lambda_re'][0], p['ssm_lambda_im'][0], p['ssm_log_step'][0],
                        p['ssm_b_re'][0], p['ssm_b_im'][0])
    disc_b = discretize(p['ssm_lambda_re'][1], p['ssm_lambda_im'][1], p['ssm_log_step'][1],
                        p['ssm_b_re'][1], p['ssm_b_im'][1])
    hcf = ssm_scan(uc, disc_f, None, reverse=False)
    hcb = ssm_scan(uc, disc_b, None, reverse=True)
    hf = ssm_scan(u, disc_f, (hcf[0][:, -1], hcf[1][:, -1]), reverse=False)
    hb = ssm_scan(u, disc_b, (hcb[0][:, 0], hcb[1][:, 0]), reverse=True)
    s_br = ssm_readout(u, hf, hb, p['ssm_c_re'], p['ssm_c_im'], p['ssm_d'],
                       p['w_glu'], p['b_glu'], gs, x.dtype)

    out = merge_out(a_br, s_br, gm, p['w_pa'], p['w_ps'], p['w_out'])
    x_new = x + gate[:, None] * out

    if update_ctx:
        qc = qc.reshape(B, Lc, ATT_HEADS, 2, ATT_DIM)
        a_c = attn_post(diff_attend(qc, kc, vc, lam), p['subln_g'], lam_init, gac)
        s_c = ssm_readout(uc, hcf, hcb, p['ssm_c_re'], p['ssm_c_im'], p['ssm_d'],
                          p['w_glu'], p['b_glu'], gsc, xc.dtype)
        xc = xc + gate_c * merge_out(a_c, s_c, gmc, p['w_pa'], p['w_ps'], p['w_out'])
    return x_new, xc


def setup_inputs(seed: int = 0) -> dict:
    key = jax.random.key(seed)
    ks = jax.random.split(key, 26)
    f32 = jnp.float32

    def nrm(k, shape, s):
        return jax.random.normal(k, shape, f32) * s

    G, P = SSM_GROUPS, SSM_STATE
    return {
        "x": nrm(ks[0], (BATCH, SEQ, D_MODEL), 1.0),
        "c": nrm(ks[1], (BATCH, D_MODEL), 1.0),
        "ctx": nrm(ks[2], (BATCH, CTX_LEN, D_MODEL), 1.0),
        "c_ctx": nrm(ks[3], (D_MODEL,), 1.0),
        "w_ada": nrm(ks[4], (DEPTH, D_MODEL, 3 * D_MODEL), 0.5 * D_MODEL ** -0.5),
        "b_ada": nrm(ks[5], (DEPTH, 3 * D_MODEL), 0.01),
        "norm_g": 1.0 + nrm(ks[6], (DEPTH, D_MODEL), 0.02),
        "w_in": nrm(ks[7], (DEPTH, D_MODEL, IN_WIDTH), D_MODEL ** -0.5),
        "lambda_q1": nrm(ks[8], (DEPTH, ATT_DIM), 0.1),
        "lambda_k1": nrm(ks[9], (DEPTH, ATT_DIM), 0.1),
        "lambda_q2": nrm(ks[10], (DEPTH, ATT_DIM), 0.1),
        "lambda_k2": nrm(ks[11], (DEPTH, ATT_DIM), 0.1),
        "subln_g": 1.0 + nrm(ks[12], (DEPTH, 2 * ATT_DIM), 0.02),
        "ssm_lambda_re": -0.5 + nrm(ks[13], (DEPTH, 2, G, P), 0.01),
        "ssm_lambda_im": jnp.pi * jnp.arange(P, dtype=f32) + nrm(ks[14], (DEPTH, 2, G, P), 0.01),
        "ssm_log_step": jax.random.uniform(ks[15], (DEPTH, 2, G), f32, math.log(1e-3), math.log(1e-1)),
        "ssm_b_re": nrm(ks[16], (DEPTH, 2, G, P, SSM_GROUP), (2 * SSM_GROUP) ** -0.5),
        "ssm_b_im": nrm(ks[17], (DEPTH, 2, G, P, SSM_GROUP), (2 * SSM_GROUP) ** -0.5),
        "ssm_c_re": nrm(ks[18], (DEPTH, 2, G, SSM_GROUP, P), P ** -0.5),
        "ssm_c_im": nrm(ks[19], (DEPTH, 2, G, SSM_GROUP, P), P ** -0.5),
        "ssm_d": nrm(ks[20], (DEPTH, G, SSM_GROUP), 1.0),
        "w_glu": nrm(ks[21], (DEPTH, SSM_WIDTH, SSM_WIDTH), SSM_WIDTH ** -0.5),
        "b_glu": nrm(ks[22], (DEPTH, SSM_WIDTH), 0.01),
        "w_pa": nrm(ks[23], (DEPTH, ATT_WIDTH, D_MODEL), ATT_WIDTH ** -0.5),
        "w_ps": nrm(ks[24], (DEPTH, SSM_WIDTH, D_MODEL), SSM_WIDTH ** -0.5),
        "w_out": nrm(ks[25], (DEPTH, D_MODEL, D_MODEL), D_MODEL ** -0.5),
        "final_g": 1.0 + nrm(jax.random.fold_in(key, 99), (D_MODEL,), 0.02),
    }


def reference(x, c, ctx, c_ctx, w_ada, b_ada, norm_g, w_in, lambda_q1, lambda_k1, lambda_q2,
              lambda_k2, subln_g, ssm_lambda_re, ssm_lambda_im, ssm_log_step, ssm_b_re, ssm_b_im,
              ssm_c_re, ssm_c_im, ssm_d, w_glu, b_glu, w_pa, w_ps, w_out, final_g):
    xc = ctx
    for i in range(DEPTH):
        p = dict(w_ada=w_ada[i], b_ada=b_ada[i], norm_g=norm_g[i], w_in=w_in[i],
                 lambda_q1=lambda_q1[i], lambda_k1=lambda_k1[i],
                 lambda_q2=lambda_q2[i], lambda_k2=lambda_k2[i], subln_g=subln_g[i],
                 ssm_lambda_re=ssm_lambda_re[i], ssm_lambda_im=ssm_lambda_im[i],
                 ssm_log_step=ssm_log_step[i], ssm_b_re=ssm_b_re[i], ssm_b_im=ssm_b_im[i],
                 ssm_c_re=ssm_c_re[i], ssm_c_im=ssm_c_im[i], ssm_d=ssm_d[i],
                 w_glu=w_glu[i], b_glu=b_glu[i], w_pa=w_pa[i], w_ps=w_ps[i], w_out=w_out[i])
        lam_init = 0.8 - 0.6 * math.exp(-0.3 * i)
        x, xc = _layer(x, xc, c, c_ctx, p, lam_init, update_ctx=(i < DEPTH - 1))
    return rmsnorm(x, final_g)
```

```python
import functools
import math

import jax
import jax.numpy as jnp
from jax import lax
from jax.experimental import pallas as pl
from jax.experimental.pallas import tpu as pltpu

F32 = jnp.float32
BF16 = jnp.bfloat16

D_MODEL = 2048
GRID_W = 64
ATT_HEADS = 8
ATT_DIM = 64
HEAD_W = 2 * ATT_DIM
ATT_WIDTH = ATT_HEADS * HEAD_W
SSM_GROUP = 16
SSM_GROUPS = 32
SSM_WIDTH = SSM_GROUP * SSM_GROUPS
SSM_STATE = 64
SSM_MODES = SSM_GROUPS * SSM_STATE
IN_WIDTH = 4 * ATT_WIDTH + 2 * SSM_WIDTH + 2 * D_MODEL
ROPE_BASE = 10000.0
NORM_EPS = 1e-6
LAM_INIT = 0.8 - 0.6 * math.exp(0.0)

LANES = 128
SUBLANES = 8
VMEM_BYTES_V7X = 64 * 1024 * 1024
MIB = 1024 * 1024

GROUPS_PER_BLOCK = LANES // SSM_GROUP
SSM_BLOCKS = SSM_GROUPS // GROUPS_PER_BLOCK
BLOCK_MODES = GROUPS_PER_BLOCK * SSM_STATE
BLOCK_STATE = 2 * BLOCK_MODES
SSM_CHUNK = 128


def _vmem_limit(nbytes):
    return int(min(nbytes + nbytes // 4, VMEM_BYTES_V7X - 8 * MIB))


def _sigmoid(v):
    return 1.0 / (1.0 + jnp.exp(-v))


def _silu(v):
    return v * _sigmoid(v)


def _adaln_kernel(c_ref, w_ref, b_ref, o_ref):
    cv = c_ref[...]
    o_ref[...] = jnp.dot(_silu(cv), w_ref[...], preferred_element_type=F32,
                         precision=lax.Precision.HIGHEST) + b_ref[...]


def _adaln(cc, w_ada, b_ada):
    rows, d = cc.shape
    n = w_ada.shape[1]
    tn = 768
    est = 2 * (d * tn * 4) + 2 * rows * tn * 4 + rows * d * 4 * 2
    return pl.pallas_call(
        _adaln_kernel,
        grid=(n // tn,),
        in_specs=[pl.BlockSpec((rows, d), lambda j: (0, 0)),
                  pl.BlockSpec((d, tn), lambda j: (0, j)),
                  pl.BlockSpec((1, tn), lambda j: (0, j))],
        out_specs=pl.BlockSpec((rows, tn), lambda j: (0, j)),
        out_shape=jax.ShapeDtypeStruct((rows, n), F32),
        compiler_params=pltpu.CompilerParams(dimension_semantics=("arbitrary",),
                                             vmem_limit_bytes=_vmem_limit(est)),
        name="adaln",
    )(cc, w_ada, b_ada)


def _in_proj_kernel(*refs, rope, norm_rows):
    if rope:
        x_ref, scale_ref, shift_ref, g_ref, w_ref, cos_ref, sina_ref, sinb_ref, o_ref, h_ref = refs
    else:
        x_ref, scale_ref, shift_ref, g_ref, w_ref, o_ref, h_ref = refs
    j = pl.program_id(1)
    tm = x_ref.shape[0]

    @pl.when(j == 0)
    def _():
        gmod = g_ref[...] * (1.0 + scale_ref[0])
        shift = shift_ref[0]

        def body(r, carry):
            r0 = pl.multiple_of(r * norm_rows, norm_rows)
            xf = x_ref[pl.ds(r0, norm_rows), :]
            ms = jnp.mean(xf * xf, axis=-1, keepdims=True)
            y = xf * lax.rsqrt(ms + NORM_EPS) * gmod + shift
            h_ref[pl.ds(r0, norm_rows), :] = y.astype(BF16)
            return carry

        lax.fori_loop(0, tm // norm_rows, body, 0)

    acc = jnp.dot(h_ref[...], w_ref[...], preferred_element_type=F32)

    if rope:
        @pl.when(j < 2)
        def _():
            qscale = jnp.where(j == 0, ATT_DIM ** -0.5, 1.0).astype(F32)
            cos = cos_ref[...] * qscale
            sina = sina_ref[...] * qscale
            sinb = sinb_ref[...] * qscale
            for cb in range(acc.shape[1] // HEAD_W):
                v = acc[:, cb * HEAD_W:(cb + 1) * HEAD_W]
                up = pltpu.roll(v, HEAD_W - ATT_DIM // 4, 1)
                dn = pltpu.roll(v, ATT_DIM // 4, 1)
                o_ref[:, cb * HEAD_W:(cb + 1) * HEAD_W] = (v * cos + up * sina + dn * sinb).astype(o_ref.dtype)

        @pl.when(j >= 2)
        def _():
            o_ref[...] = acc.astype(o_ref.dtype)
    else:
        o_ref[...] = acc.astype(o_ref.dtype)


def _in_proj(x2d, scale, shift, norm_g, w_bf, col_blocks, tn, rows_per_mod, rope_tables=None, tm=1024):
    m, d = x2d.shape
    nj = len(col_blocks)
    rope = rope_tables is not None
    blocks_per_mod = rows_per_mod // tm if rows_per_mod >= tm else None

    def mod_idx(i, j):
        return ((i // blocks_per_mod) if blocks_per_mod else 0, 0, 0)

    contiguous = all(cb == col_blocks[0] + k for k, cb in enumerate(col_blocks))
    if contiguous:
        w_map = lambda i, j: (0, j + col_blocks[0])
    else:
        cbs = col_blocks
        def w_map(i, j):
            idx = jnp.int32(cbs[-1])
            for k in range(nj - 2, -1, -1):
                idx = jnp.where(j == k, jnp.int32(cbs[k]), idx)
            return (0, idx)

    in_specs = [pl.BlockSpec((tm, d), lambda i, j: (i, 0)),
                pl.BlockSpec((1, 1, d), mod_idx),
                pl.BlockSpec((1, 1, d), mod_idx),
                pl.BlockSpec((1, d), lambda i, j: (0, 0)),
                pl.BlockSpec((d, tn), w_map)]
    args = [x2d, scale, shift, norm_g, w_bf]
    if rope:
        seq_blocks = rope_tables[0].shape[0] // tm
        for t in rope_tables:
            in_specs.append(pl.BlockSpec((tm, HEAD_W), lambda i, j: (i % seq_blocks, 0)))
            args.append(t)
    est = (2 * tm * d * 4 + tm * d * 2 + 2 * d * tn * 2 + 2 * tm * tn * 2 + 2 * tm * tn * 4
           + (6 * tm * HEAD_W * 4 if rope else 0))
    return pl.pallas_call(
        functools.partial(_in_proj_kernel, rope=rope, norm_rows=256),
        grid=(m // tm, nj),
        in_specs=in_specs,
        out_specs=pl.BlockSpec((tm, tn), lambda i, j: (i, j)),
        out_shape=jax.ShapeDtypeStruct((m, nj * tn), BF16),
        scratch_shapes=[pltpu.VMEM((tm, d), BF16)],
        compiler_params=pltpu.CompilerParams(dimension_semantics=("arbitrary", "arbitrary"),
                                             vmem_limit_bytes=_vmem_limit(est)),
        name="in_proj_rope" if rope else "in_proj_ctx",
    )(*args)


def _rope_tables(seq):
    rows = seq // GRID_W
    row = jnp.repeat(jnp.arange(rows), GRID_W).astype(F32)
    col = jnp.tile(jnp.arange(GRID_W), rows).astype(F32)
    nf = ATT_DIM // 4
    inv = ROPE_BASE ** (-jnp.arange(nf, dtype=F32) / nf)

    def axis_tables(pos):
        ang = pos[:, None] * inv[None, :]
        c, s = jnp.cos(ang), jnp.sin(ang)
        z = jnp.zeros_like(s)
        return (jnp.concatenate([c, c], -1), jnp.concatenate([-s, z], -1), jnp.concatenate([z, s], -1))

    parts = [axis_tables(row), axis_tables(col)]
    out = []
    for k in range(3):
        comp = jnp.concatenate([parts[0][k], parts[1][k]], -1)
        out.append(jnp.concatenate([comp, comp], -1))
    return out


def _attn_kernel(q_ref, k_ref, v_ref, kc_ref, vc_ref, ga_ref, lam_ref, g_ref, o_ref):
    nt = (((1,), (1,)), ((), ()))
    q = q_ref[...]
    lane = lax.broadcasted_iota(jnp.int32, q.shape, 1)
    zero = jnp.zeros_like(q)
    lamv = lam_ref[...]
    lam = (jnp.exp(jnp.sum(lamv[0:1] * lamv[1:2], axis=-1, keepdims=True))
           - jnp.exp(jnp.sum(lamv[2:3] * lamv[3:4], axis=-1, keepdims=True)) + LAM_INIT)

    probs = []
    for comp in range(2):
        qc = jnp.where((lane >= ATT_DIM) == bool(comp), q, zero)
        s_lat = lax.dot_general(qc, k_ref[...], nt, preferred_element_type=F32)
        s_ctx = lax.dot_general(qc, kc_ref[...], nt, preferred_element_type=F32)
        m = jnp.maximum(jnp.max(s_lat, axis=-1, keepdims=True), jnp.max(s_ctx, axis=-1, keepdims=True))
        p_lat = jnp.exp(s_lat - m)
        p_ctx = jnp.exp(s_ctx - m)
        l = jnp.sum(p_lat, axis=-1, keepdims=True) + jnp.sum(p_ctx, axis=-1, keepdims=True)
        probs.append((p_lat, p_ctx, 1.0 / l))
    c1 = probs[0][2]
    c2 = lam * probs[1][2]
    a_lat = (probs[0][0] * c1 - probs[1][0] * c2).astype(BF16)
    a_ctx = (probs[0][1] * c1 - probs[1][1] * c2).astype(BF16)
    o = (jnp.dot(a_lat, v_ref[...], preferred_element_type=F32)
         + jnp.dot(a_ctx, vc_ref[...], preferred_element_type=F32))
    ms = jnp.mean(o * o, axis=-1, keepdims=True)
    on = o * lax.rsqrt(ms + NORM_EPS) * g_ref[...] * (1.0 - LAM_INIT)
    o_ref[...] = (on * _silu(ga_ref[...].astype(F32))).astype(o_ref.dtype)


def _attention(qkv, kvc, lam_vecs, subln_g, batch, seq, ctx_len, tq=512):
    qb = seq // tq
    kcol = ATT_WIDTH // HEAD_W
    est = (4 * tq * (seq + ctx_len) * 4 + 3 * tq * (seq + ctx_len) * 2
           + 4 * (seq + ctx_len) * HEAD_W * 2 + 8 * tq * HEAD_W * 4)
    return pl.pallas_call(
        _attn_kernel,
        grid=(batch, ATT_HEADS, qb),
        in_specs=[pl.BlockSpec((tq, HEAD_W), lambda b, h, i: (b * qb + i, h)),
                  pl.BlockSpec((seq, HEAD_W), lambda b, h, i: (b, kcol + h)),
                  pl.BlockSpec((seq, HEAD_W), lambda b, h, i: (b, 2 * kcol + h)),
                  pl.BlockSpec((ctx_len, HEAD_W), lambda b, h, i: (b, h)),
                  pl.BlockSpec((ctx_len, HEAD_W), lambda b, h, i: (b, kcol + h)),
                  pl.BlockSpec((tq, HEAD_W), lambda b, h, i: (b * qb + i, 3 * kcol + h)),
                  pl.BlockSpec((4, ATT_DIM), lambda b, h, i: (0, 0)),
                  pl.BlockSpec((1, HEAD_W), lambda b, h, i: (0, 0))],
        out_specs=pl.BlockSpec((tq, HEAD_W), lambda b, h, i: (b * qb + i, h)),
        out_shape=jax.ShapeDtypeStruct((batch * seq, ATT_WIDTH), BF16),
        compiler_params=pltpu.CompilerParams(dimension_semantics=("arbitrary",) * 3,
                                             vmem_limit_bytes=_vmem_limit(est)),
        name="diff_attention",
    )(qkv, qkv, qkv, kvc, kvc, qkv, lam_vecs, subln_g)


def _ssm_prep_kernel(lre_ref, lim_ref, lstep_ref, bre_ref, bim_ref, cre_ref, cim_ref,
                     a_ref, bblk_ref, cblk_ref):
    lre = lre_ref[0]
    lim = lim_ref[0]
    dt = jnp.exp(lstep_ref[0])
    mag = jnp.exp(lre * dt)
    ar = mag * jnp.cos(lim * dt)
    ai = mag * jnp.sin(lim * dt)
    den = lre * lre + lim * lim
    fr = ((ar - 1.0) * lre + ai * lim) / den
    fi = (ai * lre - (ar - 1.0) * lim) / den
    a_ref[0, 0:1, :] = ar
    a_ref[0, 1:2, :] = ai
    br = bre_ref[0]
    bi = bim_ref[0]
    bbr = fr * br - fi * bi
    bbi = fr * bi + fi * br
    row_g = lax.broadcasted_iota(jnp.int32, (LANES, BLOCK_MODES), 0) // SSM_GROUP
    col_g = lax.broadcasted_iota(jnp.int32, (LANES, BLOCK_MODES), 1) // SSM_STATE
    keep_b = row_g == col_g
    row_g2 = lax.broadcasted_iota(jnp.int32, (BLOCK_MODES, LANES), 0) // SSM_STATE
    col_g2 = lax.broadcasted_iota(jnp.int32, (BLOCK_MODES, LANES), 1) // SSM_GROUP
    keep_c = row_g2 == col_g2
    for k in range(SSM_BLOCKS):
        sl = slice(k * BLOCK_MODES, (k + 1) * BLOCK_MODES)
        bblk_ref[0, k, :, 0:BLOCK_MODES] = jnp.where(keep_b, bbr[:, sl], 0.0).astype(BF16)
        bblk_ref[0, k, :, BLOCK_MODES:BLOCK_STATE] = jnp.where(keep_b, bbi[:, sl], 0.0).astype(BF16)
        cblk_ref[0, k, 0:BLOCK_MODES, :] = jnp.where(keep_c, cre_ref[0, sl, :], 0.0).astype(BF16)
        cblk_ref[0, k, BLOCK_MODES:BLOCK_STATE, :] = jnp.where(keep_c, -cim_ref[0, sl, :], 0.0).astype(BF16)


def _ssm_prep(lre, lim, lstep, bre_t, bim_t, cre_t, cim_t):
    nd = lre.shape[0]
    vec = pl.BlockSpec((1, 1, SSM_MODES), lambda d: (d, 0, 0))
    bsp = pl.BlockSpec((1, LANES, SSM_MODES), lambda d: (d, 0, 0))
    csp = pl.BlockSpec((1, SSM_MODES, LANES), lambda d: (d, 0, 0))
    est = 2 * (4 * LANES * SSM_MODES * 4 + 2 * SSM_BLOCKS * LANES * BLOCK_STATE * 2) + 8 * LANES * SSM_MODES * 4
    return pl.pallas_call(
        _ssm_prep_kernel,
        grid=(nd,),
        in_specs=[vec, vec, vec, bsp, bsp, csp, csp],
        out_specs=[pl.BlockSpec((1, 2, SSM_MODES), lambda d: (d, 0, 0)),
                   pl.BlockSpec((1, SSM_BLOCKS, LANES, BLOCK_STATE), lambda d: (d, 0, 0, 0)),
                   pl.BlockSpec((1, SSM_BLOCKS, BLOCK_STATE, LANES), lambda d: (d, 0, 0, 0))],
        out_shape=[jax.ShapeDtypeStruct((nd, 2, SSM_MODES), F32),
                   jax.ShapeDtypeStruct((nd, SSM_BLOCKS, LANES, BLOCK_STATE), BF16),
                   jax.ShapeDtypeStruct((nd, SSM_BLOCKS, BLOCK_STATE, LANES), BF16)],
        compiler_params=pltpu.CompilerParams(dimension_semantics=("arbitrary",),
                                             vmem_limit_bytes=_vmem_limit(est)),
        name="ssm_prep",
    )(lre, lim, lstep, bre_t, bim_t, cre_t, cim_t)


def _ssm_scan_kernel(*refs, reverse, batch):
    if reverse:
        u_ref, a_ref, bblk_ref, cblk_ref, yf_ref, d_ref, y_ref, s_ref, h_ref = refs
    else:
        u_ref, a_ref, bblk_ref, cblk_ref, y_ref, s_ref, h_ref = refs
    steps = u_ref.shape[0] // batch

    @pl.when(pl.program_id(0) == 0)
    def _():
        h_ref[...] = jnp.zeros_like(h_ref)

    u = u_ref[...]
    for k in range(SSM_BLOCKS):
        s_ref[:, k * BLOCK_STATE:(k + 1) * BLOCK_STATE] = jnp.dot(
            u[:, k * LANES:(k + 1) * LANES], bblk_ref[0, k], preferred_element_type=F32)

    for k in range(SSM_BLOCKS):
        c_re = k * BLOCK_STATE
        c_im = c_re + BLOCK_MODES
        ar = jnp.broadcast_to(a_ref[0, 0:1, k * BLOCK_MODES:(k + 1) * BLOCK_MODES], (batch, BLOCK_MODES))
        ai = jnp.broadcast_to(a_ref[0, 1:2, k * BLOCK_MODES:(k + 1) * BLOCK_MODES], (batch, BLOCK_MODES))

        def body(i, carry, c_re=c_re, c_im=c_im, ar=ar, ai=ai):
            hr, hi = carry
            t = (steps - 1 - i) if reverse else i
            r0 = pl.multiple_of(t * batch, batch)
            nhr = ar * hr - ai * hi + s_ref[pl.ds(r0, batch), c_re:c_re + BLOCK_MODES]
            nhi = ar * hi + ai * hr + s_ref[pl.ds(r0, batch), c_im:c_im + BLOCK_MODES]
            s_ref[pl.ds(r0, batch), c_re:c_re + BLOCK_MODES] = nhr
            s_ref[pl.ds(r0, batch), c_im:c_im + BLOCK_MODES] = nhi
            return nhr, nhi

        hr, hi = lax.fori_loop(0, steps, body,
                               (h_ref[:, c_re:c_re + BLOCK_MODES], h_ref[:, c_im:c_im + BLOCK_MODES]),
                               unroll=2)
        h_ref[:, c_re:c_re + BLOCK_MODES] = hr
        h_ref[:, c_im:c_im + BLOCK_MODES] = hi

    for k in range(SSM_BLOCKS):
        yk = jnp.dot(s_ref[:, k * BLOCK_STATE:(k + 1) * BLOCK_STATE].astype(BF16), cblk_ref[0, k],
                     preferred_element_type=F32)
        cols = slice(k * LANES, (k + 1) * LANES)
        if reverse:
            yk = yk + yf_ref[:, cols] + d_ref[:, cols] * u[:, cols].astype(F32)
        y_ref[:, cols] = yk


def _ssm_scan(u_tm, a, bblk, cblk, direction, batch, ctx_chunks, yf=None, dvec=None):
    rows = SSM_CHUNK * batch
    chunks = u_tm.shape[0] // rows
    reverse = direction == 1
    if reverse:
        last = chunks - 1
        cmap = lambda s: (jnp.where(s < ctx_chunks, ctx_chunks - 1 - s, last + ctx_chunks - s), 0)
    else:
        cmap = lambda s: (s, 0)
    in_specs = [pl.BlockSpec((rows, SSM_WIDTH), cmap),
                pl.BlockSpec((1, 2, SSM_MODES), lambda s: (direction, 0, 0)),
                pl.BlockSpec((1, SSM_BLOCKS, LANES, BLOCK_STATE), lambda s: (direction, 0, 0, 0)),
                pl.BlockSpec((1, SSM_BLOCKS, BLOCK_STATE, LANES), lambda s: (direction, 0, 0, 0))]
    args = [u_tm, a, bblk, cblk]
    if reverse:
        in_specs += [pl.BlockSpec((rows, SSM_WIDTH), cmap), pl.BlockSpec((1, SSM_WIDTH), lambda s: (0, 0))]
        args += [yf, dvec]
    state_w = SSM_BLOCKS * BLOCK_STATE
    est = (rows * state_w * 4 + 2 * rows * state_w // SSM_BLOCKS * 4 + 6 * rows * SSM_WIDTH * 4
           + 4 * SSM_BLOCKS * LANES * BLOCK_STATE * 2)
    return pl.pallas_call(
        functools.partial(_ssm_scan_kernel, reverse=reverse, batch=batch),
        grid=(chunks,),
        in_specs=in_specs,
        out_specs=pl.BlockSpec((rows, SSM_WIDTH), cmap),
        out_shape=jax.ShapeDtypeStruct((chunks * rows, SSM_WIDTH), F32),
        scratch_shapes=[pltpu.VMEM((rows, state_w), F32), pltpu.VMEM((batch, state_w), F32)],
        compiler_params=pltpu.CompilerParams(dimension_semantics=("arbitrary",),
                                             vmem_limit_bytes=_vmem_limit(est)),
        name="ssm_scan_bwd" if reverse else "ssm_scan_fwd",
    )(*args)


def _gelu_tanh(v):
    return 0.5 * v * (1.0 + jnp.tanh(math.sqrt(2.0 / math.pi) * (v + 0.044715 * v * v * v)))


def _merge_kernel(a_ref, y_ref, gs_ref, gma0_ref, gma1_ref, gms0_ref, gms1_ref, x_ref, gate_ref,
                  wglu_ref, bglu_ref, wpa_ref, wps_ref, wout_ref, fg_ref, o_ref):
    yg = _gelu_tanh(y_ref[...])
    z = yg * _sigmoid(jnp.dot(yg.astype(BF16), wglu_ref[...], preferred_element_type=F32) + bglu_ref[...])
    s_br = (z * _silu(gs_ref[...].astype(F32))).astype(BF16)
    ta = jnp.dot(a_ref[...], wpa_ref[...], preferred_element_type=F32)
    ts = jnp.dot(s_br, wps_ref[...], preferred_element_type=F32)
    half = D_MODEL // 2
    parts = []
    for p, (ga_r, gs_r) in enumerate(((gma0_ref, gms0_ref), (gma1_ref, gms1_ref))):
        cols = slice(p * half, (p + 1) * half)
        parts.append((_sigmoid(ga_r[...].astype(F32)) * ta[:, cols]
                      + _sigmoid(gs_r[...].astype(F32)) * ts[:, cols]).astype(BF16))
    t = jnp.concatenate(parts, axis=1)
    out = jnp.dot(t, wout_ref[...], preferred_element_type=F32)
    xn = x_ref[...] + gate_ref[0] * out
    ms = jnp.mean(xn * xn, axis=-1, keepdims=True)
    o_ref[...] = xn * lax.rsqrt(ms + NORM_EPS) * fg_ref[...]


def _merge_out(a_br, y_bm, qkv, x2d, gate, w_glu, b_glu, w_pa, w_ps, w_out, final_g, seq, tm=256):
    m, d = x2d.shape
    half = d // 2
    per_batch = seq // tm
    gs_col = (4 * ATT_WIDTH + SSM_WIDTH) // SSM_WIDTH
    gm_col = (4 * ATT_WIDTH + 2 * SSM_WIDTH) // half
    const = lambda i: (0, 0)
    est = (2 * (tm * ATT_WIDTH * 2 + tm * SSM_WIDTH * 6 + 4 * tm * half * 2 + 2 * tm * d * 4)
           + 2 * (SSM_WIDTH * SSM_WIDTH + ATT_WIDTH * d + SSM_WIDTH * d + d * d) * 2 + 6 * tm * d * 4)
    return pl.pallas_call(
        _merge_kernel,
        grid=(m // tm,),
        in_specs=[pl.BlockSpec((tm, ATT_WIDTH), lambda i: (i, 0)),
                  pl.BlockSpec((tm, SSM_WIDTH), lambda i: (i, 0)),
                  pl.BlockSpec((tm, SSM_WIDTH), lambda i: (i, gs_col)),
                  pl.BlockSpec((tm, half), lambda i: (i, gm_col)),
                  pl.BlockSpec((tm, half), lambda i: (i, gm_col + 1)),
                  pl.BlockSpec((tm, half), lambda i: (i, gm_col + 2)),
                  pl.BlockSpec((tm, half), lambda i: (i, gm_col + 3)),
                  pl.BlockSpec((tm, d), lambda i: (i, 0)),
                  pl.BlockSpec((1, 1, d), lambda i: (i // per_batch, 0, 0)),
                  pl.BlockSpec((SSM_WIDTH, SSM_WIDTH), const),
                  pl.BlockSpec((1, SSM_WIDTH), const),
                  pl.BlockSpec((ATT_WIDTH, d), const),
                  pl.BlockSpec((SSM_WIDTH, d), const),
                  pl.BlockSpec((d, d), const),
                  pl.BlockSpec((1, d), const)],
        out_specs=pl.BlockSpec((tm, d), lambda i: (i, 0)),
        out_shape=jax.ShapeDtypeStruct((m, d), F32),
        compiler_params=pltpu.CompilerParams(dimension_semantics=("arbitrary",),
                                             vmem_limit_bytes=_vmem_limit(est)),
        name="merge_out",
    )(a_br, y_bm, qkv, qkv, qkv, qkv, qkv, x2d, gate, w_glu, b_glu, w_pa, w_ps, w_out, final_g)


def kernel(x, c, ctx, c_ctx, w_ada, b_ada, norm_g, w_in, lambda_q1, lambda_k1, lambda_q2, lambda_k2,
           subln_g, ssm_lambda_re, ssm_lambda_im, ssm_log_step, ssm_b_re, ssm_b_im, ssm_c_re, ssm_c_im,
           ssm_d, w_glu, b_glu, w_pa, w_ps, w_out, final_g):
    batch, seq, d = x.shape
    ctx_len = ctx.shape[1]
    assert w_ada.shape[0] == 1, "single-layer trunk"
    assert (d, seq % 1024, ctx_len % SSM_CHUNK, seq % SSM_CHUNK) == (D_MODEL, 0, 0, 0)

    pad = (-(batch + 1)) % SUBLANES
    cc = jnp.concatenate([c, c_ctx[None, :], jnp.zeros((pad, d), F32)], axis=0)
    mod = _adaln(cc, w_ada[0], b_ada[0][None, :])
    shift, scale, gate = (mod[:, k * d:(k + 1) * d] for k in range(3))
    as_mod = lambda v, lo, hi: v[lo:hi].reshape(hi - lo, 1, d)

    w_in_bf = w_in[0].astype(BF16)
    x2d = x.reshape(batch * seq, d)
    ctx2d = ctx.reshape(batch * ctx_len, d)
    ng = norm_g[0][None, :]

    tn = 1024
    qkv = _in_proj(x2d, as_mod(scale, 0, batch), as_mod(shift, 0, batch), ng, w_in_bf,
                   list(range(IN_WIDTH // tn)), tn, seq, rope_tables=_rope_tables(seq))
    tnc = SSM_WIDTH
    k0 = ATT_WIDTH // tnc
    ctx_cols = list(range(k0, 3 * k0)) + [4 * ATT_WIDTH // tnc]
    kvc = _in_proj(ctx2d, as_mod(scale, batch, batch + 1), as_mod(shift, batch, batch + 1), ng, w_in_bf,
                   ctx_cols, tnc, batch * ctx_len)

    lam_vecs = jnp.stack([lambda_q1[0], lambda_k1[0], lambda_q2[0], lambda_k2[0]]).astype(F32)
    a_br = _attention(qkv, kvc, lam_vecs, subln_g[0][None, :].astype(F32), batch, seq, ctx_len)

    nd = ssm_lambda_re.shape[1]
    modes = lambda v: v[0].reshape(nd, 1, SSM_MODES)
    lstep = jnp.repeat(ssm_log_step[0], SSM_STATE, axis=-1).reshape(nd, 1, SSM_MODES)
    b_t = lambda v: jnp.tile(v[0].reshape(nd, SSM_MODES, SSM_GROUP).transpose(0, 2, 1), (1, GROUPS_PER_BLOCK, 1))
    c_t = lambda v: jnp.tile(v[0].transpose(0, 1, 3, 2).reshape(nd, SSM_MODES, SSM_GROUP), (1, 1, GROUPS_PER_BLOCK))
    a_disc, bblk, cblk = _ssm_prep(modes(ssm_lambda_re), modes(ssm_lambda_im), lstep,
                                   b_t(ssm_b_re), b_t(ssm_b_im), c_t(ssm_c_re), c_t(ssm_c_im))

    u_col = 4 * ATT_WIDTH
    u_lat = qkv[:, u_col:u_col + SSM_WIDTH].reshape(batch, seq, SSM_WIDTH).transpose(1, 0, 2)
    u_ctx = kvc[:, 2 * ATT_WIDTH:].reshape(batch, ctx_len, SSM_WIDTH).transpose(1, 0, 2)
    u_tm = jnp.concatenate([u_ctx, u_lat], axis=0).reshape((ctx_len + seq) * batch, SSM_WIDTH)
    ctx_chunks = ctx_len // SSM_CHUNK
    yf = _ssm_scan(u_tm, a_disc, bblk, cblk, 0, batch, ctx_chunks)
    y_tm = _ssm_scan(u_tm, a_disc, bblk, cblk, 1, batch, ctx_chunks, yf=yf,
                     dvec=ssm_d[0].reshape(1, SSM_WIDTH).astype(F32))
    y_bm = y_tm[ctx_len * batch:].reshape(seq, batch, SSM_WIDTH).transpose(1, 0, 2).reshape(batch * seq, SSM_WIDTH)

    out = _merge_out(a_br, y_bm, qkv, x2d, as_mod(gate, 0, batch),
                     w_glu[0].astype(BF16), b_glu[0][None, :], w_pa[0].astype(BF16), w_ps[0].astype(BF16),
                     w_out[0].astype(BF16), final_g[None, :], seq)
    return out.reshape(batch, seq, d)
```

```python
import functools
import math

import jax
import jax.numpy as jnp
from jax import lax
from jax.experimental import pallas as pl
from jax.experimental.pallas import tpu as pltpu

F32 = jnp.float32
BF16 = jnp.bfloat16

D_MODEL = 2048
GRID_W = 64
ATT_HEADS = 8
ATT_DIM = 64
HEAD_W = 2 * ATT_DIM
ATT_WIDTH = ATT_HEADS * HEAD_W
SSM_GROUP = 16
SSM_GROUPS = 32
SSM_WIDTH = SSM_GROUP * SSM_GROUPS
SSM_STATE = 64
SSM_MODES = SSM_GROUPS * SSM_STATE
IN_WIDTH = 4 * ATT_WIDTH + 2 * SSM_WIDTH + 2 * D_MODEL
ROPE_BASE = 10000.0
NORM_EPS = 1e-6
LAM_INIT = 0.8 - 0.6 * math.exp(0.0)

LANES = 128
SUBLANES = 8
VMEM_BYTES_V7X = 64 * 1024 * 1024
MIB = 1024 * 1024

GROUPS_PER_BLOCK = LANES // SSM_GROUP
SSM_BLOCKS = SSM_GROUPS // GROUPS_PER_BLOCK
BLOCK_MODES = GROUPS_PER_BLOCK * SSM_STATE
BLOCK_STATE = 2 * BLOCK_MODES
SSM_CHUNK = 128
KEY_CHUNK = 512


def _vmem_limit(nbytes):
    return int(min(nbytes + nbytes // 4, VMEM_BYTES_V7X - 8 * MIB))


def _sigmoid(v):
    return 1.0 / (1.0 + jnp.exp(-v))


def _silu(v):
    return v * _sigmoid(v)


def _adaln_kernel(c_ref, w_ref, b_ref, o_ref):
    cv = c_ref[...]
    o_ref[...] = jnp.dot(_silu(cv), w_ref[...], preferred_element_type=F32,
                         precision=lax.Precision.HIGHEST) + b_ref[...]


def _adaln(cc, w_ada, b_ada):
    rows, d = cc.shape
    n = w_ada.shape[1]
    tn = 768
    est = 2 * (d * tn * 4) + 2 * rows * tn * 4 + rows * d * 4 * 2
    return pl.pallas_call(
        _adaln_kernel,
        grid=(n // tn,),
        in_specs=[pl.BlockSpec((rows, d), lambda j: (0, 0)),
                  pl.BlockSpec((d, tn), lambda j: (0, j)),
                  pl.BlockSpec((1, tn), lambda j: (0, j))],
        out_specs=pl.BlockSpec((rows, tn), lambda j: (0, j)),
        out_shape=jax.ShapeDtypeStruct((rows, n), F32),
        compiler_params=pltpu.CompilerParams(dimension_semantics=("arbitrary",),
                                             vmem_limit_bytes=_vmem_limit(est)),
        name="adaln",
    )(cc, w_ada, b_ada)


def _in_proj_kernel(*refs, rope, norm_rows):
    if rope:
        x_ref, scale_ref, shift_ref, g_ref, w_ref, cos_ref, sina_ref, sinb_ref, o_ref, h_ref = refs
    else:
        x_ref, scale_ref, shift_ref, g_ref, w_ref, o_ref, h_ref = refs
    j = pl.program_id(1)
    tm = x_ref.shape[0]

    @pl.when(j == 0)
    def _():
        gmod = g_ref[...] * (1.0 + scale_ref[0])
        shift = shift_ref[0]

        def body(r, carry):
            r0 = pl.multiple_of(r * norm_rows, norm_rows)
            xf = x_ref[pl.ds(r0, norm_rows), :]
            ms = jnp.mean(xf * xf, axis=-1, keepdims=True)
            y = xf * lax.rsqrt(ms + NORM_EPS) * gmod + shift
            h_ref[pl.ds(r0, norm_rows), :] = y.astype(BF16)
            return carry

        lax.fori_loop(0, tm // norm_rows, body, 0)

    acc = jnp.dot(h_ref[...], w_ref[...], preferred_element_type=F32)

    if rope:
        @pl.when(j < 2)
        def _():
            qscale = jnp.where(j == 0, ATT_DIM ** -0.5 * math.log2(math.e), 1.0).astype(F32)
            cos = cos_ref[...] * qscale
            sina = sina_ref[...] * qscale
            sinb = sinb_ref[...] * qscale
            for cb in range(acc.shape[1] // HEAD_W):
                v = acc[:, cb * HEAD_W:(cb + 1) * HEAD_W]
                up = pltpu.roll(v, HEAD_W - ATT_DIM // 4, 1)
                dn = pltpu.roll(v, ATT_DIM // 4, 1)
                o_ref[:, cb * HEAD_W:(cb + 1) * HEAD_W] = (v * cos + up * sina + dn * sinb).astype(o_ref.dtype)

        @pl.when(j >= 2)
        def _():
            o_ref[...] = acc.astype(o_ref.dtype)
    else:
        o_ref[...] = acc.astype(o_ref.dtype)


def _in_proj(x2d, scale, shift, norm_g, w_bf, col_blocks, tn, rows_per_mod, rope_tables=None, tm=1024):
    m, d = x2d.shape
    nj = len(col_blocks)
    rope = rope_tables is not None
    blocks_per_mod = rows_per_mod // tm if rows_per_mod >= tm else None

    def mod_idx(i, j):
        return ((i // blocks_per_mod) if blocks_per_mod else 0, 0, 0)

    contiguous = all(cb == col_blocks[0] + k for k, cb in enumerate(col_blocks))
    if contiguous:
        w_map = lambda i, j: (0, j + col_blocks[0])
    else:
        cbs = col_blocks
        def w_map(i, j):
            idx = jnp.int32(cbs[-1])
            for k in range(nj - 2, -1, -1):
                idx = jnp.where(j == k, jnp.int32(cbs[k]), idx)
            return (0, idx)

    in_specs = [pl.BlockSpec((tm, d), lambda i, j: (i, 0)),
                pl.BlockSpec((1, 1, d), mod_idx),
                pl.BlockSpec((1, 1, d), mod_idx),
                pl.BlockSpec((1, d), lambda i, j: (0, 0)),
                pl.BlockSpec((d, tn), w_map)]
    args = [x2d, scale, shift, norm_g, w_bf]
    if rope:
        seq_blocks = rope_tables[0].shape[0] // tm
        for t in rope_tables:
            in_specs.append(pl.BlockSpec((tm, HEAD_W), lambda i, j: (i % seq_blocks, 0)))
            args.append(t)
    est = (2 * tm * d * 4 + tm * d * 2 + 2 * d * tn * 2 + 2 * tm * tn * 2 + 2 * tm * tn * 4
           + (6 * tm * HEAD_W * 4 if rope else 0))
    return pl.pallas_call(
        functools.partial(_in_proj_kernel, rope=rope, norm_rows=256),
        grid=(m // tm, nj),
        in_specs=in_specs,
        out_specs=pl.BlockSpec((tm, tn), lambda i, j: (i, j)),
        out_shape=jax.ShapeDtypeStruct((m, nj * tn), BF16),
        scratch_shapes=[pltpu.VMEM((tm, d), BF16)],
        compiler_params=pltpu.CompilerParams(dimension_semantics=("arbitrary", "arbitrary"),
                                             vmem_limit_bytes=_vmem_limit(est)),
        name="in_proj_rope" if rope else "in_proj_ctx",
    )(*args)


def _rope_tables(seq):
    rows = seq // GRID_W
    row = jnp.repeat(jnp.arange(rows), GRID_W).astype(F32)
    col = jnp.tile(jnp.arange(GRID_W), rows).astype(F32)
    nf = ATT_DIM // 4
    inv = ROPE_BASE ** (-jnp.arange(nf, dtype=F32) / nf)

    def axis_tables(pos):
        ang = pos[:, None] * inv[None, :]
        c, s = jnp.cos(ang), jnp.sin(ang)
        z = jnp.zeros_like(s)
        return (jnp.concatenate([c, c], -1), jnp.concatenate([-s, z], -1), jnp.concatenate([z, s], -1))

    parts = [axis_tables(row), axis_tables(col)]
    out = []
    for k in range(3):
        comp = jnp.concatenate([parts[0][k], parts[1][k]], -1)
        out.append(jnp.concatenate([comp, comp], -1))
    return out


def _attn_kernel(q_ref, k_ref, v_ref, kc_ref, vc_ref, ga_ref, lam_ref, g_ref, o_ref,
                 s_scr, m_scr, p_scr, a_scr, c_scr, *, rows):
    nt = (((1,), (1,)), ((), ()))
    seq = k_ref.shape[0]
    nblk = q_ref.shape[0] // rows
    lamv = lam_ref[...]
    lam = (jnp.exp(jnp.sum(lamv[0:1] * lamv[1:2], axis=-1, keepdims=True))
           - jnp.exp(jnp.sum(lamv[2:3] * lamv[3:4], axis=-1, keepdims=True)) + LAM_INIT)
    lane = lax.broadcasted_iota(jnp.int32, (rows, HEAD_W), 1)

    keys = seq + kc_ref.shape[0]
    key_chunks = [(k_ref, c0, c0, KEY_CHUNK) for c0 in range(0, seq, KEY_CHUNK)]
    key_chunks += [(kc_ref, c0, seq + c0, min(KEY_CHUNK, keys - seq - c0)) for c0 in range(0, keys - seq, KEY_CHUNK)]

    def qk(i, slot):
        q = q_ref[pl.ds(pl.multiple_of(i * rows, rows), rows), :]
        for comp in range(2):
            qc = jnp.where((lane >= ATT_DIM) == bool(comp), q, jnp.zeros_like(q))
            m = None
            for kref, src, dst, width in key_chunks:
                s = lax.dot_general(qc, kref[src:src + width, :], nt, preferred_element_type=F32)
                s_scr[slot, comp, :, dst:dst + width] = s
                for cb in range(width // LANES):
                    blk = s[:, cb * LANES:(cb + 1) * LANES]
                    m = blk if m is None else jnp.maximum(m, blk)
            m_scr[slot, comp] = jnp.broadcast_to(jnp.max(m, axis=-1, keepdims=True), (rows, LANES))

    def sm(slot):
        sums = []
        for comp in range(2):
            m = m_scr[slot, comp]
            acc = None
            for c0 in range(0, keys, LANES):
                p = jnp.exp2(s_scr[slot, comp, :, c0:c0 + LANES] - m)
                p_scr[comp, :, c0:c0 + LANES] = p.astype(BF16)
                acc = p if acc is None else acc + p
            sums.append(jnp.sum(acc, axis=-1, keepdims=True))
        c1 = 1.0 / sums[0]
        r = jnp.broadcast_to(lam * sums[0] / sums[1], (rows, LANES)).astype(BF16)
        for c0 in range(0, keys, LANES):
            a_scr[slot, :, c0:c0 + LANES] = p_scr[0, :, c0:c0 + LANES] - r * p_scr[1, :, c0:c0 + LANES]
        c_scr[slot] = jnp.broadcast_to(c1, (rows, LANES))

    def pv(i, slot):
        r0 = pl.multiple_of(i * rows, rows)
        o = (jnp.dot(a_scr[slot, :, 0:seq], v_ref[...], preferred_element_type=F32)
             + jnp.dot(a_scr[slot, :, seq:], vc_ref[...], preferred_element_type=F32)) * c_scr[slot]
        ms = jnp.mean(o * o, axis=-1, keepdims=True)
        on = o * lax.rsqrt(ms + NORM_EPS) * g_ref[...] * (1.0 - LAM_INIT)
        o_ref[pl.ds(r0, rows), :] = (on * _silu(ga_ref[pl.ds(r0, rows), :].astype(F32))).astype(o_ref.dtype)

    qk(0, 0)
    qk(1, 1)
    sm(0)

    def body(t, carry):
        i = 2 * t
        qk(i + 2, 0)
        sm(1)
        pv(i, 0)
        qk(i + 3, 1)
        sm(0)
        pv(i + 1, 1)
        return carry

    lax.fori_loop(0, (nblk - 2) // 2, body, 0)
    sm(1)
    pv(nblk - 2, 0)
    pv(nblk - 1, 1)


def _attention(qkv, kvc, lam_vecs, subln_g, batch, seq, ctx_len, rows=256):
    kcol = ATT_WIDTH // HEAD_W
    keys = seq + ctx_len
    assert (seq // rows) % 2 == 0 and seq // rows >= 4
    est = (2 * 2 * rows * keys * 4 + 2 * rows * keys * 2 + 6 * rows * LANES * 4
           + 2 * (4 * seq + 2 * ctx_len) * HEAD_W * 2 + 4 * rows * keys * 4)
    return pl.pallas_call(
        functools.partial(_attn_kernel, rows=rows),
        grid=(batch, ATT_HEADS),
        in_specs=[pl.BlockSpec((seq, HEAD_W), lambda b, h: (b, h)),
                  pl.BlockSpec((seq, HEAD_W), lambda b, h: (b, kcol + h)),
                  pl.BlockSpec((seq, HEAD_W), lambda b, h: (b, 2 * kcol + h)),
                  pl.BlockSpec((ctx_len, HEAD_W), lambda b, h: (b, h)),
                  pl.BlockSpec((ctx_len, HEAD_W), lambda b, h: (b, kcol + h)),
                  pl.BlockSpec((seq, HEAD_W), lambda b, h: (b, 3 * kcol + h)),
                  pl.BlockSpec((4, ATT_DIM), lambda b, h: (0, 0)),
                  pl.BlockSpec((1, HEAD_W), lambda b, h: (0, 0))],
        out_specs=pl.BlockSpec((seq, HEAD_W), lambda b, h: (b, h)),
        out_shape=jax.ShapeDtypeStruct((batch * seq, ATT_WIDTH), BF16),
        scratch_shapes=[pltpu.VMEM((2, 2, rows, keys), F32), pltpu.VMEM((2, 2, rows, LANES), F32),
                        pltpu.VMEM((2, rows, keys), BF16), pltpu.VMEM((2, rows, keys), BF16),
                        pltpu.VMEM((2, rows, LANES), F32)],
        compiler_params=pltpu.CompilerParams(dimension_semantics=("arbitrary",) * 2,
                                             vmem_limit_bytes=_vmem_limit(est)),
        name="diff_attention",
    )(qkv, qkv, qkv, kvc, kvc, qkv, lam_vecs, subln_g)


def _ssm_prep_kernel(lre_ref, lim_ref, lstep_ref, bre_ref, bim_ref, cre_ref, cim_ref,
                     a_ref, bblk_ref, cblk_ref):
    lre = lre_ref[0]
    lim = lim_ref[0]
    dt = jnp.exp(lstep_ref[0])
    mag = jnp.exp(lre * dt)
    ar = mag * jnp.cos(lim * dt)
    ai = mag * jnp.sin(lim * dt)
    den = lre * lre + lim * lim
    fr = ((ar - 1.0) * lre + ai * lim) / den
    fi = (ai * lre - (ar - 1.0) * lim) / den
    a_ref[0, 0:1, :] = ar
    a_ref[0, 1:2, :] = ai
    br = bre_ref[0]
    bi = bim_ref[0]
    bbr = fr * br - fi * bi
    bbi = fr * bi + fi * br
    row_g = lax.broadcasted_iota(jnp.int32, (LANES, BLOCK_MODES), 0) // SSM_GROUP
    col_g = lax.broadcasted_iota(jnp.int32, (LANES, BLOCK_MODES), 1) // SSM_STATE
    keep_b = row_g == col_g
    row_g2 = lax.broadcasted_iota(jnp.int32, (BLOCK_MODES, LANES), 0) // SSM_STATE
    col_g2 = lax.broadcasted_iota(jnp.int32, (BLOCK_MODES, LANES), 1) // SSM_GROUP
    keep_c = row_g2 == col_g2
    for k in range(SSM_BLOCKS):
        sl = slice(k * BLOCK_MODES, (k + 1) * BLOCK_MODES)
        bblk_ref[0, k, :, 0:BLOCK_MODES] = jnp.where(keep_b, bbr[:, sl], 0.0).astype(BF16)
        bblk_ref[0, k, :, BLOCK_MODES:BLOCK_STATE] = jnp.where(keep_b, bbi[:, sl], 0.0).astype(BF16)
        cblk_ref[0, k, 0:BLOCK_MODES, :] = jnp.where(keep_c, cre_ref[0, sl, :], 0.0).astype(BF16)
        cblk_ref[0, k, BLOCK_MODES:BLOCK_STATE, :] = jnp.where(keep_c, -cim_ref[0, sl, :], 0.0).astype(BF16)


def _ssm_prep(lre, lim, lstep, bre_t, bim_t, cre_t, cim_t):
    nd = lre.shape[0]
    vec = pl.BlockSpec((1, 1, SSM_MODES), lambda d: (d, 0, 0))
    bsp = pl.BlockSpec((1, LANES, SSM_MODES), lambda d: (d, 0, 0))
    csp = pl.BlockSpec((1, SSM_MODES, LANES), lambda d: (d, 0, 0))
    est = 2 * (4 * LANES * SSM_MODES * 4 + 2 * SSM_BLOCKS * LANES * BLOCK_STATE * 2) + 8 * LANES * SSM_MODES * 4
    return pl.pallas_call(
        _ssm_prep_kernel,
        grid=(nd,),
        in_specs=[vec, vec, vec, bsp, bsp, csp, csp],
        out_specs=[pl.BlockSpec((1, 2, SSM_MODES), lambda d: (d, 0, 0)),
                   pl.BlockSpec((1, SSM_BLOCKS, LANES, BLOCK_STATE), lambda d: (d, 0, 0, 0)),
                   pl.BlockSpec((1, SSM_BLOCKS, BLOCK_STATE, LANES), lambda d: (d, 0, 0, 0))],
        out_shape=[jax.ShapeDtypeStruct((nd, 2, SSM_MODES), F32),
                   jax.ShapeDtypeStruct((nd, SSM_BLOCKS, LANES, BLOCK_STATE), BF16),
                   jax.ShapeDtypeStruct((nd, SSM_BLOCKS, BLOCK_STATE, LANES), BF16)],
        compiler_params=pltpu.CompilerParams(dimension_semantics=("arbitrary",),
                                             vmem_limit_bytes=_vmem_limit(est)),
        name="ssm_prep",
    )(lre, lim, lstep, bre_t, bim_t, cre_t, cim_t)


def _ssm_scan_kernel(*refs, reverse, batch):
    if reverse:
        u_ref, a_ref, bblk_ref, cblk_ref, yf_ref, d_ref, y_ref, s_ref, h_ref = refs
    else:
        u_ref, a_ref, bblk_ref, cblk_ref, y_ref, s_ref, h_ref = refs
    steps = u_ref.shape[0] // batch

    @pl.when(pl.program_id(0) == 0)
    def _():
        h_ref[...] = jnp.zeros_like(h_ref)

    u = u_ref[...]
    for k in range(SSM_BLOCKS):
        s_ref[:, k * BLOCK_STATE:(k + 1) * BLOCK_STATE] = jnp.dot(
            u[:, k * LANES:(k + 1) * LANES], bblk_ref[0, k], preferred_element_type=F32)

    for k in range(SSM_BLOCKS):
        c_re = k * BLOCK_STATE
        c_im = c_re + BLOCK_MODES
        ar = jnp.broadcast_to(a_ref[0, 0:1, k * BLOCK_MODES:(k + 1) * BLOCK_MODES], (batch, BLOCK_MODES))
        ai = jnp.broadcast_to(a_ref[0, 1:2, k * BLOCK_MODES:(k + 1) * BLOCK_MODES], (batch, BLOCK_MODES))

        def body(i, carry, c_re=c_re, c_im=c_im, ar=ar, ai=ai):
            hr, hi = carry
            t = (steps - 1 - i) if reverse else i
            r0 = pl.multiple_of(t * batch, batch)
            nhr = ar * hr - ai * hi + s_ref[pl.ds(r0, batch), c_re:c_re + BLOCK_MODES]
            nhi = ar * hi + ai * hr + s_ref[pl.ds(r0, batch), c_im:c_im + BLOCK_MODES]
            s_ref[pl.ds(r0, batch), c_re:c_re + BLOCK_MODES] = nhr
            s_ref[pl.ds(r0, batch), c_im:c_im + BLOCK_MODES] = nhi
            return nhr, nhi

        hr, hi = lax.fori_loop(0, steps, body,
                               (h_ref[:, c_re:c_re + BLOCK_MODES], h_ref[:, c_im:c_im + BLOCK_MODES]),
                               unroll=2)
        h_ref[:, c_re:c_re + BLOCK_MODES] = hr
        h_ref[:, c_im:c_im + BLOCK_MODES] = hi

    for k in range(SSM_BLOCKS):
        yk = jnp.dot(s_ref[:, k * BLOCK_STATE:(k + 1) * BLOCK_STATE].astype(BF16), cblk_ref[0, k],
                     preferred_element_type=F32)
        cols = slice(k * LANES, (k + 1) * LANES)
        if reverse:
            yk = yk + yf_ref[:, cols] + d_ref[:, cols] * u[:, cols].astype(F32)
        y_ref[:, cols] = yk


def _ssm_scan(u_tm, a, bblk, cblk, direction, batch, ctx_chunks, yf=None, dvec=None):
    rows = SSM_CHUNK * batch
    chunks = u_tm.shape[0] // rows
    reverse = direction == 1
    if reverse:
        last = chunks - 1
        cmap = lambda s: (jnp.where(s < ctx_chunks, ctx_chunks - 1 - s, last + ctx_chunks - s), 0)
    else:
        cmap = lambda s: (s, 0)
    in_specs = [pl.BlockSpec((rows, SSM_WIDTH), cmap),
                pl.BlockSpec((1, 2, SSM_MODES), lambda s: (direction, 0, 0)),
                pl.BlockSpec((1, SSM_BLOCKS, LANES, BLOCK_STATE), lambda s: (direction, 0, 0, 0)),
                pl.BlockSpec((1, SSM_BLOCKS, BLOCK_STATE, LANES), lambda s: (direction, 0, 0, 0))]
    args = [u_tm, a, bblk, cblk]
    if reverse:
        in_specs += [pl.BlockSpec((rows, SSM_WIDTH), cmap), pl.BlockSpec((1, SSM_WIDTH), lambda s: (0, 0))]
        args += [yf, dvec]
    state_w = SSM_BLOCKS * BLOCK_STATE
    est = (rows * state_w * 4 + 2 * rows * state_w // SSM_BLOCKS * 4 + 6 * rows * SSM_WIDTH * 4
           + 4 * SSM_BLOCKS * LANES * BLOCK_STATE * 2)
    return pl.pallas_call(
        functools.partial(_ssm_scan_kernel, reverse=reverse, batch=batch),
        grid=(chunks,),
        in_specs=in_specs,
        out_specs=pl.BlockSpec((rows, SSM_WIDTH), cmap),
        out_shape=jax.ShapeDtypeStruct((chunks * rows, SSM_WIDTH), F32),
        scratch_shapes=[pltpu.VMEM((rows, state_w), F32), pltpu.VMEM((batch, state_w), F32)],
        compiler_params=pltpu.CompilerParams(dimension_semantics=("arbitrary",),
                                             vmem_limit_bytes=_vmem_limit(est)),
        name="ssm_scan_bwd" if reverse else "ssm_scan_fwd",
    )(*args)


def _gelu_tanh(v):
    return 0.5 * v * (1.0 + jnp.tanh(math.sqrt(2.0 / math.pi) * (v + 0.044715 * v * v * v)))


def _merge_kernel(a_ref, y_ref, gs_ref, gma0_ref, gma1_ref, gms0_ref, gms1_ref, x_ref, gate_ref,
                  wglu_ref, bglu_ref, wpa_ref, wps_ref, wout_ref, fg_ref, o_ref):
    yg = _gelu_tanh(y_ref[...])
    z = yg * _sigmoid(jnp.dot(yg.astype(BF16), wglu_ref[...], preferred_element_type=F32) + bglu_ref[...])
    s_br = (z * _silu(gs_ref[...].astype(F32))).astype(BF16)
    ta = jnp.dot(a_ref[...], wpa_ref[...], preferred_element_type=F32)
    ts = jnp.dot(s_br, wps_ref[...], preferred_element_type=F32)
    half = D_MODEL // 2
    parts = []
    for p, (ga_r, gs_r) in enumerate(((gma0_ref, gms0_ref), (gma1_ref, gms1_ref))):
        cols = slice(p * half, (p + 1) * half)
        parts.append((_sigmoid(ga_r[...].astype(F32)) * ta[:, cols]
                      + _sigmoid(gs_r[...].astype(F32)) * ts[:, cols]).astype(BF16))
    t = jnp.concatenate(parts, axis=1)
    out = jnp.dot(t, wout_ref[...], preferred_element_type=F32)
    xn = x_ref[...] + gate_ref[0] * out
    ms = jnp.mean(xn * xn, axis=-1, keepdims=True)
    o_ref[...] = xn * lax.rsqrt(ms + NORM_EPS) * fg_ref[...]


def _merge_out(a_br, y_bm, qkv, x2d, gate, w_glu, b_glu, w_pa, w_ps, w_out, final_g, seq, tm=256):
    m, d = x2d.shape
    half = d // 2
    per_batch = seq // tm
    gs_col = (4 * ATT_WIDTH + SSM_WIDTH) // SSM_WIDTH
    gm_col = (4 * ATT_WIDTH + 2 * SSM_WIDTH) // half
    const = lambda i: (0, 0)
    est = (2 * (tm * ATT_WIDTH * 2 + tm * SSM_WIDTH * 6 + 4 * tm * half * 2 + 2 * tm * d * 4)
           + 2 * (SSM_WIDTH * SSM_WIDTH + ATT_WIDTH * d + SSM_WIDTH * d + d * d) * 2 + 6 * tm * d * 4)
    return pl.pallas_call(
        _merge_kernel,
        grid=(m // tm,),
        in_specs=[pl.BlockSpec((tm, ATT_WIDTH), lambda i: (i, 0)),
                  pl.BlockSpec((tm, SSM_WIDTH), lambda i: (i, 0)),
                  pl.BlockSpec((tm, SSM_WIDTH), lambda i: (i, gs_col)),
                  pl.BlockSpec((tm, half), lambda i: (i, gm_col)),
                  pl.BlockSpec((tm, half), lambda i: (i, gm_col + 1)),
                  pl.BlockSpec((tm, half), lambda i: (i, gm_col + 2)),
                  pl.BlockSpec((tm, half), lambda i: (i, gm_col + 3)),
                  pl.BlockSpec((tm, d), lambda i: (i, 0)),
                  pl.BlockSpec((1, 1, d), lambda i: (i // per_batch, 0, 0)),
                  pl.BlockSpec((SSM_WIDTH, SSM_WIDTH), const),
                  pl.BlockSpec((1, SSM_WIDTH), const),
                  pl.BlockSpec((ATT_WIDTH, d), const),
                  pl.BlockSpec((SSM_WIDTH, d), const),
                  pl.BlockSpec((d, d), const),
                  pl.BlockSpec((1, d), const)],
        out_specs=pl.BlockSpec((tm, d), lambda i: (i, 0)),
        out_shape=jax.ShapeDtypeStruct((m, d), F32),
        compiler_params=pltpu.CompilerParams(dimension_semantics=("arbitrary",),
                                             vmem_limit_bytes=_vmem_limit(est)),
        name="merge_out",
    )(a_br, y_bm, qkv, qkv, qkv, qkv, qkv, x2d, gate, w_glu, b_glu, w_pa, w_ps, w_out, final_g)


def kernel(x, c, ctx, c_ctx, w_ada, b_ada, norm_g, w_in, lambda_q1, lambda_k1, lambda_q2, lambda_k2,
           subln_g, ssm_lambda_re, ssm_lambda_im, ssm_log_step, ssm_b_re, ssm_b_im, ssm_c_re, ssm_c_im,
           ssm_d, w_glu, b_glu, w_pa, w_ps, w_out, final_g):
    batch, seq, d = x.shape
    ctx_len = ctx.shape[1]
    assert w_ada.shape[0] == 1, "single-layer trunk"
    assert (d, seq % 1024, ctx_len % SSM_CHUNK, seq % SSM_CHUNK) == (D_MODEL, 0, 0, 0)

    pad = (-(batch + 1)) % SUBLANES
    cc = jnp.concatenate([c, c_ctx[None, :], jnp.zeros((pad, d), F32)], axis=0)
    mod = _adaln(cc, w_ada[0], b_ada[0][None, :])
    shift, scale, gate = (mod[:, k * d:(k + 1) * d] for k in range(3))
    as_mod = lambda v, lo, hi: v[lo:hi].reshape(hi - lo, 1, d)

    w_in_bf = w_in[0].astype(BF16)
    x2d = x.reshape(batch * seq, d)
    ctx2d = ctx.reshape(batch * ctx_len, d)
    ng = norm_g[0][None, :]

    tn = 1024
    qkv = _in_proj(x2d, as_mod(scale, 0, batch), as_mod(shift, 0, batch), ng, w_in_bf,
                   list(range(IN_WIDTH // tn)), tn, seq, rope_tables=_rope_tables(seq))
    tnc = SSM_WIDTH
    k0 = ATT_WIDTH // tnc
    ctx_cols = list(range(k0, 3 * k0)) + [4 * ATT_WIDTH // tnc]
    kvc = _in_proj(ctx2d, as_mod(scale, batch, batch + 1), as_mod(shift, batch, batch + 1), ng, w_in_bf,
                   ctx_cols, tnc, batch * ctx_len)

    lam_vecs = jnp.stack([lambda_q1[0], lambda_k1[0], lambda_q2[0], lambda_k2[0]]).astype(F32)
    a_br = _attention(qkv, kvc, lam_vecs, subln_g[0][None, :].astype(F32), batch, seq, ctx_len)

    nd = ssm_lambda_re.shape[1]
    modes = lambda v: v[0].reshape(nd, 1, SSM_MODES)
    lstep = jnp.repeat(ssm_log_step[0], SSM_STATE, axis=-1).reshape(nd, 1, SSM_MODES)
    b_t = lambda v: jnp.tile(v[0].reshape(nd, SSM_MODES, SSM_GROUP).transpose(0, 2, 1), (1, GROUPS_PER_BLOCK, 1))
    c_t = lambda v: jnp.tile(v[0].transpose(0, 1, 3, 2).reshape(nd, SSM_MODES, SSM_GROUP), (1, 1, GROUPS_PER_BLOCK))
    a_disc, bblk, cblk = _ssm_prep(modes(ssm_lambda_re), modes(ssm_lambda_im), lstep,
                                   b_t(ssm_b_re), b_t(ssm_b_im), c_t(ssm_c_re), c_t(ssm_c_im))

    u_col = 4 * ATT_WIDTH
    u_lat = qkv[:, u_col:u_col + SSM_WIDTH].reshape(batch, seq, SSM_WIDTH).transpose(1, 0, 2)
    u_ctx = kvc[:, 2 * ATT_WIDTH:].reshape(batch, ctx_len, SSM_WIDTH).transpose(1, 0, 2)
    u_tm = jnp.concatenate([u_ctx, u_lat], axis=0).reshape((ctx_len + seq) * batch, SSM_WIDTH)
    ctx_chunks = ctx_len // SSM_CHUNK
    yf = _ssm_scan(u_tm, a_disc, bblk, cblk, 0, batch, ctx_chunks)
    y_tm = _ssm_scan(u_tm, a_disc, bblk, cblk, 1, batch, ctx_chunks, yf=yf,
                     dvec=ssm_d[0].reshape(1, SSM_WIDTH).astype(F32))
    y_bm = y_tm[ctx_len * batch:].reshape(seq, batch, SSM_WIDTH).transpose(1, 0, 2).reshape(batch * seq, SSM_WIDTH)

    out = _merge_out(a_br, y_bm, qkv, x2d, as_mod(gate, 0, batch),
                     w_glu[0].astype(BF16), b_glu[0][None, :], w_pa[0].astype(BF16), w_ps[0].astype(BF16),
                     w_out[0].astype(BF16), final_g[None, :], seq)
    return out.reshape(batch, seq, d)
```

```python
import functools
import math

import jax
import jax.numpy as jnp
import numpy as np
from jax import lax
from jax.experimental import pallas as pl
from jax.experimental.pallas import tpu as pltpu

F32 = jnp.float32
BF16 = jnp.bfloat16

D_MODEL = 2048
GRID_W = 64
ATT_HEADS = 8
ATT_DIM = 64
HEAD_W = 2 * ATT_DIM
ATT_WIDTH = ATT_HEADS * HEAD_W
SSM_GROUP = 16
SSM_GROUPS = 32
SSM_WIDTH = SSM_GROUP * SSM_GROUPS
SSM_STATE = 64
SSM_MODES = SSM_GROUPS * SSM_STATE
IN_WIDTH = 4 * ATT_WIDTH + 2 * SSM_WIDTH + 2 * D_MODEL
ROPE_BASE = 10000.0
NORM_EPS = 1e-6
LAM_INIT = 0.8 - 0.6 * math.exp(0.0)

LANES = 128
SUBLANES = 8
VMEM_BYTES_V7X = 64 * 1024 * 1024
MIB = 1024 * 1024

GROUPS_PER_BLOCK = LANES // SSM_GROUP
SSM_BLOCKS = SSM_GROUPS // GROUPS_PER_BLOCK
BLOCK_MODES = GROUPS_PER_BLOCK * SSM_STATE
BLOCK_STATE = 2 * BLOCK_MODES
SSM_CHUNK = 128
KEY_CHUNK = 512


def _vmem_limit(nbytes):
    return int(min(nbytes + nbytes // 4, VMEM_BYTES_V7X - 8 * MIB))


def _sigmoid(v):
    return 1.0 / (1.0 + jnp.exp(-v))


def _silu(v):
    return v * _sigmoid(v)


def _adaln_kernel(c_ref, w_ref, b_ref, o_ref):
    cv = c_ref[...]
    o_ref[...] = jnp.dot(_silu(cv), w_ref[...], preferred_element_type=F32,
                         precision=lax.Precision.HIGHEST) + b_ref[...]


def _adaln(cc, w_ada, b_ada):
    rows, d = cc.shape
    n = w_ada.shape[1]
    tn = 768
    est = 2 * (d * tn * 4) + 2 * rows * tn * 4 + rows * d * 4 * 2
    return pl.pallas_call(
        _adaln_kernel,
        grid=(n // tn,),
        in_specs=[pl.BlockSpec((rows, d), lambda j: (0, 0)),
                  pl.BlockSpec((d, tn), lambda j: (0, j)),
                  pl.BlockSpec((1, tn), lambda j: (0, j))],
        out_specs=pl.BlockSpec((rows, tn), lambda j: (0, j)),
        out_shape=jax.ShapeDtypeStruct((rows, n), F32),
        compiler_params=pltpu.CompilerParams(dimension_semantics=("arbitrary",),
                                             vmem_limit_bytes=_vmem_limit(est)),
        name="adaln",
    )(cc, w_ada, b_ada)


def _in_proj_kernel(*refs, rope, norm_rows, u_tile, u_off, seq_rows):
    if rope:
        x_ref, scale_ref, shift_ref, g_ref, w_ref, cos_ref, sina_ref, sinb_ref, o_ref, u_ref, h_ref = refs
    else:
        x_ref, scale_ref, shift_ref, g_ref, w_ref, o_ref, u_ref, h_ref = refs
    j = pl.program_id(1)
    tm = x_ref.shape[0]

    @pl.when(j == 0)
    def _():
        gmod = g_ref[...] * (1.0 + scale_ref[0])
        shift = shift_ref[0]

        def body(r, carry):
            r0 = pl.multiple_of(r * norm_rows, norm_rows)
            xf = x_ref[pl.ds(r0, norm_rows), :]
            ms = jnp.mean(xf * xf, axis=-1, keepdims=True)
            y = xf * lax.rsqrt(ms + NORM_EPS) * gmod + shift
            h_ref[pl.ds(r0, norm_rows), :] = y.astype(BF16)
            return carry

        lax.fori_loop(0, tm // norm_rows, body, 0)

    acc = jnp.dot(h_ref[...], w_ref[...], preferred_element_type=F32)

    if rope:
        is_rot = j < 2
        qscale = jnp.where(j == 0, ATT_DIM ** -0.5 * math.log2(math.e), 1.0).astype(F32)
        cos = jnp.where(is_rot, cos_ref[...] * qscale, 1.0)
        sina = jnp.where(is_rot, sina_ref[...] * qscale, 0.0)
        sinb = jnp.where(is_rot, sinb_ref[...] * qscale, 0.0)
        for cb in range(acc.shape[1] // HEAD_W):
            v = acc[:, cb * HEAD_W:(cb + 1) * HEAD_W]
            up = pltpu.roll(v, HEAD_W - ATT_DIM // 4, 1)
            dn = pltpu.roll(v, ATT_DIM // 4, 1)
            o_ref[:, cb * HEAD_W:(cb + 1) * HEAD_W] = (v * cos + up * sina + dn * sinb).astype(o_ref.dtype)
    else:
        o_ref[...] = acc.astype(o_ref.dtype)

    @pl.when(j == u_tile)
    def _():
        for bb in range(tm // seq_rows):
            u_ref[:, bb * SSM_WIDTH:(bb + 1) * SSM_WIDTH] = (
                o_ref[bb * seq_rows:(bb + 1) * seq_rows, u_off:u_off + SSM_WIDTH])


def _in_proj(x2d, scale, shift, norm_g, w_bf, col_blocks, tn, seq_len, u_tile, u_off, rope_tables=None, tm=1024):
    m, d = x2d.shape
    nj = len(col_blocks)
    rope = rope_tables is not None
    seq_blocks = max(seq_len // tm, 1)
    seq_rows = min(seq_len, tm)
    per_batch_mod = scale.shape[0] > 1

    def mod_idx(i, j):
        return ((i // seq_blocks) if per_batch_mod else 0, 0, 0)

    contiguous = all(cb == col_blocks[0] + k for k, cb in enumerate(col_blocks))
    if contiguous:
        w_map = lambda i, j: (0, j + col_blocks[0])
    else:
        cbs = col_blocks
        def w_map(i, j):
            idx = jnp.int32(cbs[-1])
            for k in range(nj - 2, -1, -1):
                idx = jnp.where(j == k, jnp.int32(cbs[k]), idx)
            return (0, idx)

    in_specs = [pl.BlockSpec((tm, d), lambda i, j: (i, 0)),
                pl.BlockSpec((1, 1, d), mod_idx),
                pl.BlockSpec((1, 1, d), mod_idx),
                pl.BlockSpec((1, d), lambda i, j: (0, 0)),
                pl.BlockSpec((d, tn), w_map)]
    args = [x2d, scale, shift, norm_g, w_bf]
    if rope:
        for t in rope_tables:
            in_specs.append(pl.BlockSpec((tm, HEAD_W), lambda i, j: (i % seq_blocks, 0)))
            args.append(t)
    u_block = (seq_rows, (tm // seq_rows) * SSM_WIDTH)
    est = (2 * tm * d * 4 + tm * d * 2 + 2 * d * tn * 2 + 2 * tm * tn * 2 + 2 * tm * tn * 4
           + 2 * u_block[0] * u_block[1] * 2 + (6 * tm * HEAD_W * 4 if rope else 0))
    return pl.pallas_call(
        functools.partial(_in_proj_kernel, rope=rope, norm_rows=256, u_tile=u_tile, u_off=u_off,
                          seq_rows=seq_rows),
        grid=(m // tm, nj),
        in_specs=in_specs,
        out_specs=[pl.BlockSpec((tm, tn), lambda i, j: (i, j)),
                   pl.BlockSpec(u_block, lambda i, j: (i % seq_blocks, i // seq_blocks))],
        out_shape=[jax.ShapeDtypeStruct((m, nj * tn), BF16),
                   jax.ShapeDtypeStruct((seq_len, (m // seq_len) * SSM_WIDTH), BF16)],
        scratch_shapes=[pltpu.VMEM((tm, d), BF16)],
        compiler_params=pltpu.CompilerParams(dimension_semantics=("arbitrary", "arbitrary"),
                                             vmem_limit_bytes=_vmem_limit(est)),
        name="in_proj_rope" if rope else "in_proj_ctx",
    )(*args)


def _rope_tables(seq):
    rows = seq // GRID_W
    row = np.repeat(np.arange(rows), GRID_W).astype(np.float64)
    col = np.tile(np.arange(GRID_W), rows).astype(np.float64)
    nf = ATT_DIM // 4
    inv = ROPE_BASE ** (-np.arange(nf, dtype=np.float64) / nf)

    def axis_tables(pos):
        ang = pos[:, None] * inv[None, :]
        c, s = np.cos(ang), np.sin(ang)
        z = np.zeros_like(s)
        return (np.concatenate([c, c], -1), np.concatenate([-s, z], -1), np.concatenate([z, s], -1))

    parts = [axis_tables(row), axis_tables(col)]
    out = []
    for k in range(3):
        comp = np.concatenate([parts[0][k], parts[1][k]], -1)
        out.append(jnp.asarray(np.concatenate([comp, comp], -1), dtype=F32))
    return out


def _attn_kernel(q_ref, k_ref, v_ref, kc_ref, vc_ref, ga_ref, lam_ref, g_ref, o_ref,
                 s_scr, m_scr, p_scr, a_scr, c_scr, *, rows):
    nt = (((1,), (1,)), ((), ()))
    seq = k_ref.shape[0]
    nblk = q_ref.shape[0] // rows
    lamv = lam_ref[...]
    lam = (jnp.exp(jnp.sum(lamv[0:1] * lamv[1:2], axis=-1, keepdims=True))
           - jnp.exp(jnp.sum(lamv[2:3] * lamv[3:4], axis=-1, keepdims=True)) + LAM_INIT)
    lane = lax.broadcasted_iota(jnp.int32, (rows, HEAD_W), 1)

    keys = seq + kc_ref.shape[0]
    key_chunks = [(k_ref, c0, c0, KEY_CHUNK) for c0 in range(0, seq, KEY_CHUNK)]
    key_chunks += [(kc_ref, c0, seq + c0, min(KEY_CHUNK, keys - seq - c0)) for c0 in range(0, keys - seq, KEY_CHUNK)]

    def qk(i, slot):
        q = q_ref[pl.ds(pl.multiple_of(i * rows, rows), rows), :]
        for comp in range(2):
            qc = jnp.where((lane >= ATT_DIM) == bool(comp), q, jnp.zeros_like(q))
            m = None
            for kref, src, dst, width in key_chunks:
                s = lax.dot_general(qc, kref[src:src + width, :], nt, preferred_element_type=F32)
                s_scr[slot, comp, :, dst:dst + width] = s
                for cb in range(width // LANES):
                    blk = s[:, cb * LANES:(cb + 1) * LANES]
                    m = blk if m is None else jnp.maximum(m, blk)
            m_scr[slot, comp] = jnp.broadcast_to(jnp.max(m, axis=-1, keepdims=True), (rows, LANES))

    def sm(slot):
        sums = []
        for comp in range(2):
            m = m_scr[slot, comp]
            acc = None
            for c0 in range(0, keys, LANES):
                p = jnp.exp2(s_scr[slot, comp, :, c0:c0 + LANES] - m)
                p_scr[comp, :, c0:c0 + LANES] = p.astype(BF16)
                acc = p if acc is None else acc + p
            sums.append(jnp.sum(acc, axis=-1, keepdims=True))
        c1 = 1.0 / sums[0]
        r = jnp.broadcast_to(lam * sums[0] / sums[1], (rows, LANES)).astype(BF16)
        for c0 in range(0, keys, LANES):
            a_scr[slot, :, c0:c0 + LANES] = p_scr[0, :, c0:c0 + LANES] - r * p_scr[1, :, c0:c0 + LANES]
        c_scr[slot] = jnp.broadcast_to(c1, (rows, LANES))

    def pv(i, slot):
        r0 = pl.multiple_of(i * rows, rows)
        o = (jnp.dot(a_scr[slot, :, 0:seq], v_ref[...], preferred_element_type=F32)
             + jnp.dot(a_scr[slot, :, seq:], vc_ref[...], preferred_element_type=F32)) * c_scr[slot]
        ms = jnp.mean(o * o, axis=-1, keepdims=True)
        on = o * lax.rsqrt(ms + NORM_EPS) * g_ref[...] * (1.0 - LAM_INIT)
        o_ref[pl.ds(r0, rows), :] = (on * _silu(ga_ref[pl.ds(r0, rows), :].astype(F32))).astype(o_ref.dtype)

    qk(0, 0)
    qk(1, 1)
    sm(0)

    def body(t, carry):
        i = 2 * t
        qk(i + 2, 0)
        sm(1)
        pv(i, 0)
        qk(i + 3, 1)
        sm(0)
        pv(i + 1, 1)
        return carry

    lax.fori_loop(0, (nblk - 2) // 2, body, 0)
    sm(1)
    pv(nblk - 2, 0)
    pv(nblk - 1, 1)


def _attention(qkv, kvc, lam_vecs, subln_g, batch, seq, ctx_len, rows=256):
    kcol = ATT_WIDTH // HEAD_W
    keys = seq + ctx_len
    assert (seq // rows) % 2 == 0 and seq // rows >= 4
    est = (2 * 2 * rows * keys * 4 + 2 * rows * keys * 2 + 6 * rows * LANES * 4
           + 2 * (4 * seq + 2 * ctx_len) * HEAD_W * 2 + 4 * rows * keys * 4)
    return pl.pallas_call(
        functools.partial(_attn_kernel, rows=rows),
        grid=(batch, ATT_HEADS),
        in_specs=[pl.BlockSpec((seq, HEAD_W), lambda b, h: (b, h)),
                  pl.BlockSpec((seq, HEAD_W), lambda b, h: (b, kcol + h)),
                  pl.BlockSpec((seq, HEAD_W), lambda b, h: (b, 2 * kcol + h)),
                  pl.BlockSpec((ctx_len, HEAD_W), lambda b, h: (b, h)),
                  pl.BlockSpec((ctx_len, HEAD_W), lambda b, h: (b, kcol + h)),
                  pl.BlockSpec((seq, HEAD_W), lambda b, h: (b, 3 * kcol + h)),
                  pl.BlockSpec((4, ATT_DIM), lambda b, h: (0, 0)),
                  pl.BlockSpec((1, HEAD_W), lambda b, h: (0, 0))],
        out_specs=pl.BlockSpec((seq, HEAD_W), lambda b, h: (b, h)),
        out_shape=jax.ShapeDtypeStruct((batch * seq, ATT_WIDTH), BF16),
        scratch_shapes=[pltpu.VMEM((2, 2, rows, keys), F32), pltpu.VMEM((2, 2, rows, LANES), F32),
                        pltpu.VMEM((2, rows, keys), BF16), pltpu.VMEM((2, rows, keys), BF16),
                        pltpu.VMEM((2, rows, LANES), F32)],
        compiler_params=pltpu.CompilerParams(dimension_semantics=("arbitrary",) * 2,
                                             vmem_limit_bytes=_vmem_limit(est)),
        name="diff_attention",
    )(qkv, qkv, qkv, kvc, kvc, qkv, lam_vecs, subln_g)


def _ssm_prep_kernel(lre_ref, lim_ref, lstep_ref, bre_ref, bim_ref, cre_ref, cim_ref,
                     a_ref, bblk_ref, cblk_ref):
    lre = lre_ref[0]
    lim = lim_ref[0]
    dt = jnp.exp(lstep_ref[0])
    mag = jnp.exp(lre * dt)
    ar = mag * jnp.cos(lim * dt)
    ai = mag * jnp.sin(lim * dt)
    den = lre * lre + lim * lim
    fr = ((ar - 1.0) * lre + ai * lim) / den
    fi = (ai * lre - (ar - 1.0) * lim) / den
    a_ref[0, 0:1, :] = ar
    a_ref[0, 1:2, :] = ai
    br = bre_ref[0]
    bi = bim_ref[0]
    bbr = fr * br - fi * bi
    bbi = fr * bi + fi * br
    row_g = lax.broadcasted_iota(jnp.int32, (LANES, BLOCK_MODES), 0) // SSM_GROUP
    col_g = lax.broadcasted_iota(jnp.int32, (LANES, BLOCK_MODES), 1) // SSM_STATE
    keep_b = row_g == col_g
    row_g2 = lax.broadcasted_iota(jnp.int32, (BLOCK_MODES, LANES), 0) // SSM_STATE
    col_g2 = lax.broadcasted_iota(jnp.int32, (BLOCK_MODES, LANES), 1) // SSM_GROUP
    keep_c = row_g2 == col_g2
    for k in range(SSM_BLOCKS):
        sl = slice(k * BLOCK_MODES, (k + 1) * BLOCK_MODES)
        bblk_ref[0, k, :, 0:BLOCK_MODES] = jnp.where(keep_b, bbr[:, sl], 0.0).astype(BF16)
        bblk_ref[0, k, :, BLOCK_MODES:BLOCK_STATE] = jnp.where(keep_b, bbi[:, sl], 0.0).astype(BF16)
        cblk_ref[0, k, 0:BLOCK_MODES, :] = jnp.where(keep_c, cre_ref[0, sl, :], 0.0).astype(BF16)
        cblk_ref[0, k, BLOCK_MODES:BLOCK_STATE, :] = jnp.where(keep_c, -cim_ref[0, sl, :], 0.0).astype(BF16)


def _ssm_prep(lre, lim, lstep, bre_t, bim_t, cre_t, cim_t):
    nd = lre.shape[0]
    vec = pl.BlockSpec((1, 1, SSM_MODES), lambda d: (d, 0, 0))
    bsp = pl.BlockSpec((1, LANES, SSM_MODES), lambda d: (d, 0, 0))
    csp = pl.BlockSpec((1, SSM_MODES, LANES), lambda d: (d, 0, 0))
    est = 2 * (4 * LANES * SSM_MODES * 4 + 2 * SSM_BLOCKS * LANES * BLOCK_STATE * 2) + 8 * LANES * SSM_MODES * 4
    return pl.pallas_call(
        _ssm_prep_kernel,
        grid=(nd,),
        in_specs=[vec, vec, vec, bsp, bsp, csp, csp],
        out_specs=[pl.BlockSpec((1, 2, SSM_MODES), lambda d: (d, 0, 0)),
                   pl.BlockSpec((1, SSM_BLOCKS, LANES, BLOCK_STATE), lambda d: (d, 0, 0, 0)),
                   pl.BlockSpec((1, SSM_BLOCKS, BLOCK_STATE, LANES), lambda d: (d, 0, 0, 0))],
        out_shape=[jax.ShapeDtypeStruct((nd, 2, SSM_MODES), F32),
                   jax.ShapeDtypeStruct((nd, SSM_BLOCKS, LANES, BLOCK_STATE), BF16),
                   jax.ShapeDtypeStruct((nd, SSM_BLOCKS, BLOCK_STATE, LANES), BF16)],
        compiler_params=pltpu.CompilerParams(dimension_semantics=("arbitrary",),
                                             vmem_limit_bytes=_vmem_limit(est)),
        name="ssm_prep",
    )(lre, lim, lstep, bre_t, bim_t, cre_t, cim_t)


def _ssm_scan_kernel(*refs, reverse, batch, ctx_chunks):
    if reverse:
        uc_ref, ul_ref, a_ref, bblk_ref, cblk_ref, yf_ref, d_ref, y_ref, s_ref, h_ref = refs
    else:
        uc_ref, ul_ref, a_ref, bblk_ref, cblk_ref, y_ref, s_ref, h_ref = refs
    steps = ul_ref.shape[0] // batch
    step = pl.program_id(0)

    @pl.when(step == 0)
    def _():
        h_ref[...] = jnp.zeros_like(h_ref)

    u = jnp.where(step < ctx_chunks, uc_ref[...], ul_ref[...])
    for k in range(SSM_BLOCKS):
        s_ref[:, k * BLOCK_STATE:(k + 1) * BLOCK_STATE] = jnp.dot(
            u[:, k * LANES:(k + 1) * LANES], bblk_ref[0, k], preferred_element_type=F32)

    for k in range(SSM_BLOCKS):
        c_re = k * BLOCK_STATE
        c_im = c_re + BLOCK_MODES
        ar = jnp.broadcast_to(a_ref[0, 0:1, k * BLOCK_MODES:(k + 1) * BLOCK_MODES], (batch, BLOCK_MODES))
        ai = jnp.broadcast_to(a_ref[0, 1:2, k * BLOCK_MODES:(k + 1) * BLOCK_MODES], (batch, BLOCK_MODES))
        hr = h_ref[:, c_re:c_re + BLOCK_MODES]
        hi = h_ref[:, c_im:c_im + BLOCK_MODES]
        for i in range(steps):
            r0 = ((steps - 1 - i) if reverse else i) * batch
            nhr = ar * hr - ai * hi + s_ref[r0:r0 + batch, c_re:c_re + BLOCK_MODES]
            nhi = ar * hi + ai * hr + s_ref[r0:r0 + batch, c_im:c_im + BLOCK_MODES]
            s_ref[r0:r0 + batch, c_re:c_re + BLOCK_MODES] = nhr
            s_ref[r0:r0 + batch, c_im:c_im + BLOCK_MODES] = nhi
            hr, hi = nhr, nhi
        h_ref[:, c_re:c_re + BLOCK_MODES] = hr
        h_ref[:, c_im:c_im + BLOCK_MODES] = hi

    for k in range(SSM_BLOCKS):
        yk = jnp.dot(s_ref[:, k * BLOCK_STATE:(k + 1) * BLOCK_STATE].astype(BF16), cblk_ref[0, k],
                     preferred_element_type=F32)
        cols = slice(k * LANES, (k + 1) * LANES)
        if reverse:
            yk = yk + yf_ref[:, cols] + d_ref[:, cols] * u[:, cols].astype(F32)
        y_ref[:, cols] = yk


def _ssm_scan(u_ctx, u_lat, a, bblk, cblk, direction, batch, yf=None, dvec=None):
    rows = SSM_CHUNK * batch
    cc = u_ctx.shape[0] // rows
    lc = u_lat.shape[0] // rows
    reverse = direction == 1
    if reverse:
        cmap = lambda s: (jnp.maximum(cc - 1 - s, 0), 0)
        lmap = lambda s: (jnp.minimum(lc - 1, lc - 1 + cc - s), 0)
    else:
        cmap = lambda s: (jnp.minimum(s, cc - 1), 0)
        lmap = lambda s: (jnp.maximum(s - cc, 0), 0)
    in_specs = [pl.BlockSpec((rows, SSM_WIDTH), cmap),
                pl.BlockSpec((rows, SSM_WIDTH), lmap),
                pl.BlockSpec((1, 2, SSM_MODES), lambda s: (direction, 0, 0)),
                pl.BlockSpec((1, SSM_BLOCKS, LANES, BLOCK_STATE), lambda s: (direction, 0, 0, 0)),
                pl.BlockSpec((1, SSM_BLOCKS, BLOCK_STATE, LANES), lambda s: (direction, 0, 0, 0))]
    args = [u_ctx, u_lat, a, bblk, cblk]
    if reverse:
        in_specs += [pl.BlockSpec((rows, SSM_WIDTH), lmap), pl.BlockSpec((1, SSM_WIDTH), lambda s: (0, 0))]
        args += [yf, dvec]
    state_w = SSM_BLOCKS * BLOCK_STATE
    est = (rows * state_w * 4 + 2 * rows * state_w // SSM_BLOCKS * 4 + 8 * rows * SSM_WIDTH * 4
           + 4 * SSM_BLOCKS * LANES * BLOCK_STATE * 2)
    return pl.pallas_call(
        functools.partial(_ssm_scan_kernel, reverse=reverse, batch=batch, ctx_chunks=cc),
        grid=(cc + lc,),
        in_specs=in_specs,
        out_specs=pl.BlockSpec((rows, SSM_WIDTH), lmap),
        out_shape=jax.ShapeDtypeStruct((lc * rows, SSM_WIDTH), F32),
        scratch_shapes=[pltpu.VMEM((rows, state_w), F32), pltpu.VMEM((batch, state_w), F32)],
        compiler_params=pltpu.CompilerParams(dimension_semantics=("arbitrary",),
                                             vmem_limit_bytes=_vmem_limit(est)),
        name="ssm_scan_bwd" if reverse else "ssm_scan_fwd",
    )(*args)


def _gelu_tanh(v):
    return 0.5 * v * (1.0 + jnp.tanh(math.sqrt(2.0 / math.pi) * (v + 0.044715 * v * v * v)))


def _merge_kernel(a_ref, y_ref, gs_ref, gma0_ref, gma1_ref, gms0_ref, gms1_ref, x_ref, gate_ref,
                  wglu_ref, bglu_ref, wpa_ref, wps_ref, wout_ref, fg_ref, o_ref):
    yg = _gelu_tanh(y_ref[...])
    z = yg * _sigmoid(jnp.dot(yg.astype(BF16), wglu_ref[...], preferred_element_type=F32) + bglu_ref[...])
    s_br = (z * _silu(gs_ref[...].astype(F32))).astype(BF16)
    ta = jnp.dot(a_ref[...], wpa_ref[...], preferred_element_type=F32)
    ts = jnp.dot(s_br, wps_ref[...], preferred_element_type=F32)
    half = D_MODEL // 2
    parts = []
    for p, (ga_r, gs_r) in enumerate(((gma0_ref, gms0_ref), (gma1_ref, gms1_ref))):
        cols = slice(p * half, (p + 1) * half)
        parts.append((_sigmoid(ga_r[...].astype(F32)) * ta[:, cols]
                      + _sigmoid(gs_r[...].astype(F32)) * ts[:, cols]).astype(BF16))
    t = jnp.concatenate(parts, axis=1)
    out = jnp.dot(t, wout_ref[...], preferred_element_type=F32)
    xn = x_ref[...] + gate_ref[0] * out
    ms = jnp.mean(xn * xn, axis=-1, keepdims=True)
    o_ref[...] = xn * lax.rsqrt(ms + NORM_EPS) * fg_ref[...]


def _merge_out(a_br, y_tm, qkv, x2d, gate, w_glu, b_glu, w_pa, w_ps, w_out, final_g, seq, tm=256):
    m, d = x2d.shape
    half = d // 2
    per_batch = seq // tm
    gs_col = (4 * ATT_WIDTH + SSM_WIDTH) // SSM_WIDTH
    gm_col = (4 * ATT_WIDTH + 2 * SSM_WIDTH) // half
    const = lambda i: (0, 0)
    est = (2 * (tm * ATT_WIDTH * 2 + tm * SSM_WIDTH * 6 + 4 * tm * half * 2 + 2 * tm * d * 4)
           + 2 * (SSM_WIDTH * SSM_WIDTH + ATT_WIDTH * d + SSM_WIDTH * d + d * d) * 2 + 6 * tm * d * 4)
    return pl.pallas_call(
        _merge_kernel,
        grid=(m // tm,),
        in_specs=[pl.BlockSpec((tm, ATT_WIDTH), lambda i: (i, 0)),
                  pl.BlockSpec((tm, SSM_WIDTH), lambda i: (i % per_batch, i // per_batch)),
                  pl.BlockSpec((tm, SSM_WIDTH), lambda i: (i, gs_col)),
                  pl.BlockSpec((tm, half), lambda i: (i, gm_col)),
                  pl.BlockSpec((tm, half), lambda i: (i, gm_col + 1)),
                  pl.BlockSpec((tm, half), lambda i: (i, gm_col + 2)),
                  pl.BlockSpec((tm, half), lambda i: (i, gm_col + 3)),
                  pl.BlockSpec((tm, d), lambda i: (i, 0)),
                  pl.BlockSpec((1, 1, d), lambda i: (i // per_batch, 0, 0)),
                  pl.BlockSpec((SSM_WIDTH, SSM_WIDTH), const),
                  pl.BlockSpec((1, SSM_WIDTH), const),
                  pl.BlockSpec((ATT_WIDTH, d), const),
                  pl.BlockSpec((SSM_WIDTH, d), const),
                  pl.BlockSpec((d, d), const),
                  pl.BlockSpec((1, d), const)],
        out_specs=pl.BlockSpec((tm, d), lambda i: (i, 0)),
        out_shape=jax.ShapeDtypeStruct((m, d), F32),
        compiler_params=pltpu.CompilerParams(dimension_semantics=("arbitrary",),
                                             vmem_limit_bytes=_vmem_limit(est)),
        name="merge_out",
    )(a_br, y_tm, qkv, qkv, qkv, qkv, qkv, x2d, gate, w_glu, b_glu, w_pa, w_ps, w_out, final_g)


def kernel(x, c, ctx, c_ctx, w_ada, b_ada, norm_g, w_in, lambda_q1, lambda_k1, lambda_q2, lambda_k2,
           subln_g, ssm_lambda_re, ssm_lambda_im, ssm_log_step, ssm_b_re, ssm_b_im, ssm_c_re, ssm_c_im,
           ssm_d, w_glu, b_glu, w_pa, w_ps, w_out, final_g):
    batch, seq, d = x.shape
    ctx_len = ctx.shape[1]
    assert w_ada.shape[0] == 1, "single-layer trunk"
    assert (d, seq % 1024, ctx_len % SSM_CHUNK, seq % SSM_CHUNK) == (D_MODEL, 0, 0, 0)

    pad = (-(batch + 1)) % SUBLANES
    cc = jnp.concatenate([c, c_ctx[None, :], jnp.zeros((pad, d), F32)], axis=0)
    mod = _adaln(cc, w_ada[0], b_ada[0][None, :])
    shift, scale, gate = (mod[:, k * d:(k + 1) * d] for k in range(3))
    as_mod = lambda v, lo, hi: v[lo:hi].reshape(hi - lo, 1, d)

    w_in_bf = w_in[0].astype(BF16)
    x2d = x.reshape(batch * seq, d)
    ctx2d = ctx.reshape(batch * ctx_len, d)
    ng = norm_g[0][None, :]

    tn = 1024
    u_col = 4 * ATT_WIDTH
    qkv, u_lat = _in_proj(x2d, as_mod(scale, 0, batch), as_mod(shift, 0, batch), ng, w_in_bf,
                          list(range(IN_WIDTH // tn)), tn, seq, u_col // tn, u_col % tn,
                          rope_tables=_rope_tables(seq))
    tnc = SSM_WIDTH
    k0 = ATT_WIDTH // tnc
    ctx_cols = list(range(k0, 3 * k0)) + [u_col // tnc]
    kvc, u_ctx = _in_proj(ctx2d, as_mod(scale, batch, batch + 1), as_mod(shift, batch, batch + 1), ng, w_in_bf,
                          ctx_cols, tnc, ctx_len, len(ctx_cols) - 1, 0)

    lam_vecs = jnp.stack([lambda_q1[0], lambda_k1[0], lambda_q2[0], lambda_k2[0]]).astype(F32)
    a_br = _attention(qkv, kvc, lam_vecs, subln_g[0][None, :].astype(F32), batch, seq, ctx_len)

    nd = ssm_lambda_re.shape[1]
    modes = lambda v: v[0].reshape(nd, 1, SSM_MODES)
    lstep = jnp.repeat(ssm_log_step[0], SSM_STATE, axis=-1).reshape(nd, 1, SSM_MODES)
    b_t = lambda v: jnp.tile(v[0].reshape(nd, SSM_MODES, SSM_GROUP).transpose(0, 2, 1), (1, GROUPS_PER_BLOCK, 1))
    c_t = lambda v: jnp.tile(v[0].transpose(0, 1, 3, 2).reshape(nd, SSM_MODES, SSM_GROUP), (1, 1, GROUPS_PER_BLOCK))
    a_disc, bblk, cblk = _ssm_prep(modes(ssm_lambda_re), modes(ssm_lambda_im), lstep,
                                   b_t(ssm_b_re), b_t(ssm_b_im), c_t(ssm_c_re), c_t(ssm_c_im))

    u_lat = u_lat.reshape(seq * batch, SSM_WIDTH)
    u_ctx = u_ctx.reshape(ctx_len * batch, SSM_WIDTH)
    yf = _ssm_scan(u_ctx, u_lat, a_disc, bblk, cblk, 0, batch)
    y_tm = _ssm_scan(u_ctx, u_lat, a_disc, bblk, cblk, 1, batch, yf=yf,
                     dvec=ssm_d[0].reshape(1, SSM_WIDTH).astype(F32))

    out = _merge_out(a_br, y_tm.reshape(seq, batch * SSM_WIDTH), qkv, x2d, as_mod(gate, 0, batch),
                     w_glu[0].astype(BF16), b_glu[0][None, :], w_pa[0].astype(BF16), w_ps[0].astype(BF16),
                     w_out[0].astype(BF16), final_g[None, :], seq)
    return out.reshape(batch, seq, d)
```

```python
import functools
import math

import jax
import jax.numpy as jnp
import numpy as np
from jax import lax
from jax.experimental import pallas as pl
from jax.experimental.pallas import tpu as pltpu

F32 = jnp.float32
BF16 = jnp.bfloat16

D_MODEL = 2048
GRID_W = 64
ATT_HEADS = 8
ATT_DIM = 64
HEAD_W = 2 * ATT_DIM
ATT_WIDTH = ATT_HEADS * HEAD_W
SSM_GROUP = 16
SSM_GROUPS = 32
SSM_WIDTH = SSM_GROUP * SSM_GROUPS
SSM_STATE = 64
SSM_MODES = SSM_GROUPS * SSM_STATE
IN_WIDTH = 4 * ATT_WIDTH + 2 * SSM_WIDTH + 2 * D_MODEL
ROPE_BASE = 10000.0
NORM_EPS = 1e-6
LAM_INIT = 0.8 - 0.6 * math.exp(0.0)

LANES = 128
SUBLANES = 8
VMEM_BYTES_V7X = 64 * 1024 * 1024
MIB = 1024 * 1024

GROUPS_PER_BLOCK = LANES // SSM_GROUP
SSM_BLOCKS = SSM_GROUPS // GROUPS_PER_BLOCK
BLOCK_MODES = GROUPS_PER_BLOCK * SSM_STATE
BLOCK_STATE = 2 * BLOCK_MODES
SSM_CHUNK = 128
KEY_CHUNK = 512


def _vmem_limit(nbytes):
    return int(min(nbytes + nbytes // 4, VMEM_BYTES_V7X - 8 * MIB))


def _sigmoid(v):
    return 1.0 / (1.0 + jnp.exp(-v))


def _silu(v):
    return v * _sigmoid(v)


def _adaln_kernel(c_ref, w_ref, b_ref, o_ref):
    cv = c_ref[...]
    o_ref[...] = jnp.dot(_silu(cv), w_ref[...], preferred_element_type=F32,
                         precision=lax.Precision.HIGHEST) + b_ref[...]


def _adaln(cc, w_ada, b_ada):
    rows, d = cc.shape
    n = w_ada.shape[1]
    tn = 768
    est = 2 * (d * tn * 4) + 2 * rows * tn * 4 + rows * d * 4 * 2
    return pl.pallas_call(
        _adaln_kernel,
        grid=(n // tn,),
        in_specs=[pl.BlockSpec((rows, d), lambda j: (0, 0)),
                  pl.BlockSpec((d, tn), lambda j: (0, j)),
                  pl.BlockSpec((1, tn), lambda j: (0, j))],
        out_specs=pl.BlockSpec((rows, tn), lambda j: (0, j)),
        out_shape=jax.ShapeDtypeStruct((rows, n), F32),
        compiler_params=pltpu.CompilerParams(dimension_semantics=("arbitrary",),
                                             vmem_limit_bytes=_vmem_limit(est)),
        name="adaln",
    )(cc, w_ada, b_ada)


def _in_proj_kernel(*refs, rope, norm_rows):
    if rope:
        x_ref, scale_ref, shift_ref, g_ref, w_ref, cos_ref, sina_ref, sinb_ref, o_ref, h_ref = refs
    else:
        x_ref, scale_ref, shift_ref, g_ref, w_ref, o_ref, h_ref = refs
    j = pl.program_id(1)
    tm = x_ref.shape[0]

    @pl.when(j == 0)
    def _():
        gmod = g_ref[...] * (1.0 + scale_ref[0])
        shift = shift_ref[0]

        def body(r, carry):
            r0 = pl.multiple_of(r * norm_rows, norm_rows)
            xf = x_ref[pl.ds(r0, norm_rows), :]
            ms = jnp.mean(xf * xf, axis=-1, keepdims=True)
            y = xf * lax.rsqrt(ms + NORM_EPS) * gmod + shift
            h_ref[pl.ds(r0, norm_rows), :] = y.astype(BF16)
            return carry

        lax.fori_loop(0, tm // norm_rows, body, 0)

    acc = jnp.dot(h_ref[...], w_ref[...], preferred_element_type=F32)

    if rope:
        is_rot = j < 2
        qscale = jnp.where(j == 0, ATT_DIM ** -0.5 * math.log2(math.e), 1.0).astype(F32)
        cos = jnp.where(is_rot, cos_ref[...] * qscale, 1.0)
        sina = jnp.where(is_rot, sina_ref[...] * qscale, 0.0)
        sinb = jnp.where(is_rot, sinb_ref[...] * qscale, 0.0)
        for cb in range(acc.shape[1] // HEAD_W):
            v = acc[:, cb * HEAD_W:(cb + 1) * HEAD_W]
            up = pltpu.roll(v, HEAD_W - ATT_DIM // 4, 1)
            dn = pltpu.roll(v, ATT_DIM // 4, 1)
            o_ref[:, cb * HEAD_W:(cb + 1) * HEAD_W] = (v * cos + up * sina + dn * sinb).astype(o_ref.dtype)
    else:
        o_ref[...] = acc.astype(o_ref.dtype)


def _in_proj(x2d, scale, shift, norm_g, w_bf, col_blocks, tn, seq_len, rope_tables=None, tm=1024):
    m, d = x2d.shape
    nj = len(col_blocks)
    rope = rope_tables is not None
    seq_blocks = max(seq_len // tm, 1)
    per_batch_mod = scale.shape[0] > 1

    def mod_idx(i, j):
        return ((i // seq_blocks) if per_batch_mod else 0, 0, 0)

    contiguous = all(cb == col_blocks[0] + k for k, cb in enumerate(col_blocks))
    if contiguous:
        w_map = lambda i, j: (0, j + col_blocks[0])
    else:
        cbs = col_blocks
        def w_map(i, j):
            idx = jnp.int32(cbs[-1])
            for k in range(nj - 2, -1, -1):
                idx = jnp.where(j == k, jnp.int32(cbs[k]), idx)
            return (0, idx)

    in_specs = [pl.BlockSpec((tm, d), lambda i, j: (i, 0)),
                pl.BlockSpec((1, 1, d), mod_idx),
                pl.BlockSpec((1, 1, d), mod_idx),
                pl.BlockSpec((1, d), lambda i, j: (0, 0)),
                pl.BlockSpec((d, tn), w_map)]
    args = [x2d, scale, shift, norm_g, w_bf]
    if rope:
        for t in rope_tables:
            in_specs.append(pl.BlockSpec((tm, HEAD_W), lambda i, j: (i % seq_blocks, 0)))
            args.append(t)
    est = (2 * tm * d * 4 + tm * d * 2 + 2 * d * tn * 2 + 2 * tm * tn * 2 + 2 * tm * tn * 4
           + (6 * tm * HEAD_W * 4 if rope else 0))
    return pl.pallas_call(
        functools.partial(_in_proj_kernel, rope=rope, norm_rows=256),
        grid=(m // tm, nj),
        in_specs=in_specs,
        out_specs=pl.BlockSpec((tm, tn), lambda i, j: (i, j)),
        out_shape=jax.ShapeDtypeStruct((m, nj * tn), BF16),
        scratch_shapes=[pltpu.VMEM((tm, d), BF16)],
        compiler_params=pltpu.CompilerParams(dimension_semantics=("arbitrary", "arbitrary"),
                                             vmem_limit_bytes=_vmem_limit(est)),
        name="in_proj_rope" if rope else "in_proj_ctx",
    )(*args)


def _rope_tables(seq):
    rows = seq // GRID_W
    row = np.repeat(np.arange(rows), GRID_W).astype(np.float64)
    col = np.tile(np.arange(GRID_W), rows).astype(np.float64)
    nf = ATT_DIM // 4
    inv = ROPE_BASE ** (-np.arange(nf, dtype=np.float64) / nf)

    def axis_tables(pos):
        ang = pos[:, None] * inv[None, :]
        c, s = np.cos(ang), np.sin(ang)
        z = np.zeros_like(s)
        return (np.concatenate([c, c], -1), np.concatenate([-s, z], -1), np.concatenate([z, s], -1))

    parts = [axis_tables(row), axis_tables(col)]
    out = []
    for k in range(3):
        comp = np.concatenate([parts[0][k], parts[1][k]], -1)
        out.append(jnp.asarray(np.concatenate([comp, comp], -1), dtype=F32))
    return out


def _attn_kernel(q_ref, k_ref, v_ref, kc_ref, vc_ref, ga_ref, lam_ref, g_ref, o_ref,
                 s_scr, m_scr, p_scr, a_scr, c_scr, *, rows):
    nt = (((1,), (1,)), ((), ()))
    seq = k_ref.shape[0]
    nblk = q_ref.shape[0] // rows
    lamv = lam_ref[...]
    lam = (jnp.exp(jnp.sum(lamv[0:1] * lamv[1:2], axis=-1, keepdims=True))
           - jnp.exp(jnp.sum(lamv[2:3] * lamv[3:4], axis=-1, keepdims=True)) + LAM_INIT)
    lane = lax.broadcasted_iota(jnp.int32, (rows, HEAD_W), 1)

    keys = seq + kc_ref.shape[0]
    key_chunks = [(k_ref, c0, c0, KEY_CHUNK) for c0 in range(0, seq, KEY_CHUNK)]
    key_chunks += [(kc_ref, c0, seq + c0, min(KEY_CHUNK, keys - seq - c0)) for c0 in range(0, keys - seq, KEY_CHUNK)]

    def qk(i, slot):
        q = q_ref[pl.ds(pl.multiple_of(i * rows, rows), rows), :]
        for comp in range(2):
            qc = jnp.where((lane >= ATT_DIM) == bool(comp), q, jnp.zeros_like(q))
            m = None
            for kref, src, dst, width in key_chunks:
                s = lax.dot_general(qc, kref[src:src + width, :], nt, preferred_element_type=F32)
                s_scr[slot, comp, :, dst:dst + width] = s
                for cb in range(width // LANES):
                    blk = s[:, cb * LANES:(cb + 1) * LANES]
                    m = blk if m is None else jnp.maximum(m, blk)
            m_scr[slot, comp] = jnp.broadcast_to(jnp.max(m, axis=-1, keepdims=True), (rows, LANES))

    def sm(slot):
        sums = []
        for comp in range(2):
            m = m_scr[slot, comp]
            acc = None
            for c0 in range(0, keys, LANES):
                p = jnp.exp2(s_scr[slot, comp, :, c0:c0 + LANES] - m)
                p_scr[comp, :, c0:c0 + LANES] = p.astype(BF16)
                acc = p if acc is None else acc + p
            sums.append(jnp.sum(acc, axis=-1, keepdims=True))
        c1 = 1.0 / sums[0]
        r = jnp.broadcast_to(lam * sums[0] / sums[1], (rows, LANES)).astype(BF16)
        for c0 in range(0, keys, LANES):
            a_scr[slot, :, c0:c0 + LANES] = p_scr[0, :, c0:c0 + LANES] - r * p_scr[1, :, c0:c0 + LANES]
        c_scr[slot] = jnp.broadcast_to(c1, (rows, LANES))

    def pv(i, slot):
        r0 = pl.multiple_of(i * rows, rows)
        o = (jnp.dot(a_scr[slot, :, 0:seq], v_ref[...], preferred_element_type=F32)
             + jnp.dot(a_scr[slot, :, seq:], vc_ref[...], preferred_element_type=F32)) * c_scr[slot]
        ms = jnp.mean(o * o, axis=-1, keepdims=True)
        on = o * lax.rsqrt(ms + NORM_EPS) * g_ref[...] * (1.0 - LAM_INIT)
        o_ref[pl.ds(r0, rows), :] = (on * _silu(ga_ref[pl.ds(r0, rows), :].astype(F32))).astype(o_ref.dtype)

    qk(0, 0)
    qk(1, 1)
    sm(0)

    def body(i, carry):
        @pl.when(i % 2 == 0)
        def _():
            qk(i + 2, 0)
            sm(1)
            pv(i, 0)

        @pl.when(i % 2 == 1)
        def _():
            qk(i + 2, 1)
            sm(0)
            pv(i, 1)

        return carry

    lax.fori_loop(0, nblk - 2, body, 0)
    sm(1)
    pv(nblk - 2, 0)
    pv(nblk - 1, 1)


def _attention(qkv, kvc, lam_vecs, subln_g, batch, seq, ctx_len, rows=256):
    kcol = ATT_WIDTH // HEAD_W
    keys = seq + ctx_len
    assert (seq // rows) % 2 == 0 and seq // rows >= 4
    est = (2 * 2 * rows * keys * 4 + 2 * rows * keys * 2 + 6 * rows * LANES * 4
           + 2 * (4 * seq + 2 * ctx_len) * HEAD_W * 2 + 4 * rows * keys * 4)
    return pl.pallas_call(
        functools.partial(_attn_kernel, rows=rows),
        grid=(batch, ATT_HEADS),
        in_specs=[pl.BlockSpec((seq, HEAD_W), lambda b, h: (b, h)),
                  pl.BlockSpec((seq, HEAD_W), lambda b, h: (b, kcol + h)),
                  pl.BlockSpec((seq, HEAD_W), lambda b, h: (b, 2 * kcol + h)),
                  pl.BlockSpec((ctx_len, HEAD_W), lambda b, h: (b, h)),
                  pl.BlockSpec((ctx_len, HEAD_W), lambda b, h: (b, kcol + h)),
                  pl.BlockSpec((seq, HEAD_W), lambda b, h: (b, 3 * kcol + h)),
                  pl.BlockSpec((4, ATT_DIM), lambda b, h: (0, 0)),
                  pl.BlockSpec((1, HEAD_W), lambda b, h: (0, 0))],
        out_specs=pl.BlockSpec((seq, HEAD_W), lambda b, h: (b, h)),
        out_shape=jax.ShapeDtypeStruct((batch * seq, ATT_WIDTH), BF16),
        scratch_shapes=[pltpu.VMEM((2, 2, rows, keys), F32), pltpu.VMEM((2, 2, rows, LANES), F32),
                        pltpu.VMEM((2, rows, keys), BF16), pltpu.VMEM((2, rows, keys), BF16),
                        pltpu.VMEM((2, rows, LANES), F32)],
        compiler_params=pltpu.CompilerParams(dimension_semantics=("arbitrary",) * 2,
                                             vmem_limit_bytes=_vmem_limit(est)),
        name="diff_attention",
    )(qkv, qkv, qkv, kvc, kvc, qkv, lam_vecs, subln_g)


def _ssm_prep_kernel(lre_ref, lim_ref, lstep_ref, bre_ref, bim_ref, cre_ref, cim_ref,
                     a_ref, bblk_ref, cblk_ref):
    lre = lre_ref[0]
    lim = lim_ref[0]
    dt = jnp.exp(lstep_ref[0])
    mag = jnp.exp(lre * dt)
    ar = mag * jnp.cos(lim * dt)
    ai = mag * jnp.sin(lim * dt)
    den = lre * lre + lim * lim
    fr = ((ar - 1.0) * lre + ai * lim) / den
    fi = (ai * lre - (ar - 1.0) * lim) / den
    a_ref[0, 0:1, :] = ar
    a_ref[0, 1:2, :] = ai
    br = bre_ref[0]
    bi = bim_ref[0]
    bbr = fr * br - fi * bi
    bbi = fr * bi + fi * br
    row_g = lax.broadcasted_iota(jnp.int32, (LANES, BLOCK_MODES), 0) // SSM_GROUP
    col_g = lax.broadcasted_iota(jnp.int32, (LANES, BLOCK_MODES), 1) // SSM_STATE
    keep_b = row_g == col_g
    row_g2 = lax.broadcasted_iota(jnp.int32, (BLOCK_MODES, LANES), 0) // SSM_STATE
    col_g2 = lax.broadcasted_iota(jnp.int32, (BLOCK_MODES, LANES), 1) // SSM_GROUP
    keep_c = row_g2 == col_g2
    for k in range(SSM_BLOCKS):
        sl = slice(k * BLOCK_MODES, (k + 1) * BLOCK_MODES)
        bblk_ref[0, k, :, 0:BLOCK_MODES] = jnp.where(keep_b, bbr[:, sl], 0.0).astype(BF16)
        bblk_ref[0, k, :, BLOCK_MODES:BLOCK_STATE] = jnp.where(keep_b, bbi[:, sl], 0.0).astype(BF16)
        cblk_ref[0, k, 0:BLOCK_MODES, :] = jnp.where(keep_c, cre_ref[0, sl, :], 0.0).astype(BF16)
        cblk_ref[0, k, BLOCK_MODES:BLOCK_STATE, :] = jnp.where(keep_c, -cim_ref[0, sl, :], 0.0).astype(BF16)


def _ssm_prep(lre, lim, lstep, bre_t, bim_t, cre_t, cim_t):
    nd = lre.shape[0]
    vec = pl.BlockSpec((1, 1, SSM_MODES), lambda d: (d, 0, 0))
    bsp = pl.BlockSpec((1, LANES, SSM_MODES), lambda d: (d, 0, 0))
    csp = pl.BlockSpec((1, SSM_MODES, LANES), lambda d: (d, 0, 0))
    est = 2 * (4 * LANES * SSM_MODES * 4 + 2 * SSM_BLOCKS * LANES * BLOCK_STATE * 2) + 8 * LANES * SSM_MODES * 4
    return pl.pallas_call(
        _ssm_prep_kernel,
        grid=(nd,),
        in_specs=[vec, vec, vec, bsp, bsp, csp, csp],
        out_specs=[pl.BlockSpec((1, 2, SSM_MODES), lambda d: (d, 0, 0)),
                   pl.BlockSpec((1, SSM_BLOCKS, LANES, BLOCK_STATE), lambda d: (d, 0, 0, 0)),
                   pl.BlockSpec((1, SSM_BLOCKS, BLOCK_STATE, LANES), lambda d: (d, 0, 0, 0))],
        out_shape=[jax.ShapeDtypeStruct((nd, 2, SSM_MODES), F32),
                   jax.ShapeDtypeStruct((nd, SSM_BLOCKS, LANES, BLOCK_STATE), BF16),
                   jax.ShapeDtypeStruct((nd, SSM_BLOCKS, BLOCK_STATE, LANES), BF16)],
        compiler_params=pltpu.CompilerParams(dimension_semantics=("arbitrary",),
                                             vmem_limit_bytes=_vmem_limit(est)),
        name="ssm_prep",
    )(lre, lim, lstep, bre_t, bim_t, cre_t, cim_t)


def _ssm_scan_kernel(*refs, reverse, batch, ctx_chunks):
    if reverse:
        uc_ref, ul_ref, a_ref, bblk_ref, cblk_ref, yf_ref, d_ref, y_ref, s_ref, h_ref = refs
    else:
        uc_ref, ul_ref, a_ref, bblk_ref, cblk_ref, y_ref, s_ref, h_ref = refs
    steps = ul_ref.shape[1]
    rows = steps * batch
    step = pl.program_id(0)

    @pl.when(step == 0)
    def _():
        h_ref[...] = jnp.zeros_like(h_ref)

    u_bt = jnp.where(step < ctx_chunks, uc_ref[...], ul_ref[...]).astype(F32)
    u32 = pltpu.einshape("bts->tbs", u_bt).reshape(rows, SSM_WIDTH)
    u = u32.astype(BF16)
    for k in range(SSM_BLOCKS):
        s_ref[:, k * BLOCK_STATE:(k + 1) * BLOCK_STATE] = jnp.dot(
            u[:, k * LANES:(k + 1) * LANES], bblk_ref[0, k], preferred_element_type=F32)

    for k in range(SSM_BLOCKS):
        c_re = k * BLOCK_STATE
        c_im = c_re + BLOCK_MODES
        ar = jnp.broadcast_to(a_ref[0, 0:1, k * BLOCK_MODES:(k + 1) * BLOCK_MODES], (batch, BLOCK_MODES))
        ai = jnp.broadcast_to(a_ref[0, 1:2, k * BLOCK_MODES:(k + 1) * BLOCK_MODES], (batch, BLOCK_MODES))
        hr = h_ref[:, c_re:c_re + BLOCK_MODES]
        hi = h_ref[:, c_im:c_im + BLOCK_MODES]
        for i in range(steps):
            r0 = ((steps - 1 - i) if reverse else i) * batch
            nhr = ar * hr - ai * hi + s_ref[r0:r0 + batch, c_re:c_re + BLOCK_MODES]
            nhi = ar * hi + ai * hr + s_ref[r0:r0 + batch, c_im:c_im + BLOCK_MODES]
            s_ref[r0:r0 + batch, c_re:c_re + BLOCK_MODES] = nhr
            s_ref[r0:r0 + batch, c_im:c_im + BLOCK_MODES] = nhi
            hr, hi = nhr, nhi
        h_ref[:, c_re:c_re + BLOCK_MODES] = hr
        h_ref[:, c_im:c_im + BLOCK_MODES] = hi

    ys = []
    for k in range(SSM_BLOCKS):
        yk = jnp.dot(s_ref[:, k * BLOCK_STATE:(k + 1) * BLOCK_STATE].astype(BF16), cblk_ref[0, k],
                     preferred_element_type=F32)
        cols = slice(k * LANES, (k + 1) * LANES)
        if reverse:
            yk = yk + yf_ref[:, cols] + d_ref[:, cols] * u32[:, cols]
        ys.append(yk)
    y = jnp.concatenate(ys, axis=1)
    if reverse:
        y_ref[...] = pltpu.einshape("tbs->bts", y.reshape(steps, batch, SSM_WIDTH))
    else:
        y_ref[...] = y


def _ssm_scan(u_ctx, u_lat, u_ctx_col, u_lat_col, a, bblk, cblk, direction, yf=None, dvec=None):
    batch = u_lat.shape[0]
    rows = SSM_CHUNK * batch
    cc = u_ctx.shape[1] // SSM_CHUNK
    lc = u_lat.shape[1] // SSM_CHUNK
    reverse = direction == 1
    if reverse:
        cidx = lambda s: jnp.maximum(cc - 1 - s, 0)
        lidx = lambda s: jnp.minimum(lc - 1, lc - 1 + cc - s)
    else:
        cidx = lambda s: jnp.minimum(s, cc - 1)
        lidx = lambda s: jnp.maximum(s - cc, 0)
    in_specs = [pl.BlockSpec((batch, SSM_CHUNK, SSM_WIDTH), lambda s: (0, cidx(s), u_ctx_col)),
                pl.BlockSpec((batch, SSM_CHUNK, SSM_WIDTH), lambda s: (0, lidx(s), u_lat_col)),
                pl.BlockSpec((1, 2, SSM_MODES), lambda s: (direction, 0, 0)),
                pl.BlockSpec((1, SSM_BLOCKS, LANES, BLOCK_STATE), lambda s: (direction, 0, 0, 0)),
                pl.BlockSpec((1, SSM_BLOCKS, BLOCK_STATE, LANES), lambda s: (direction, 0, 0, 0))]
    args = [u_ctx, u_lat, a, bblk, cblk]
    if reverse:
        in_specs += [pl.BlockSpec((rows, SSM_WIDTH), lambda s: (lidx(s), 0)),
                     pl.BlockSpec((1, SSM_WIDTH), lambda s: (0, 0))]
        args += [yf, dvec]
        out_spec = pl.BlockSpec((batch, SSM_CHUNK, SSM_WIDTH), lambda s: (0, lidx(s), 0))
        out_shape = jax.ShapeDtypeStruct((batch, lc * SSM_CHUNK, SSM_WIDTH), F32)
    else:
        out_spec = pl.BlockSpec((rows, SSM_WIDTH), lambda s: (lidx(s), 0))
        out_shape = jax.ShapeDtypeStruct((lc * rows, SSM_WIDTH), F32)
    state_w = SSM_BLOCKS * BLOCK_STATE
    est = (rows * state_w * 4 + 2 * rows * state_w // SSM_BLOCKS * 4 + 10 * rows * SSM_WIDTH * 4
           + 4 * SSM_BLOCKS * LANES * BLOCK_STATE * 2)
    return pl.pallas_call(
        functools.partial(_ssm_scan_kernel, reverse=reverse, batch=batch, ctx_chunks=cc),
        grid=(cc + lc,),
        in_specs=in_specs,
        out_specs=out_spec,
        out_shape=out_shape,
        scratch_shapes=[pltpu.VMEM((rows, state_w), F32), pltpu.VMEM((batch, state_w), F32)],
        compiler_params=pltpu.CompilerParams(dimension_semantics=("arbitrary",),
                                             vmem_limit_bytes=_vmem_limit(est)),
        name="ssm_scan_bwd" if reverse else "ssm_scan_fwd",
    )(*args)


def _gelu_tanh(v):
    return 0.5 * v * (1.0 + jnp.tanh(math.sqrt(2.0 / math.pi) * (v + 0.044715 * v * v * v)))


def _merge_kernel(a_ref, y_ref, gs_ref, gma0_ref, gma1_ref, gms0_ref, gms1_ref, x_ref, gate_ref,
                  wglu_ref, bglu_ref, wpa_ref, wps_ref, wout_ref, fg_ref, o_ref):
    yg = _gelu_tanh(y_ref[...])
    z = yg * _sigmoid(jnp.dot(yg.astype(BF16), wglu_ref[...], preferred_element_type=F32) + bglu_ref[...])
    s_br = (z * _silu(gs_ref[...].astype(F32))).astype(BF16)
    ta = jnp.dot(a_ref[...], wpa_ref[...], preferred_element_type=F32)
    ts = jnp.dot(s_br, wps_ref[...], preferred_element_type=F32)
    half = D_MODEL // 2
    parts = []
    for p, (ga_r, gs_r) in enumerate(((gma0_ref, gms0_ref), (gma1_ref, gms1_ref))):
        cols = slice(p * half, (p + 1) * half)
        parts.append((_sigmoid(ga_r[...].astype(F32)) * ta[:, cols]
                      + _sigmoid(gs_r[...].astype(F32)) * ts[:, cols]).astype(BF16))
    t = jnp.concatenate(parts, axis=1)
    out = jnp.dot(t, wout_ref[...], preferred_element_type=F32)
    xn = x_ref[...] + gate_ref[0] * out
    ms = jnp.mean(xn * xn, axis=-1, keepdims=True)
    o_ref[...] = xn * lax.rsqrt(ms + NORM_EPS) * fg_ref[...]


def _merge_out(a_br, y_bm, qkv, x2d, gate, w_glu, b_glu, w_pa, w_ps, w_out, final_g, seq, tm=256):
    m, d = x2d.shape
    half = d // 2
    per_batch = seq // tm
    gs_col = (4 * ATT_WIDTH + SSM_WIDTH) // SSM_WIDTH
    gm_col = (4 * ATT_WIDTH + 2 * SSM_WIDTH) // half
    const = lambda i: (0, 0)
    est = (2 * (tm * ATT_WIDTH * 2 + tm * SSM_WIDTH * 6 + 4 * tm * half * 2 + 2 * tm * d * 4)
           + 2 * (SSM_WIDTH * SSM_WIDTH + ATT_WIDTH * d + SSM_WIDTH * d + d * d) * 2 + 6 * tm * d * 4)
    return pl.pallas_call(
        _merge_kernel,
        grid=(m // tm,),
        in_specs=[pl.BlockSpec((tm, ATT_WIDTH), lambda i: (i, 0)),
                  pl.BlockSpec((tm, SSM_WIDTH), lambda i: (i, 0)),
                  pl.BlockSpec((tm, SSM_WIDTH), lambda i: (i, gs_col)),
                  pl.BlockSpec((tm, half), lambda i: (i, gm_col)),
                  pl.BlockSpec((tm, half), lambda i: (i, gm_col + 1)),
                  pl.BlockSpec((tm, half), lambda i: (i, gm_col + 2)),
                  pl.BlockSpec((tm, half), lambda i: (i, gm_col + 3)),
                  pl.BlockSpec((tm, d), lambda i: (i, 0)),
                  pl.BlockSpec((1, 1, d), lambda i: (i // per_batch, 0, 0)),
                  pl.BlockSpec((SSM_WIDTH, SSM_WIDTH), const),
                  pl.BlockSpec((1, SSM_WIDTH), const),
                  pl.BlockSpec((ATT_WIDTH, d), const),
                  pl.BlockSpec((SSM_WIDTH, d), const),
                  pl.BlockSpec((d, d), const),
                  pl.BlockSpec((1, d), const)],
        out_specs=pl.BlockSpec((tm, d), lambda i: (i, 0)),
        out_shape=jax.ShapeDtypeStruct((m, d), F32),
        compiler_params=pltpu.CompilerParams(dimension_semantics=("arbitrary",),
                                             vmem_limit_bytes=_vmem_limit(est)),
        name="merge_out",
    )(a_br, y_bm, qkv, qkv, qkv, qkv, qkv, x2d, gate, w_glu, b_glu, w_pa, w_ps, w_out, final_g)


def kernel(x, c, ctx, c_ctx, w_ada, b_ada, norm_g, w_in, lambda_q1, lambda_k1, lambda_q2, lambda_k2,
           subln_g, ssm_lambda_re, ssm_lambda_im, ssm_log_step, ssm_b_re, ssm_b_im, ssm_c_re, ssm_c_im,
           ssm_d, w_glu, b_glu, w_pa, w_ps, w_out, final_g):
    batch, seq, d = x.shape
    ctx_len = ctx.shape[1]
    assert w_ada.shape[0] == 1, "single-layer trunk"
    assert (d, seq % 1024, ctx_len % SSM_CHUNK, seq % SSM_CHUNK) == (D_MODEL, 0, 0, 0)

    pad = (-(batch + 1)) % SUBLANES
    cc = jnp.concatenate([c, c_ctx[None, :], jnp.zeros((pad, d), F32)], axis=0)
    mod = _adaln(cc, w_ada[0], b_ada[0][None, :])
    shift, scale, gate = (mod[:, k * d:(k + 1) * d] for k in range(3))
    as_mod = lambda v, lo, hi: v[lo:hi].reshape(hi - lo, 1, d)

    w_in_bf = w_in[0].astype(BF16)
    x2d = x.reshape(batch * seq, d)
    ctx2d = ctx.reshape(batch * ctx_len, d)
    ng = norm_g[0][None, :]

    tn = 1024
    u_col = 4 * ATT_WIDTH
    qkv = _in_proj(x2d, as_mod(scale, 0, batch), as_mod(shift, 0, batch), ng, w_in_bf,
                   list(range(IN_WIDTH // tn)), tn, seq, rope_tables=_rope_tables(seq))
    tnc = SSM_WIDTH
    k0 = ATT_WIDTH // tnc
    ctx_cols = list(range(k0, 3 * k0)) + [u_col // tnc]
    kvc = _in_proj(ctx2d, as_mod(scale, batch, batch + 1), as_mod(shift, batch, batch + 1), ng, w_in_bf,
                   ctx_cols, tnc, ctx_len)

    lam_vecs = jnp.stack([lambda_q1[0], lambda_k1[0], lambda_q2[0], lambda_k2[0]]).astype(F32)
    a_br = _attention(qkv, kvc, lam_vecs, subln_g[0][None, :].astype(F32), batch, seq, ctx_len)

    nd = ssm_lambda_re.shape[1]
    modes = lambda v: v[0].reshape(nd, 1, SSM_MODES)
    lstep = jnp.repeat(ssm_log_step[0], SSM_STATE, axis=-1).reshape(nd, 1, SSM_MODES)
    b_t = lambda v: jnp.tile(v[0].reshape(nd, SSM_MODES, SSM_GROUP).transpose(0, 2, 1), (1, GROUPS_PER_BLOCK, 1))
    c_t = lambda v: jnp.tile(v[0].transpose(0, 1, 3, 2).reshape(nd, SSM_MODES, SSM_GROUP), (1, 1, GROUPS_PER_BLOCK))
    a_disc, bblk, cblk = _ssm_prep(modes(ssm_lambda_re), modes(ssm_lambda_im), lstep,
                                   b_t(ssm_b_re), b_t(ssm_b_im), c_t(ssm_c_re), c_t(ssm_c_im))

    qkv3 = qkv.reshape(batch, seq, IN_WIDTH)
    kvc3 = kvc.reshape(batch, ctx_len, len(ctx_cols) * tnc)
    scan = functools.partial(_ssm_scan, kvc3, qkv3, len(ctx_cols) - 1, u_col // SSM_WIDTH, a_disc, bblk, cblk)
    yf = scan(0)
    y_bm = scan(1, yf=yf, dvec=ssm_d[0].reshape(1, SSM_WIDTH).astype(F32))

    out = _merge_out(a_br, y_bm.reshape(batch * seq, SSM_WIDTH), qkv, x2d, as_mod(gate, 0, batch),
                     w_glu[0].astype(BF16), b_glu[0][None, :], w_pa[0].astype(BF16), w_ps[0].astype(BF16),
                     w_out[0].astype(BF16), final_g[None, :], seq)
    return out.reshape(batch, seq, d)
```

```python
import functools
import math

import jax
import jax.numpy as jnp
import numpy as np
from jax import lax
from jax.experimental import pallas as pl
from jax.experimental.pallas import tpu as pltpu

F32 = jnp.float32
BF16 = jnp.bfloat16

D_MODEL = 2048
GRID_W = 64
ATT_HEADS = 8
ATT_DIM = 64
HEAD_W = 2 * ATT_DIM
ATT_WIDTH = ATT_HEADS * HEAD_W
SSM_GROUP = 16
SSM_GROUPS = 32
SSM_WIDTH = SSM_GROUP * SSM_GROUPS
SSM_STATE = 64
SSM_MODES = SSM_GROUPS * SSM_STATE
IN_WIDTH = 4 * ATT_WIDTH + 2 * SSM_WIDTH + 2 * D_MODEL
ROPE_BASE = 10000.0
NORM_EPS = 1e-6
LAM_INIT = 0.8 - 0.6 * math.exp(0.0)

LANES = 128
SUBLANES = 8
VMEM_BYTES_V7X = 64 * 1024 * 1024
MIB = 1024 * 1024

GROUPS_PER_BLOCK = LANES // SSM_GROUP
SSM_BLOCKS = SSM_GROUPS // GROUPS_PER_BLOCK
BLOCK_MODES = GROUPS_PER_BLOCK * SSM_STATE
BLOCK_STATE = 2 * BLOCK_MODES
SSM_CHUNK = 128
KEY_CHUNK = 512


def _vmem_limit(nbytes):
    return int(min(nbytes + nbytes // 4, VMEM_BYTES_V7X - 8 * MIB))


def _sigmoid(v):
    return 1.0 / (1.0 + jnp.exp(-v))


def _silu(v):
    return v * _sigmoid(v)


def _adaln_kernel(c_ref, w_ref, b_ref, o_ref):
    cv = c_ref[...]
    o_ref[...] = jnp.dot(_silu(cv), w_ref[...], preferred_element_type=F32,
                         precision=lax.Precision.HIGHEST) + b_ref[...]


def _adaln(cc, w_ada, b_ada):
    rows, d = cc.shape
    n = w_ada.shape[1]
    tn = 768
    est = 2 * (d * tn * 4) + 2 * rows * tn * 4 + rows * d * 4 * 2
    return pl.pallas_call(
        _adaln_kernel,
        grid=(n // tn,),
        in_specs=[pl.BlockSpec((rows, d), lambda j: (0, 0)),
                  pl.BlockSpec((d, tn), lambda j: (0, j)),
                  pl.BlockSpec((1, tn), lambda j: (0, j))],
        out_specs=pl.BlockSpec((rows, tn), lambda j: (0, j)),
        out_shape=jax.ShapeDtypeStruct((rows, n), F32),
        compiler_params=pltpu.CompilerParams(dimension_semantics=("arbitrary",),
                                             vmem_limit_bytes=_vmem_limit(est)),
        name="adaln",
    )(cc, w_ada, b_ada)


def _in_proj_kernel(*refs, rope, norm_rows):
    if rope:
        x_ref, scale_ref, shift_ref, g_ref, w_ref, cos_ref, sina_ref, sinb_ref, o_ref, h_ref = refs
    else:
        x_ref, scale_ref, shift_ref, g_ref, w_ref, o_ref, h_ref = refs
    j = pl.program_id(1)
    tm = x_ref.shape[0]

    @pl.when(j == 0)
    def _():
        gmod = g_ref[...] * (1.0 + scale_ref[0])
        shift = shift_ref[0]

        def body(r, carry):
            r0 = pl.multiple_of(r * norm_rows, norm_rows)
            xf = x_ref[pl.ds(r0, norm_rows), :]
            ms = jnp.mean(xf * xf, axis=-1, keepdims=True)
            y = xf * lax.rsqrt(ms + NORM_EPS) * gmod + shift
            h_ref[pl.ds(r0, norm_rows), :] = y.astype(BF16)
            return carry

        lax.fori_loop(0, tm // norm_rows, body, 0)

    acc = jnp.dot(h_ref[...], w_ref[...], preferred_element_type=F32)

    if rope:
        @pl.when(j < 2)
        def _():
            qscale = jnp.where(j == 0, ATT_DIM ** -0.5 * math.log2(math.e), 1.0).astype(F32)
            cos = cos_ref[...] * qscale
            sina = sina_ref[...] * qscale
            sinb = sinb_ref[...] * qscale
            for cb in range(acc.shape[1] // HEAD_W):
                v = acc[:, cb * HEAD_W:(cb + 1) * HEAD_W]
                up = pltpu.roll(v, HEAD_W - ATT_DIM // 4, 1)
                dn = pltpu.roll(v, ATT_DIM // 4, 1)
                o_ref[:, cb * HEAD_W:(cb + 1) * HEAD_W] = (v * cos + up * sina + dn * sinb).astype(o_ref.dtype)

        @pl.when(j >= 2)
        def _():
            o_ref[...] = acc.astype(o_ref.dtype)
    else:
        o_ref[...] = acc.astype(o_ref.dtype)


def _in_proj(x2d, scale, shift, norm_g, w_bf, col_blocks, tn, seq_len, rope_tables=None, tm=1024):
    m, d = x2d.shape
    nj = len(col_blocks)
    rope = rope_tables is not None
    seq_blocks = max(seq_len // tm, 1)
    per_batch_mod = scale.shape[0] > 1

    def mod_idx(i, j):
        return ((i // seq_blocks) if per_batch_mod else 0, 0, 0)

    contiguous = all(cb == col_blocks[0] + k for k, cb in enumerate(col_blocks))
    if contiguous:
        w_map = lambda i, j: (0, j + col_blocks[0])
    else:
        cbs = col_blocks
        def w_map(i, j):
            idx = jnp.int32(cbs[-1])
            for k in range(nj - 2, -1, -1):
                idx = jnp.where(j == k, jnp.int32(cbs[k]), idx)
            return (0, idx)

    in_specs = [pl.BlockSpec((tm, d), lambda i, j: (i, 0)),
                pl.BlockSpec((1, 1, d), mod_idx),
                pl.BlockSpec((1, 1, d), mod_idx),
                pl.BlockSpec((1, d), lambda i, j: (0, 0)),
                pl.BlockSpec((d, tn), w_map)]
    args = [x2d, scale, shift, norm_g, w_bf]
    if rope:
        for t in rope_tables:
            in_specs.append(pl.BlockSpec((tm, HEAD_W), lambda i, j: (i % seq_blocks, 0)))
            args.append(t)
    est = (2 * tm * d * 4 + tm * d * 2 + 2 * d * tn * 2 + 2 * tm * tn * 2 + 2 * tm * tn * 4
           + (6 * tm * HEAD_W * 4 if rope else 0))
    return pl.pallas_call(
        functools.partial(_in_proj_kernel, rope=rope, norm_rows=256),
        grid=(m // tm, nj),
        in_specs=in_specs,
        out_specs=pl.BlockSpec((tm, tn), lambda i, j: (i, j)),
        out_shape=jax.ShapeDtypeStruct((m, nj * tn), BF16),
        scratch_shapes=[pltpu.VMEM((tm, d), BF16)],
        compiler_params=pltpu.CompilerParams(dimension_semantics=("arbitrary", "arbitrary"),
                                             vmem_limit_bytes=_vmem_limit(est)),
        name="in_proj_rope" if rope else "in_proj_ctx",
    )(*args)


def _rope_tables(seq):
    rows = seq // GRID_W
    row = np.repeat(np.arange(rows), GRID_W).astype(np.float64)
    col = np.tile(np.arange(GRID_W), rows).astype(np.float64)
    nf = ATT_DIM // 4
    inv = ROPE_BASE ** (-np.arange(nf, dtype=np.float64) / nf)

    def axis_tables(pos):
        ang = pos[:, None] * inv[None, :]
        c, s = np.cos(ang), np.sin(ang)
        z = np.zeros_like(s)
        return (np.concatenate([c, c], -1), np.concatenate([-s, z], -1), np.concatenate([z, s], -1))

    parts = [axis_tables(row), axis_tables(col)]
    out = []
    for k in range(3):
        comp = np.concatenate([parts[0][k], parts[1][k]], -1)
        out.append(jnp.asarray(np.concatenate([comp, comp], -1), dtype=F32))
    return out


def _attn_kernel(q_ref, k_ref, v_ref, kc_ref, vc_ref, ga_ref, lam_ref, g_ref, o_ref,
                 s_scr, m_scr, p_scr, a_scr, c_scr, *, rows):
    nt = (((1,), (1,)), ((), ()))
    seq = k_ref.shape[0]
    nblk = q_ref.shape[0] // rows
    lamv = lam_ref[...]
    lam = (jnp.exp(jnp.sum(lamv[0:1] * lamv[1:2], axis=-1, keepdims=True))
           - jnp.exp(jnp.sum(lamv[2:3] * lamv[3:4], axis=-1, keepdims=True)) + LAM_INIT)
    lane = lax.broadcasted_iota(jnp.int32, (rows, HEAD_W), 1)

    keys = seq + kc_ref.shape[0]
    key_chunks = [(k_ref, c0, c0, KEY_CHUNK) for c0 in range(0, seq, KEY_CHUNK)]
    key_chunks += [(kc_ref, c0, seq + c0, min(KEY_CHUNK, keys - seq - c0)) for c0 in range(0, keys - seq, KEY_CHUNK)]

    def qk(i, slot):
        q = q_ref[pl.ds(pl.multiple_of(i * rows, rows), rows), :]
        for comp in range(2):
            qc = jnp.where((lane >= ATT_DIM) == bool(comp), q, jnp.zeros_like(q))
            m = None
            for kref, src, dst, width in key_chunks:
                s = lax.dot_general(qc, kref[src:src + width, :], nt, preferred_element_type=F32)
                s_scr[slot, comp, :, dst:dst + width] = s
                for cb in range(width // LANES):
                    blk = s[:, cb * LANES:(cb + 1) * LANES]
                    m = blk if m is None else jnp.maximum(m, blk)
            m_scr[slot, comp] = jnp.broadcast_to(jnp.max(m, axis=-1, keepdims=True), (rows, LANES))

    def sm(slot):
        sums = []
        for comp in range(2):
            m = m_scr[slot, comp]
            acc = None
            for c0 in range(0, keys, LANES):
                p = jnp.exp2(s_scr[slot, comp, :, c0:c0 + LANES] - m)
                p_scr[comp, :, c0:c0 + LANES] = p.astype(BF16)
                acc = p if acc is None else acc + p
            sums.append(jnp.sum(acc, axis=-1, keepdims=True))
        c1 = 1.0 / sums[0]
        r = jnp.broadcast_to(lam * sums[0] / sums[1], (rows, LANES)).astype(BF16)
        for c0 in range(0, keys, LANES):
            a_scr[slot, :, c0:c0 + LANES] = p_scr[0, :, c0:c0 + LANES] - r * p_scr[1, :, c0:c0 + LANES]
        c_scr[slot] = jnp.broadcast_to(c1, (rows, LANES))

    def pv(i, slot):
        r0 = pl.multiple_of(i * rows, rows)
        o = (jnp.dot(a_scr[slot, :, 0:seq], v_ref[...], preferred_element_type=F32)
             + jnp.dot(a_scr[slot, :, seq:], vc_ref[...], preferred_element_type=F32)) * c_scr[slot]
        ms = jnp.mean(o * o, axis=-1, keepdims=True)
        on = o * lax.rsqrt(ms + NORM_EPS) * g_ref[...] * (1.0 - LAM_INIT)
        o_ref[pl.ds(r0, rows), :] = (on * _silu(ga_ref[pl.ds(r0, rows), :].astype(F32))).astype(o_ref.dtype)

    qk(0, 0)
    qk(1, 1)
    sm(0)

    def body(t, carry):
        i = 2 * t
        qk(i + 2, 0)
        sm(1)
        pv(i, 0)
        qk(i + 3, 1)
        sm(0)
        pv(i + 1, 1)
        return carry

    lax.fori_loop(0, (nblk - 2) // 2, body, 0)
    sm(1)
    pv(nblk - 2, 0)
    pv(nblk - 1, 1)


def _attention(qkv, kvc, lam_vecs, subln_g, batch, seq, ctx_len, rows=256):
    kcol = ATT_WIDTH // HEAD_W
    keys = seq + ctx_len
    assert (seq // rows) % 2 == 0 and seq // rows >= 4
    est = (2 * 2 * rows * keys * 4 + 2 * rows * keys * 2 + 6 * rows * LANES * 4
           + 2 * (4 * seq + 2 * ctx_len) * HEAD_W * 2 + 4 * rows * keys * 4)
    return pl.pallas_call(
        functools.partial(_attn_kernel, rows=rows),
        grid=(batch, ATT_HEADS),
        in_specs=[pl.BlockSpec((seq, HEAD_W), lambda b, h: (b, h)),
                  pl.BlockSpec((seq, HEAD_W), lambda b, h: (b, kcol + h)),
                  pl.BlockSpec((seq, HEAD_W), lambda b, h: (b, 2 * kcol + h)),
                  pl.BlockSpec((ctx_len, HEAD_W), lambda b, h: (b, h)),
                  pl.BlockSpec((ctx_len, HEAD_W), lambda b, h: (b, kcol + h)),
                  pl.BlockSpec((seq, HEAD_W), lambda b, h: (b, 3 * kcol + h)),
                  pl.BlockSpec((4, ATT_DIM), lambda b, h: (0, 0)),
                  pl.BlockSpec((1, HEAD_W), lambda b, h: (0, 0))],
        out_specs=pl.BlockSpec((seq, HEAD_W), lambda b, h: (b, h)),
        out_shape=jax.ShapeDtypeStruct((batch * seq, ATT_WIDTH), BF16),
        scratch_shapes=[pltpu.VMEM((2, 2, rows, keys), F32), pltpu.VMEM((2, 2, rows, LANES), F32),
                        pltpu.VMEM((2, rows, keys), BF16), pltpu.VMEM((2, rows, keys), BF16),
                        pltpu.VMEM((2, rows, LANES), F32)],
        compiler_params=pltpu.CompilerParams(dimension_semantics=("arbitrary",) * 2,
                                             vmem_limit_bytes=_vmem_limit(est)),
        name="diff_attention",
    )(qkv, qkv, qkv, kvc, kvc, qkv, lam_vecs, subln_g)


def _ssm_prep_kernel(lre_ref, lim_ref, lstep_ref, bre_ref, bim_ref, cre_ref, cim_ref,
                     a_ref, bblk_ref, cblk_ref):
    lre = lre_ref[0]
    lim = lim_ref[0]
    dt = jnp.exp(lstep_ref[0])
    mag = jnp.exp(lre * dt)
    ar = mag * jnp.cos(lim * dt)
    ai = mag * jnp.sin(lim * dt)
    den = lre * lre + lim * lim
    fr = ((ar - 1.0) * lre + ai * lim) / den
    fi = (ai * lre - (ar - 1.0) * lim) / den
    a_ref[0, 0:1, :] = ar
    a_ref[0, 1:2, :] = ai
    br = bre_ref[0]
    bi = bim_ref[0]
    bbr = fr * br - fi * bi
    bbi = fr * bi + fi * br
    row_g = lax.broadcasted_iota(jnp.int32, (LANES, BLOCK_MODES), 0) // SSM_GROUP
    col_g = lax.broadcasted_iota(jnp.int32, (LANES, BLOCK_MODES), 1) // SSM_STATE
    keep_b = row_g == col_g
    row_g2 = lax.broadcasted_iota(jnp.int32, (BLOCK_MODES, LANES), 0) // SSM_STATE
    col_g2 = lax.broadcasted_iota(jnp.int32, (BLOCK_MODES, LANES), 1) // SSM_GROUP
    keep_c = row_g2 == col_g2
    for k in range(SSM_BLOCKS):
        sl = slice(k * BLOCK_MODES, (k + 1) * BLOCK_MODES)
        bblk_ref[0, k, :, 0:BLOCK_MODES] = jnp.where(keep_b, bbr[:, sl], 0.0).astype(BF16)
        bblk_ref[0, k, :, BLOCK_MODES:BLOCK_STATE] = jnp.where(keep_b, bbi[:, sl], 0.0).astype(BF16)
        cblk_ref[0, k, 0:BLOCK_MODES, :] = jnp.where(keep_c, cre_ref[0, sl, :], 0.0).astype(BF16)
        cblk_ref[0, k, BLOCK_MODES:BLOCK_STATE, :] = jnp.where(keep_c, -cim_ref[0, sl, :], 0.0).astype(BF16)


def _ssm_prep(lre, lim, lstep, bre_t, bim_t, cre_t, cim_t):
    nd = lre.shape[0]
    vec = pl.BlockSpec((1, 1, SSM_MODES), lambda d: (d, 0, 0))
    bsp = pl.BlockSpec((1, LANES, SSM_MODES), lambda d: (d, 0, 0))
    csp = pl.BlockSpec((1, SSM_MODES, LANES), lambda d: (d, 0, 0))
    est = 2 * (4 * LANES * SSM_MODES * 4 + 2 * SSM_BLOCKS * LANES * BLOCK_STATE * 2) + 8 * LANES * SSM_MODES * 4
    return pl.pallas_call(
        _ssm_prep_kernel,
        grid=(nd,),
        in_specs=[vec, vec, vec, bsp, bsp, csp, csp],
        out_specs=[pl.BlockSpec((1, 2, SSM_MODES), lambda d: (d, 0, 0)),
                   pl.BlockSpec((1, SSM_BLOCKS, LANES, BLOCK_STATE), lambda d: (d, 0, 0, 0)),
                   pl.BlockSpec((1, SSM_BLOCKS, BLOCK_STATE, LANES), lambda d: (d, 0, 0, 0))],
        out_shape=[jax.ShapeDtypeStruct((nd, 2, SSM_MODES), F32),
                   jax.ShapeDtypeStruct((nd, SSM_BLOCKS, LANES, BLOCK_STATE), BF16),
                   jax.ShapeDtypeStruct((nd, SSM_BLOCKS, BLOCK_STATE, LANES), BF16)],
        compiler_params=pltpu.CompilerParams(dimension_semantics=("arbitrary",),
                                             vmem_limit_bytes=_vmem_limit(est)),
        name="ssm_prep",
    )(lre, lim, lstep, bre_t, bim_t, cre_t, cim_t)


def _ssm_scan_kernel(*refs, reverse, batch, ctx_chunks):
    if reverse:
        uc_ref, ul_ref, a_ref, bblk_ref, cblk_ref, yf_ref, d_ref, y_ref, s_ref, h_ref = refs
    else:
        uc_ref, ul_ref, a_ref, bblk_ref, cblk_ref, y_ref, s_ref, h_ref = refs
    steps = ul_ref.shape[1]
    rows = steps * batch
    step = pl.program_id(0)

    @pl.when(step == 0)
    def _():
        h_ref[...] = jnp.zeros_like(h_ref)

    u_bt = jnp.where(step < ctx_chunks, uc_ref[...], ul_ref[...]).astype(F32)
    u32 = jnp.swapaxes(u_bt, 0, 1).reshape(rows, SSM_WIDTH)
    u = u32.astype(BF16)
    for k in range(SSM_BLOCKS):
        s_ref[:, k * BLOCK_STATE:(k + 1) * BLOCK_STATE] = jnp.dot(
            u[:, k * LANES:(k + 1) * LANES], bblk_ref[0, k], preferred_element_type=F32)

    for k in range(SSM_BLOCKS):
        c_re = k * BLOCK_STATE
        c_im = c_re + BLOCK_MODES
        ar = jnp.broadcast_to(a_ref[0, 0:1, k * BLOCK_MODES:(k + 1) * BLOCK_MODES], (batch, BLOCK_MODES))
        ai = jnp.broadcast_to(a_ref[0, 1:2, k * BLOCK_MODES:(k + 1) * BLOCK_MODES], (batch, BLOCK_MODES))
        hr = h_ref[:, c_re:c_re + BLOCK_MODES]
        hi = h_ref[:, c_im:c_im + BLOCK_MODES]
        for i in range(steps):
            r0 = ((steps - 1 - i) if reverse else i) * batch
            nhr = ar * hr - ai * hi + s_ref[r0:r0 + batch, c_re:c_re + BLOCK_MODES]
            nhi = ar * hi + ai * hr + s_ref[r0:r0 + batch, c_im:c_im + BLOCK_MODES]
            s_ref[r0:r0 + batch, c_re:c_re + BLOCK_MODES] = nhr
            s_ref[r0:r0 + batch, c_im:c_im + BLOCK_MODES] = nhi
            hr, hi = nhr, nhi
        h_ref[:, c_re:c_re + BLOCK_MODES] = hr
        h_ref[:, c_im:c_im + BLOCK_MODES] = hi

    ys = []
    for k in range(SSM_BLOCKS):
        yk = jnp.dot(s_ref[:, k * BLOCK_STATE:(k + 1) * BLOCK_STATE].astype(BF16), cblk_ref[0, k],
                     preferred_element_type=F32)
        cols = slice(k * LANES, (k + 1) * LANES)
        if reverse:
            yk = yk + yf_ref[:, cols] + d_ref[:, cols] * u32[:, cols]
        ys.append(yk)
    y = jnp.concatenate(ys, axis=1)
    if reverse:
        y_ref[...] = jnp.swapaxes(y.reshape(steps, batch, SSM_WIDTH), 0, 1)
    else:
        y_ref[...] = y


def _ssm_scan(u_ctx, u_lat, u_ctx_col, u_lat_col, a, bblk, cblk, direction, yf=None, dvec=None):
    batch = u_lat.shape[0]
    rows = SSM_CHUNK * batch
    cc = u_ctx.shape[1] // SSM_CHUNK
    lc = u_lat.shape[1] // SSM_CHUNK
    reverse = direction == 1
    if reverse:
        cidx = lambda s: jnp.maximum(cc - 1 - s, 0)
        lidx = lambda s: jnp.minimum(lc - 1, lc - 1 + cc - s)
    else:
        cidx = lambda s: jnp.minimum(s, cc - 1)
        lidx = lambda s: jnp.maximum(s - cc, 0)
    in_specs = [pl.BlockSpec((batch, SSM_CHUNK, SSM_WIDTH), lambda s: (0, cidx(s), u_ctx_col)),
                pl.BlockSpec((batch, SSM_CHUNK, SSM_WIDTH), lambda s: (0, lidx(s), u_lat_col)),
                pl.BlockSpec((1, 2, SSM_MODES), lambda s: (direction, 0, 0)),
                pl.BlockSpec((1, SSM_BLOCKS, LANES, BLOCK_STATE), lambda s: (direction, 0, 0, 0)),
                pl.BlockSpec((1, SSM_BLOCKS, BLOCK_STATE, LANES), lambda s: (direction, 0, 0, 0))]
    args = [u_ctx, u_lat, a, bblk, cblk]
    if reverse:
        in_specs += [pl.BlockSpec((rows, SSM_WIDTH), lambda s: (lidx(s), 0)),
                     pl.BlockSpec((1, SSM_WIDTH), lambda s: (0, 0))]
        args += [yf, dvec]
        out_spec = pl.BlockSpec((batch, SSM_CHUNK, SSM_WIDTH), lambda s: (0, lidx(s), 0))
        out_shape = jax.ShapeDtypeStruct((batch, lc * SSM_CHUNK, SSM_WIDTH), F32)
    else:
        out_spec = pl.BlockSpec((rows, SSM_WIDTH), lambda s: (lidx(s), 0))
        out_shape = jax.ShapeDtypeStruct((lc * rows, SSM_WIDTH), F32)
    state_w = SSM_BLOCKS * BLOCK_STATE
    est = (rows * state_w * 4 + 2 * rows * state_w // SSM_BLOCKS * 4 + 10 * rows * SSM_WIDTH * 4
           + 4 * SSM_BLOCKS * LANES * BLOCK_STATE * 2)
    return pl.pallas_call(
        functools.partial(_ssm_scan_kernel, reverse=reverse, batch=batch, ctx_chunks=cc),
        grid=(cc + lc,),
        in_specs=in_specs,
        out_specs=out_spec,
        out_shape=out_shape,
        scratch_shapes=[pltpu.VMEM((rows, state_w), F32), pltpu.VMEM((batch, state_w), F32)],
        compiler_params=pltpu.CompilerParams(dimension_semantics=("arbitrary",),
                                             vmem_limit_bytes=_vmem_limit(est)),
        name="ssm_scan_bwd" if reverse else "ssm_scan_fwd",
    )(*args)


def _gelu_tanh(v):
    return 0.5 * v * (1.0 + jnp.tanh(math.sqrt(2.0 / math.pi) * (v + 0.044715 * v * v * v)))


def _merge_kernel(a_ref, y_ref, gs_ref, gma0_ref, gma1_ref, gms0_ref, gms1_ref, x_ref, gate_ref,
                  wglu_ref, bglu_ref, wpa_ref, wps_ref, wout_ref, fg_ref, o_ref):
    yg = _gelu_tanh(y_ref[...])
    z = yg * _sigmoid(jnp.dot(yg.astype(BF16), wglu_ref[...], preferred_element_type=F32) + bglu_ref[...])
    s_br = (z * _silu(gs_ref[...].astype(F32))).astype(BF16)
    ta = jnp.dot(a_ref[...], wpa_ref[...], preferred_element_type=F32)
    ts = jnp.dot(s_br, wps_ref[...], preferred_element_type=F32)
    half = D_MODEL // 2
    parts = []
    for p, (ga_r, gs_r) in enumerate(((gma0_ref, gms0_ref), (gma1_ref, gms1_ref))):
        cols = slice(p * half, (p + 1) * half)
        parts.append((_sigmoid(ga_r[...].astype(F32)) * ta[:, cols]
                      + _sigmoid(gs_r[...].astype(F32)) * ts[:, cols]).astype(BF16))
    t = jnp.concatenate(parts, axis=1)
    out = jnp.dot(t, wout_ref[...], preferred_element_type=F32)
    xn = x_ref[...] + gate_ref[0] * out
    ms = jnp.mean(xn * xn, axis=-1, keepdims=True)
    o_ref[...] = xn * lax.rsqrt(ms + NORM_EPS) * fg_ref[...]


def _merge_out(a_br, y_bm, qkv, x2d, gate, w_glu, b_glu, w_pa, w_ps, w_out, final_g, seq, tm=256):
    m, d = x2d.shape
    half = d // 2
    per_batch = seq // tm
    gs_col = (4 * ATT_WIDTH + SSM_WIDTH) // SSM_WIDTH
    gm_col = (4 * ATT_WIDTH + 2 * SSM_WIDTH) // half
    const = lambda i: (0, 0)
    est = (2 * (tm * ATT_WIDTH * 2 + tm * SSM_WIDTH * 6 + 4 * tm * half * 2 + 2 * tm * d * 4)
           + 2 * (SSM_WIDTH * SSM_WIDTH + ATT_WIDTH * d + SSM_WIDTH * d + d * d) * 2 + 6 * tm * d * 4)
    return pl.pallas_call(
        _merge_kernel,
        grid=(m // tm,),
        in_specs=[pl.BlockSpec((tm, ATT_WIDTH), lambda i: (i, 0)),
                  pl.BlockSpec((tm, SSM_WIDTH), lambda i: (i, 0)),
                  pl.BlockSpec((tm, SSM_WIDTH), lambda i: (i, gs_col)),
                  pl.BlockSpec((tm, half), lambda i: (i, gm_col)),
                  pl.BlockSpec((tm, half), lambda i: (i, gm_col + 1)),
                  pl.BlockSpec((tm, half), lambda i: (i, gm_col + 2)),
                  pl.BlockSpec((tm, half), lambda i: (i, gm_col + 3)),
                  pl.BlockSpec((tm, d), lambda i: (i, 0)),
                  pl.BlockSpec((1, 1, d), lambda i: (i // per_batch, 0, 0)),
                  pl.BlockSpec((SSM_WIDTH, SSM_WIDTH), const),
                  pl.BlockSpec((1, SSM_WIDTH), const),
                  pl.BlockSpec((ATT_WIDTH, d), const),
                  pl.BlockSpec((SSM_WIDTH, d), const),
                  pl.BlockSpec((d, d), const),
                  pl.BlockSpec((1, d), const)],
        out_specs=pl.BlockSpec((tm, d), lambda i: (i, 0)),
        out_shape=jax.ShapeDtypeStruct((m, d), F32),
        compiler_params=pltpu.CompilerParams(dimension_semantics=("arbitrary",),
                                             vmem_limit_bytes=_vmem_limit(est)),
        name="merge_out",
    )(a_br, y_bm, qkv, qkv, qkv, qkv, qkv, x2d, gate, w_glu, b_glu, w_pa, w_ps, w_out, final_g)


def kernel(x, c, ctx, c_ctx, w_ada, b_ada, norm_g, w_in, lambda_q1, lambda_k1, lambda_q2, lambda_k2,
           subln_g, ssm_lambda_re, ssm_lambda_im, ssm_log_step, ssm_b_re, ssm_b_im, ssm_c_re, ssm_c_im,
           ssm_d, w_glu, b_glu, w_pa, w_ps, w_out, final_g):
    batch, seq, d = x.shape
    ctx_len = ctx.shape[1]
    assert w_ada.shape[0] == 1, "single-layer trunk"
    assert (d, seq % 1024, ctx_len % SSM_CHUNK, seq % SSM_CHUNK) == (D_MODEL, 0, 0, 0)

    pad = (-(batch + 1)) % SUBLANES
    cc = jnp.concatenate([c, c_ctx[None, :], jnp.zeros((pad, d), F32)], axis=0)
    mod = _adaln(cc, w_ada[0], b_ada[0][None, :])
    shift, scale, gate = (mod[:, k * d:(k + 1) * d] for k in range(3))
    as_mod = lambda v, lo, hi: v[lo:hi].reshape(hi - lo, 1, d)

    w_in_bf = w_in[0].astype(BF16)
    x2d = x.reshape(batch * seq, d)
    ctx2d = ctx.reshape(batch * ctx_len, d)
    ng = norm_g[0][None, :]

    tn = 1024
    u_col = 4 * ATT_WIDTH
    qkv = _in_proj(x2d, as_mod(scale, 0, batch), as_mod(shift, 0, batch), ng, w_in_bf,
                   list(range(IN_WIDTH // tn)), tn, seq, rope_tables=_rope_tables(seq))
    tnc = SSM_WIDTH
    k0 = ATT_WIDTH // tnc
    ctx_cols = list(range(k0, 3 * k0)) + [u_col // tnc]
    kvc = _in_proj(ctx2d, as_mod(scale, batch, batch + 1), as_mod(shift, batch, batch + 1), ng, w_in_bf,
                   ctx_cols, tnc, ctx_len)

    lam_vecs = jnp.stack([lambda_q1[0], lambda_k1[0], lambda_q2[0], lambda_k2[0]]).astype(F32)
    a_br = _attention(qkv, kvc, lam_vecs, subln_g[0][None, :].astype(F32), batch, seq, ctx_len)

    nd = ssm_lambda_re.shape[1]
    modes = lambda v: v[0].reshape(nd, 1, SSM_MODES)
    lstep = jnp.repeat(ssm_log_step[0], SSM_STATE, axis=-1).reshape(nd, 1, SSM_MODES)
    b_t = lambda v: jnp.tile(v[0].reshape(nd, SSM_MODES, SSM_GROUP).transpose(0, 2, 1), (1, GROUPS_PER_BLOCK, 1))
    c_t = lambda v: jnp.tile(v[0].transpose(0, 1, 3, 2).reshape(nd, SSM_MODES, SSM_GROUP), (1, 1, GROUPS_PER_BLOCK))
    a_disc, bblk, cblk = _ssm_prep(modes(ssm_lambda_re), modes(ssm_lambda_im), lstep,
                                   b_t(ssm_b_re), b_t(ssm_b_im), c_t(ssm_c_re), c_t(ssm_c_im))

    qkv3 = qkv.reshape(batch, seq, IN_WIDTH)
    kvc3 = kvc.reshape(batch, ctx_len, len(ctx_cols) * tnc)
    scan = functools.partial(_ssm_scan, kvc3, qkv3, len(ctx_cols) - 1, u_col // SSM_WIDTH, a_disc, bblk, cblk)
    yf = scan(0)
    y_bm = scan(1, yf=yf, dvec=ssm_d[0].reshape(1, SSM_WIDTH).astype(F32))

    out = _merge_out(a_br, y_bm.reshape(batch * seq, SSM_WIDTH), qkv, x2d, as_mod(gate, 0, batch),
                     w_glu[0].astype(BF16), b_glu[0][None, :], w_pa[0].astype(BF16), w_ps[0].astype(BF16),
                     w_out[0].astype(BF16), final_g[None, :], seq)
    return out.reshape(batch, seq, d)
```

```python
import functools
import math

import jax
import jax.numpy as jnp
import numpy as np
from jax import lax
from jax.experimental import pallas as pl
from jax.experimental.pallas import tpu as pltpu

F32 = jnp.float32
BF16 = jnp.bfloat16

D_MODEL = 2048
GRID_W = 64
ATT_HEADS = 8
ATT_DIM = 64
HEAD_W = 2 * ATT_DIM
ATT_WIDTH = ATT_HEADS * HEAD_W
SSM_GROUP = 16
SSM_GROUPS = 32
SSM_WIDTH = SSM_GROUP * SSM_GROUPS
SSM_STATE = 64
SSM_MODES = SSM_GROUPS * SSM_STATE
IN_WIDTH = 4 * ATT_WIDTH + 2 * SSM_WIDTH + 2 * D_MODEL
ROPE_BASE = 10000.0
NORM_EPS = 1e-6
LAM_INIT = 0.8 - 0.6 * math.exp(0.0)

LANES = 128
SUBLANES = 8
VMEM_BYTES_V7X = 64 * 1024 * 1024
MIB = 1024 * 1024

GROUPS_PER_BLOCK = LANES // SSM_GROUP
SSM_BLOCKS = SSM_GROUPS // GROUPS_PER_BLOCK
BLOCK_MODES = GROUPS_PER_BLOCK * SSM_STATE
BLOCK_STATE = 2 * BLOCK_MODES
SSM_CHUNK = 128
KEY_CHUNK = 256


def _vmem_limit(nbytes):
    return int(min(nbytes + nbytes // 4, VMEM_BYTES_V7X - 8 * MIB))


def _sigmoid(v):
    return 1.0 / (1.0 + jnp.exp(-v))


def _silu(v):
    return v * _sigmoid(v)


def _adaln_kernel(c_ref, w_ref, b_ref, o_ref):
    cv = c_ref[...]
    o_ref[...] = jnp.dot(_silu(cv), w_ref[...], preferred_element_type=F32,
                         precision=lax.Precision.HIGHEST) + b_ref[...]


def _adaln(cc, w_ada, b_ada):
    rows, d = cc.shape
    n = w_ada.shape[1]
    tn = 768
    est = 2 * (d * tn * 4) + 2 * rows * tn * 4 + rows * d * 4 * 2
    return pl.pallas_call(
        _adaln_kernel,
        grid=(n // tn,),
        in_specs=[pl.BlockSpec((rows, d), lambda j: (0, 0)),
                  pl.BlockSpec((d, tn), lambda j: (0, j)),
                  pl.BlockSpec((1, tn), lambda j: (0, j))],
        out_specs=pl.BlockSpec((rows, tn), lambda j: (0, j)),
        out_shape=jax.ShapeDtypeStruct((rows, n), F32),
        compiler_params=pltpu.CompilerParams(dimension_semantics=("arbitrary",),
                                             vmem_limit_bytes=_vmem_limit(est)),
        name="adaln",
    )(cc, w_ada, b_ada)


def _in_proj_kernel(*refs, rope, norm_rows):
    if rope:
        x_ref, scale_ref, shift_ref, g_ref, w_ref, cos_ref, sina_ref, sinb_ref, o_ref, h_ref = refs
    else:
        x_ref, scale_ref, shift_ref, g_ref, w_ref, o_ref, h_ref = refs
    j = pl.program_id(1)
    tm = x_ref.shape[0]

    @pl.when(j == 0)
    def _():
        gmod = g_ref[...] * (1.0 + scale_ref[0])
        shift = shift_ref[0]

        def body(r, carry):
            r0 = pl.multiple_of(r * norm_rows, norm_rows)
            xf = x_ref[pl.ds(r0, norm_rows), :]
            ms = jnp.mean(xf * xf, axis=-1, keepdims=True)
            y = xf * lax.rsqrt(ms + NORM_EPS) * gmod + shift
            h_ref[pl.ds(r0, norm_rows), :] = y.astype(BF16)
            return carry

        lax.fori_loop(0, tm // norm_rows, body, 0)

    acc = jnp.dot(h_ref[...], w_ref[...], preferred_element_type=F32)

    if rope:
        @pl.when(j < 2)
        def _():
            qscale = jnp.where(j == 0, ATT_DIM ** -0.5 * math.log2(math.e), 1.0).astype(F32)
            cos = cos_ref[...] * qscale
            sina = sina_ref[...] * qscale
            sinb = sinb_ref[...] * qscale
            for cb in range(acc.shape[1] // HEAD_W):
                v = acc[:, cb * HEAD_W:(cb + 1) * HEAD_W]
                up = pltpu.roll(v, HEAD_W - ATT_DIM // 4, 1)
                dn = pltpu.roll(v, ATT_DIM // 4, 1)
                o_ref[:, cb * HEAD_W:(cb + 1) * HEAD_W] = (v * cos + up * sina + dn * sinb).astype(o_ref.dtype)

        @pl.when(j >= 2)
        def _():
            o_ref[...] = acc.astype(o_ref.dtype)
    else:
        o_ref[...] = acc.astype(o_ref.dtype)


def _in_proj(x2d, scale, shift, norm_g, w_bf, col_blocks, tn, seq_len, rope_tables=None, tm=1024):
    m, d = x2d.shape
    nj = len(col_blocks)
    rope = rope_tables is not None
    seq_blocks = max(seq_len // tm, 1)
    per_batch_mod = scale.shape[0] > 1

    def mod_idx(i, j):
        return ((i // seq_blocks) if per_batch_mod else 0, 0, 0)

    contiguous = all(cb == col_blocks[0] + k for k, cb in enumerate(col_blocks))
    if contiguous:
        w_map = lambda i, j: (0, j + col_blocks[0])
    else:
        cbs = col_blocks
        def w_map(i, j):
            idx = jnp.int32(cbs[-1])
            for k in range(nj - 2, -1, -1):
                idx = jnp.where(j == k, jnp.int32(cbs[k]), idx)
            return (0, idx)

    in_specs = [pl.BlockSpec((tm, d), lambda i, j: (i, 0)),
                pl.BlockSpec((1, 1, d), mod_idx),
                pl.BlockSpec((1, 1, d), mod_idx),
                pl.BlockSpec((1, d), lambda i, j: (0, 0)),
                pl.BlockSpec((d, tn), w_map)]
    args = [x2d, scale, shift, norm_g, w_bf]
    if rope:
        for t in rope_tables:
            in_specs.append(pl.BlockSpec((tm, HEAD_W), lambda i, j: (i % seq_blocks, 0)))
            args.append(t)
    est = (2 * tm * d * 4 + tm * d * 2 + 2 * d * tn * 2 + 2 * tm * tn * 2 + 2 * tm * tn * 4
           + (6 * tm * HEAD_W * 4 if rope else 0))
    return pl.pallas_call(
        functools.partial(_in_proj_kernel, rope=rope, norm_rows=256),
        grid=(m // tm, nj),
        in_specs=in_specs,
        out_specs=pl.BlockSpec((tm, tn), lambda i, j: (i, j)),
        out_shape=jax.ShapeDtypeStruct((m, nj * tn), BF16),
        scratch_shapes=[pltpu.VMEM((tm, d), BF16)],
        compiler_params=pltpu.CompilerParams(dimension_semantics=("arbitrary", "arbitrary"),
                                             vmem_limit_bytes=_vmem_limit(est)),
        name="in_proj_rope" if rope else "in_proj_ctx",
    )(*args)


def _rope_tables(seq):
    rows = seq // GRID_W
    row = np.repeat(np.arange(rows), GRID_W).astype(np.float64)
    col = np.tile(np.arange(GRID_W), rows).astype(np.float64)
    nf = ATT_DIM // 4
    inv = ROPE_BASE ** (-np.arange(nf, dtype=np.float64) / nf)

    def axis_tables(pos):
        ang = pos[:, None] * inv[None, :]
        c, s = np.cos(ang), np.sin(ang)
        z = np.zeros_like(s)
        return (np.concatenate([c, c], -1), np.concatenate([-s, z], -1), np.concatenate([z, s], -1))

    parts = [axis_tables(row), axis_tables(col)]
    out = []
    for k in range(3):
        comp = np.concatenate([parts[0][k], parts[1][k]], -1)
        out.append(jnp.asarray(np.concatenate([comp, comp], -1), dtype=F32))
    return out


def _attn_kernel(q_ref, k_ref, v_ref, kc_ref, vc_ref, ga_ref, lam_ref, gcol_ref, o_ref,
                 qt_scr, kall_scr, vt_scr, s_scr, p_scr, *, rows):
    seq = k_ref.shape[0]
    keys = seq + kc_ref.shape[0]
    nblk = q_ref.shape[0] // rows
    nch = keys // KEY_CHUNK
    sub = KEY_CHUNK // SUBLANES
    lamv = lam_ref[...]
    lam = (jnp.exp(jnp.sum(lamv[0:1] * lamv[1:2], axis=-1, keepdims=True))
           - jnp.exp(jnp.sum(lamv[2:3] * lamv[3:4], axis=-1, keepdims=True)) + LAM_INIT)

    kall_scr[0:seq, :] = k_ref[...]
    kall_scr[seq:keys, :] = kc_ref[...]
    vt_scr[:, 0:seq] = v_ref[...].T
    vt_scr[:, seq:keys] = vc_ref[...].T
    drow = lax.broadcasted_iota(jnp.int32, (HEAD_W, rows), 0)
    for i in range(nblk):
        qt = q_ref[i * rows:(i + 1) * rows, :].T
        qt_scr[0, i] = jnp.where(drow < ATT_DIM, qt, jnp.zeros_like(qt))
        qt_scr[1, i] = jnp.where(drow >= ATT_DIM, qt, jnp.zeros_like(qt))

    def colmax(x):
        return jnp.max(x.reshape(sub, SUBLANES, rows), axis=0)

    def colsum(x):
        return jnp.sum(x.reshape(sub, SUBLANES, rows), axis=0)

    def step(t, slot, do_qk, do_sm, do_pv, mfin, rfin, cfin):
        mpart = [None, None]
        lpart = [None, None]
        acc = None
        for c in range(nch):
            ks = slice(c * KEY_CHUNK, (c + 1) * KEY_CHUNK)
            if do_qk:
                kc = kall_scr[ks, :]
                for comp in range(2):
                    s = jnp.dot(kc, qt_scr[comp, t], preferred_element_type=F32)
                    s_scr[slot, comp, c] = s
                    cm = colmax(s)
                    mpart[comp] = cm if mpart[comp] is None else jnp.maximum(mpart[comp], cm)
            if do_sm:
                for comp in range(2):
                    p = jnp.exp2(s_scr[1 - slot, comp, c] - mfin[comp])
                    p_scr[1 - slot, comp, c] = p.astype(BF16)
                    cs = colsum(p)
                    lpart[comp] = cs if lpart[comp] is None else lpart[comp] + cs
            if do_pv:
                a = p_scr[slot, 0, c] - rfin * p_scr[slot, 1, c]
                d = jnp.dot(vt_scr[:, ks], a, preferred_element_type=F32)
                acc = d if acc is None else acc + d
        if do_pv:
            r0 = pl.multiple_of((t - 2) * rows, rows)
            o = acc * cfin
            ms = jnp.mean(o * o, axis=0, keepdims=True)
            on = (o * lax.rsqrt(ms + NORM_EPS) * gcol_ref[...] * (1.0 - LAM_INIT)).T
            o_ref[pl.ds(r0, rows), :] = (on * _silu(ga_ref[pl.ds(r0, rows), :].astype(F32))).astype(o_ref.dtype)
        new_m = [jnp.max(m, axis=0, keepdims=True) for m in mpart] if do_qk else mfin
        if do_sm:
            l = [jnp.sum(x, axis=0, keepdims=True) for x in lpart]
            return new_m, (lam * l[0] / l[1]).astype(BF16), 1.0 / l[0]
        return new_m, rfin, cfin

    m0, _, _ = step(0, 0, True, False, False, None, None, None)
    m1, r0, c0 = step(1, 1, True, True, False, m0, None, None)

    def body(tt, carry):
        ma, mb, r, c = carry
        t = 2 * tt
        m, r, c = step(t, 0, True, True, True, [ma, mb], r, c)
        m, r, c = step(t + 1, 1, True, True, True, m, r, c)
        return m[0], m[1], r, c

    ma, mb, r1, c1 = lax.fori_loop(1, nblk // 2, body, (m1[0], m1[1], r0, c0))
    _, r2, c2 = step(nblk, nblk % 2, False, True, True, [ma, mb], r1, c1)
    step(nblk + 1, (nblk + 1) % 2, False, False, True, None, r2, c2)


def _attention(qkv, kvc, lam_vecs, subln_g, batch, seq, ctx_len, rows=256):
    kcol = ATT_WIDTH // HEAD_W
    keys = seq + ctx_len
    nch = keys // KEY_CHUNK
    assert keys % KEY_CHUNK == 0 and seq % (2 * rows) == 0
    gcol = jnp.broadcast_to(subln_g.reshape(HEAD_W, 1), (HEAD_W, rows))
    est = (2 * 2 * keys * rows * (4 + 2) + 2 * seq * HEAD_W * 2 + 2 * keys * HEAD_W * 2
           + 2 * (4 * seq + 2 * ctx_len) * HEAD_W * 2 + 64 * KEY_CHUNK * rows * 4)
    return pl.pallas_call(
        functools.partial(_attn_kernel, rows=rows),
        grid=(batch, ATT_HEADS),
        in_specs=[pl.BlockSpec((seq, HEAD_W), lambda b, h: (b, h)),
                  pl.BlockSpec((seq, HEAD_W), lambda b, h: (b, kcol + h)),
                  pl.BlockSpec((seq, HEAD_W), lambda b, h: (b, 2 * kcol + h)),
                  pl.BlockSpec((ctx_len, HEAD_W), lambda b, h: (b, h)),
                  pl.BlockSpec((ctx_len, HEAD_W), lambda b, h: (b, kcol + h)),
                  pl.BlockSpec((seq, HEAD_W), lambda b, h: (b, 3 * kcol + h)),
                  pl.BlockSpec((4, ATT_DIM), lambda b, h: (0, 0)),
                  pl.BlockSpec((HEAD_W, rows), lambda b, h: (0, 0))],
        out_specs=pl.BlockSpec((seq, HEAD_W), lambda b, h: (b, h)),
        out_shape=jax.ShapeDtypeStruct((batch * seq, ATT_WIDTH), BF16),
        scratch_shapes=[pltpu.VMEM((2, seq // rows, HEAD_W, rows), BF16),
                        pltpu.VMEM((keys, HEAD_W), BF16),
                        pltpu.VMEM((HEAD_W, keys), BF16),
                        pltpu.VMEM((2, 2, nch, KEY_CHUNK, rows), F32),
                        pltpu.VMEM((2, 2, nch, KEY_CHUNK, rows), BF16)],
        compiler_params=pltpu.CompilerParams(dimension_semantics=("arbitrary",) * 2,
                                             vmem_limit_bytes=_vmem_limit(est)),
        name="diff_attention",
    )(qkv, qkv, qkv, kvc, kvc, qkv, lam_vecs, gcol)


def _ssm_prep_kernel(lre_ref, lim_ref, lstep_ref, bre_ref, bim_ref, cre_ref, cim_ref,
                     a_ref, bblk_ref, cblk_ref):
    lre = lre_ref[0]
    lim = lim_ref[0]
    dt = jnp.exp(lstep_ref[0])
    mag = jnp.exp(lre * dt)
    ar = mag * jnp.cos(lim * dt)
    ai = mag * jnp.sin(lim * dt)
    den = lre * lre + lim * lim
    fr = ((ar - 1.0) * lre + ai * lim) / den
    fi = (ai * lre - (ar - 1.0) * lim) / den
    a_ref[0, 0:1, :] = ar
    a_ref[0, 1:2, :] = ai
    br = bre_ref[0]
    bi = bim_ref[0]
    bbr = fr * br - fi * bi
    bbi = fr * bi + fi * br
    row_g = lax.broadcasted_iota(jnp.int32, (LANES, BLOCK_MODES), 0) // SSM_GROUP
    col_g = lax.broadcasted_iota(jnp.int32, (LANES, BLOCK_MODES), 1) // SSM_STATE
    keep_b = row_g == col_g
    row_g2 = lax.broadcasted_iota(jnp.int32, (BLOCK_MODES, LANES), 0) // SSM_STATE
    col_g2 = lax.broadcasted_iota(jnp.int32, (BLOCK_MODES, LANES), 1) // SSM_GROUP
    keep_c = row_g2 == col_g2
    for k in range(SSM_BLOCKS):
        sl = slice(k * BLOCK_MODES, (k + 1) * BLOCK_MODES)
        bblk_ref[0, k, :, 0:BLOCK_MODES] = jnp.where(keep_b, bbr[:, sl], 0.0).astype(BF16)
        bblk_ref[0, k, :, BLOCK_MODES:BLOCK_STATE] = jnp.where(keep_b, bbi[:, sl], 0.0).astype(BF16)
        cblk_ref[0, k, 0:BLOCK_MODES, :] = jnp.where(keep_c, cre_ref[0, sl, :], 0.0).astype(BF16)
        cblk_ref[0, k, BLOCK_MODES:BLOCK_STATE, :] = jnp.where(keep_c, -cim_ref[0, sl, :], 0.0).astype(BF16)


def _ssm_prep(lre, lim, lstep, bre_t, bim_t, cre_t, cim_t):
    nd = lre.shape[0]
    vec = pl.BlockSpec((1, 1, SSM_MODES), lambda d: (d, 0, 0))
    bsp = pl.BlockSpec((1, LANES, SSM_MODES), lambda d: (d, 0, 0))
    csp = pl.BlockSpec((1, SSM_MODES, LANES), lambda d: (d, 0, 0))
    est = 2 * (4 * LANES * SSM_MODES * 4 + 2 * SSM_BLOCKS * LANES * BLOCK_STATE * 2) + 8 * LANES * SSM_MODES * 4
    return pl.pallas_call(
        _ssm_prep_kernel,
        grid=(nd,),
        in_specs=[vec, vec, vec, bsp, bsp, csp, csp],
        out_specs=[pl.BlockSpec((1, 2, SSM_MODES), lambda d: (d, 0, 0)),
                   pl.BlockSpec((1, SSM_BLOCKS, LANES, BLOCK_STATE), lambda d: (d, 0, 0, 0)),
                   pl.BlockSpec((1, SSM_BLOCKS, BLOCK_STATE, LANES), lambda d: (d, 0, 0, 0))],
        out_shape=[jax.ShapeDtypeStruct((nd, 2, SSM_MODES), F32),
                   jax.ShapeDtypeStruct((nd, SSM_BLOCKS, LANES, BLOCK_STATE), BF16),
                   jax.ShapeDtypeStruct((nd, SSM_BLOCKS, BLOCK_STATE, LANES), BF16)],
        compiler_params=pltpu.CompilerParams(dimension_semantics=("arbitrary",),
                                             vmem_limit_bytes=_vmem_limit(est)),
        name="ssm_prep",
    )(lre, lim, lstep, bre_t, bim_t, cre_t, cim_t)


def _ssm_scan_kernel(*refs, reverse, batch, ctx_chunks):
    if reverse:
        uc_ref, ul_ref, a_ref, bblk_ref, cblk_ref, yf_ref, d_ref, y_ref, s_ref, h_ref = refs
    else:
        uc_ref, ul_ref, a_ref, bblk_ref, cblk_ref, y_ref, s_ref, h_ref = refs
    steps = ul_ref.shape[1]
    rows = steps * batch
    step = pl.program_id(0)

    @pl.when(step == 0)
    def _():
        h_ref[...] = jnp.zeros_like(h_ref)

    u_bt = jnp.where(step < ctx_chunks, uc_ref[...], ul_ref[...]).astype(F32)
    u32 = jnp.swapaxes(u_bt, 0, 1).reshape(rows, SSM_WIDTH)
    u = u32.astype(BF16)
    for k in range(SSM_BLOCKS):
        s_ref[:, k * BLOCK_STATE:(k + 1) * BLOCK_STATE] = jnp.dot(
            u[:, k * LANES:(k + 1) * LANES], bblk_ref[0, k], preferred_element_type=F32)

    for k in range(SSM_BLOCKS):
        c_re = k * BLOCK_STATE
        c_im = c_re + BLOCK_MODES
        ar = jnp.broadcast_to(a_ref[0, 0:1, k * BLOCK_MODES:(k + 1) * BLOCK_MODES], (batch, BLOCK_MODES))
        ai = jnp.broadcast_to(a_ref[0, 1:2, k * BLOCK_MODES:(k + 1) * BLOCK_MODES], (batch, BLOCK_MODES))
        hr = h_ref[:, c_re:c_re + BLOCK_MODES]
        hi = h_ref[:, c_im:c_im + BLOCK_MODES]
        for i in range(steps):
            r0 = ((steps - 1 - i) if reverse else i) * batch
            nhr = ar * hr - ai * hi + s_ref[r0:r0 + batch, c_re:c_re + BLOCK_MODES]
            nhi = ar * hi + ai * hr + s_ref[r0:r0 + batch, c_im:c_im + BLOCK_MODES]
            s_ref[r0:r0 + batch, c_re:c_re + BLOCK_MODES] = nhr
            s_ref[r0:r0 + batch, c_im:c_im + BLOCK_MODES] = nhi
            hr, hi = nhr, nhi
        h_ref[:, c_re:c_re + BLOCK_MODES] = hr
        h_ref[:, c_im:c_im + BLOCK_MODES] = hi

    ys = []
    for k in range(SSM_BLOCKS):
        yk = jnp.dot(s_ref[:, k * BLOCK_STATE:(k + 1) * BLOCK_STATE].astype(BF16), cblk_ref[0, k],
                     preferred_element_type=F32)
        cols = slice(k * LANES, (k + 1) * LANES)
        if reverse:
            yk = yk + yf_ref[:, cols] + d_ref[:, cols] * u32[:, cols]
        ys.append(yk)
    y = jnp.concatenate(ys, axis=1)
    if reverse:
        y_ref[...] = jnp.swapaxes(y.reshape(steps, batch, SSM_WIDTH), 0, 1)
    else:
        y_ref[...] = y


def _ssm_scan(u_ctx, u_lat, u_ctx_col, u_lat_col, a, bblk, cblk, direction, yf=None, dvec=None):
    batch = u_lat.shape[0]
    rows = SSM_CHUNK * batch
    cc = u_ctx.shape[1] // SSM_CHUNK
    lc = u_lat.shape[1] // SSM_CHUNK
    reverse = direction == 1
    if reverse:
        cidx = lambda s: jnp.maximum(cc - 1 - s, 0)
        lidx = lambda s: jnp.minimum(lc - 1, lc - 1 + cc - s)
    else:
        cidx = lambda s: jnp.minimum(s, cc - 1)
        lidx = lambda s: jnp.maximum(s - cc, 0)
    in_specs = [pl.BlockSpec((batch, SSM_CHUNK, SSM_WIDTH), lambda s: (0, cidx(s), u_ctx_col)),
                pl.BlockSpec((batch, SSM_CHUNK, SSM_WIDTH), lambda s: (0, lidx(s), u_lat_col)),
                pl.BlockSpec((1, 2, SSM_MODES), lambda s: (direction, 0, 0)),
                pl.BlockSpec((1, SSM_BLOCKS, LANES, BLOCK_STATE), lambda s: (direction, 0, 0, 0)),
                pl.BlockSpec((1, SSM_BLOCKS, BLOCK_STATE, LANES), lambda s: (direction, 0, 0, 0))]
    args = [u_ctx, u_lat, a, bblk, cblk]
    if reverse:
        in_specs += [pl.BlockSpec((rows, SSM_WIDTH), lambda s: (lidx(s), 0)),
                     pl.BlockSpec((1, SSM_WIDTH), lambda s: (0, 0))]
        args += [yf, dvec]
        out_spec = pl.BlockSpec((batch, SSM_CHUNK, SSM_WIDTH), lambda s: (0, lidx(s), 0))
        out_shape = jax.ShapeDtypeStruct((batch, lc * SSM_CHUNK, SSM_WIDTH), F32)
    else:
        out_spec = pl.BlockSpec((rows, SSM_WIDTH), lambda s: (lidx(s), 0))
        out_shape = jax.ShapeDtypeStruct((lc * rows, SSM_WIDTH), F32)
    state_w = SSM_BLOCKS * BLOCK_STATE
    est = (rows * state_w * 4 + 2 * rows * state_w // SSM_BLOCKS * 4 + 10 * rows * SSM_WIDTH * 4
           + 4 * SSM_BLOCKS * LANES * BLOCK_STATE * 2)
    return pl.pallas_call(
        functools.partial(_ssm_scan_kernel, reverse=reverse, batch=batch, ctx_chunks=cc),
        grid=(cc + lc,),
        in_specs=in_specs,
        out_specs=out_spec,
        out_shape=out_shape,
        scratch_shapes=[pltpu.VMEM((rows, state_w), F32), pltpu.VMEM((batch, state_w), F32)],
        compiler_params=pltpu.CompilerParams(dimension_semantics=("arbitrary",),
                                             vmem_limit_bytes=_vmem_limit(est)),
        name="ssm_scan_bwd" if reverse else "ssm_scan_fwd",
    )(*args)


def _gelu_tanh(v):
    return 0.5 * v * (1.0 + jnp.tanh(math.sqrt(2.0 / math.pi) * (v + 0.044715 * v * v * v)))


def _merge_kernel(a_ref, y_ref, gs_ref, gma0_ref, gma1_ref, gms0_ref, gms1_ref, x_ref, gate_ref,
                  wglu_ref, bglu_ref, wpa_ref, wps_ref, wout_ref, fg_ref, o_ref):
    yg = _gelu_tanh(y_ref[...])
    z = yg * _sigmoid(jnp.dot(yg.astype(BF16), wglu_ref[...], preferred_element_type=F32) + bglu_ref[...])
    s_br = (z * _silu(gs_ref[...].astype(F32))).astype(BF16)
    ta = jnp.dot(a_ref[...], wpa_ref[...], preferred_element_type=F32)
    ts = jnp.dot(s_br, wps_ref[...], preferred_element_type=F32)
    half = D_MODEL // 2
    parts = []
    for p, (ga_r, gs_r) in enumerate(((gma0_ref, gms0_ref), (gma1_ref, gms1_ref))):
        cols = slice(p * half, (p + 1) * half)
        parts.append((_sigmoid(ga_r[...].astype(F32)) * ta[:, cols]
                      + _sigmoid(gs_r[...].astype(F32)) * ts[:, cols]).astype(BF16))
    t = jnp.concatenate(parts, axis=1)
    out = jnp.dot(t, wout_ref[...], preferred_element_type=F32)
    xn = x_ref[...] + gate_ref[0] * out
    ms = jnp.mean(xn * xn, axis=-1, keepdims=True)
    o_ref[...] = xn * lax.rsqrt(ms + NORM_EPS) * fg_ref[...]


def _merge_out(a_br, y_bm, qkv, x2d, gate, w_glu, b_glu, w_pa, w_ps, w_out, final_g, seq, tm=256):
    m, d = x2d.shape
    half = d // 2
    per_batch = seq // tm
    gs_col = (4 * ATT_WIDTH + SSM_WIDTH) // SSM_WIDTH
    gm_col = (4 * ATT_WIDTH + 2 * SSM_WIDTH) // half
    const = lambda i: (0, 0)
    est = (2 * (tm * ATT_WIDTH * 2 + tm * SSM_WIDTH * 6 + 4 * tm * half * 2 + 2 * tm * d * 4)
           + 2 * (SSM_WIDTH * SSM_WIDTH + ATT_WIDTH * d + SSM_WIDTH * d + d * d) * 2 + 6 * tm * d * 4)
    return pl.pallas_call(
        _merge_kernel,
        grid=(m // tm,),
        in_specs=[pl.BlockSpec((tm, ATT_WIDTH), lambda i: (i, 0)),
                  pl.BlockSpec((tm, SSM_WIDTH), lambda i: (i, 0)),
                  pl.BlockSpec((tm, SSM_WIDTH), lambda i: (i, gs_col)),
                  pl.BlockSpec((tm, half), lambda i: (i, gm_col)),
                  pl.BlockSpec((tm, half), lambda i: (i, gm_col + 1)),
                  pl.BlockSpec((tm, half), lambda i: (i, gm_col + 2)),
                  pl.BlockSpec((tm, half), lambda i: (i, gm_col + 3)),
                  pl.BlockSpec((tm, d), lambda i: (i, 0)),
                  pl.BlockSpec((1, 1, d), lambda i: (i // per_batch, 0, 0)),
                  pl.BlockSpec((SSM_WIDTH, SSM_WIDTH), const),
                  pl.BlockSpec((1, SSM_WIDTH), const),
                  pl.BlockSpec((ATT_WIDTH, d), const),
                  pl.BlockSpec((SSM_WIDTH, d), const),
                  pl.BlockSpec((d, d), const),
                  pl.BlockSpec((1, d), const)],
        out_specs=pl.BlockSpec((tm, d), lambda i: (i, 0)),
        out_shape=jax.ShapeDtypeStruct((m, d), F32),
        compiler_params=pltpu.CompilerParams(dimension_semantics=("arbitrary",),
                                             vmem_limit_bytes=_vmem_limit(est)),
        name="merge_out",
    )(a_br, y_bm, qkv, qkv, qkv, qkv, qkv, x2d, gate, w_glu, b_glu, w_pa, w_ps, w_out, final_g)


def kernel(x, c, ctx, c_ctx, w_ada, b_ada, norm_g, w_in, lambda_q1, lambda_k1, lambda_q2, lambda_k2,
           subln_g, ssm_lambda_re, ssm_lambda_im, ssm_log_step, ssm_b_re, ssm_b_im, ssm_c_re, ssm_c_im,
           ssm_d, w_glu, b_glu, w_pa, w_ps, w_out, final_g):
    batch, seq, d = x.shape
    ctx_len = ctx.shape[1]
    assert w_ada.shape[0] == 1, "single-layer trunk"
    assert (d, seq % 1024, ctx_len % SSM_CHUNK, seq % SSM_CHUNK) == (D_MODEL, 0, 0, 0)

    pad = (-(batch + 1)) % SUBLANES
    cc = jnp.concatenate([c, c_ctx[None, :], jnp.zeros((pad, d), F32)], axis=0)
    mod = _adaln(cc, w_ada[0], b_ada[0][None, :])
    shift, scale, gate = (mod[:, k * d:(k + 1) * d] for k in range(3))
    as_mod = lambda v, lo, hi: v[lo:hi].reshape(hi - lo, 1, d)

    w_in_bf = w_in[0].astype(BF16)
    x2d = x.reshape(batch * seq, d)
    ctx2d = ctx.reshape(batch * ctx_len, d)
    ng = norm_g[0][None, :]

    tn = 1024
    u_col = 4 * ATT_WIDTH
    qkv = _in_proj(x2d, as_mod(scale, 0, batch), as_mod(shift, 0, batch), ng, w_in_bf,
                   list(range(IN_WIDTH // tn)), tn, seq, rope_tables=_rope_tables(seq))
    tnc = SSM_WIDTH
    k0 = ATT_WIDTH // tnc
    ctx_cols = list(range(k0, 3 * k0)) + [u_col // tnc]
    kvc = _in_proj(ctx2d, as_mod(scale, batch, batch + 1), as_mod(shift, batch, batch + 1), ng, w_in_bf,
                   ctx_cols, tnc, ctx_len)

    lam_vecs = jnp.stack([lambda_q1[0], lambda_k1[0], lambda_q2[0], lambda_k2[0]]).astype(F32)
    a_br = _attention(qkv, kvc, lam_vecs, subln_g[0].astype(F32), batch, seq, ctx_len)

    nd = ssm_lambda_re.shape[1]
    modes = lambda v: v[0].reshape(nd, 1, SSM_MODES)
    lstep = jnp.repeat(ssm_log_step[0], SSM_STATE, axis=-1).reshape(nd, 1, SSM_MODES)
    b_t = lambda v: jnp.tile(v[0].reshape(nd, SSM_MODES, SSM_GROUP).transpose(0, 2, 1), (1, GROUPS_PER_BLOCK, 1))
    c_t = lambda v: jnp.tile(v[0].transpose(0, 1, 3, 2).reshape(nd, SSM_MODES, SSM_GROUP), (1, 1, GROUPS_PER_BLOCK))
    a_disc, bblk, cblk = _ssm_prep(modes(ssm_lambda_re), modes(ssm_lambda_im), lstep,
                                   b_t(ssm_b_re), b_t(ssm_b_im), c_t(ssm_c_re), c_t(ssm_c_im))

    qkv3 = qkv.reshape(batch, seq, IN_WIDTH)
    kvc3 = kvc.reshape(batch, ctx_len, len(ctx_cols) * tnc)
    scan = functools.partial(_ssm_scan, kvc3, qkv3, len(ctx_cols) - 1, u_col // SSM_WIDTH, a_disc, bblk, cblk)
    yf = scan(0)
    y_bm = scan(1, yf=yf, dvec=ssm_d[0].reshape(1, SSM_WIDTH).astype(F32))

    out = _merge_out(a_br, y_bm.reshape(batch * seq, SSM_WIDTH), qkv, x2d, as_mod(gate, 0, batch),
                     w_glu[0].astype(BF16), b_glu[0][None, :], w_pa[0].astype(BF16), w_ps[0].astype(BF16),
                     w_out[0].astype(BF16), final_g[None, :], seq)
    return out.reshape(batch, seq, d)
```

```python
import functools
import math

import jax
import jax.numpy as jnp
import numpy as np
from jax import lax
from jax.experimental import pallas as pl
from jax.experimental.pallas import tpu as pltpu

F32 = jnp.float32
BF16 = jnp.bfloat16

D_MODEL = 2048
GRID_W = 64
ATT_HEADS = 8
ATT_DIM = 64
HEAD_W = 2 * ATT_DIM
ATT_WIDTH = ATT_HEADS * HEAD_W
SSM_GROUP = 16
SSM_GROUPS = 32
SSM_WIDTH = SSM_GROUP * SSM_GROUPS
SSM_STATE = 64
SSM_MODES = SSM_GROUPS * SSM_STATE
IN_WIDTH = 4 * ATT_WIDTH + 2 * SSM_WIDTH + 2 * D_MODEL
ROPE_BASE = 10000.0
NORM_EPS = 1e-6
LAM_INIT = 0.8 - 0.6 * math.exp(0.0)

LANES = 128
SUBLANES = 8
VMEM_BYTES_V7X = 64 * 1024 * 1024
MIB = 1024 * 1024

GROUPS_PER_BLOCK = LANES // SSM_GROUP
SSM_BLOCKS = SSM_GROUPS // GROUPS_PER_BLOCK
BLOCK_MODES = GROUPS_PER_BLOCK * SSM_STATE
BLOCK_STATE = 2 * BLOCK_MODES
SSM_CHUNK = 128
KEY_CHUNK = 256
VT_PAD = 16


def _vmem_limit(nbytes):
    return int(min(nbytes + nbytes // 4, VMEM_BYTES_V7X - 8 * MIB))


def _sigmoid(v):
    return 1.0 / (1.0 + jnp.exp(-v))


def _silu(v):
    return v * _sigmoid(v)


def _adaln_kernel(c_ref, w_ref, b_ref, o_ref):
    cv = c_ref[...]
    o_ref[...] = jnp.dot(_silu(cv), w_ref[...], preferred_element_type=F32,
                         precision=lax.Precision.HIGHEST) + b_ref[...]


def _adaln(cc, w_ada, b_ada):
    rows, d = cc.shape
    n = w_ada.shape[1]
    tn = 768
    est = 2 * (d * tn * 4) + 2 * rows * tn * 4 + rows * d * 4 * 2
    return pl.pallas_call(
        _adaln_kernel,
        grid=(n // tn,),
        in_specs=[pl.BlockSpec((rows, d), lambda j: (0, 0)),
                  pl.BlockSpec((d, tn), lambda j: (0, j)),
                  pl.BlockSpec((1, tn), lambda j: (0, j))],
        out_specs=pl.BlockSpec((rows, tn), lambda j: (0, j)),
        out_shape=jax.ShapeDtypeStruct((rows, n), F32),
        compiler_params=pltpu.CompilerParams(dimension_semantics=("arbitrary",),
                                             vmem_limit_bytes=_vmem_limit(est)),
        name="adaln",
    )(cc, w_ada, b_ada)


def _in_proj_kernel(*refs, rope, norm_rows):
    if rope:
        x_ref, scale_ref, shift_ref, g_ref, w_ref, cos_ref, sina_ref, sinb_ref, o_ref, h_ref = refs
    else:
        x_ref, scale_ref, shift_ref, g_ref, w_ref, o_ref, h_ref = refs
    j = pl.program_id(1)
    tm = x_ref.shape[0]

    @pl.when(j == 0)
    def _():
        gmod = g_ref[...] * (1.0 + scale_ref[0])
        shift = shift_ref[0]

        def body(r, carry):
            r0 = pl.multiple_of(r * norm_rows, norm_rows)
            xf = x_ref[pl.ds(r0, norm_rows), :]
            ms = jnp.mean(xf * xf, axis=-1, keepdims=True)
            y = xf * lax.rsqrt(ms + NORM_EPS) * gmod + shift
            h_ref[pl.ds(r0, norm_rows), :] = y.astype(BF16)
            return carry

        lax.fori_loop(0, tm // norm_rows, body, 0)

    acc = jnp.dot(h_ref[...], w_ref[...], preferred_element_type=F32)

    if rope:
        @pl.when(j < 2)
        def _():
            qscale = jnp.where(j == 0, ATT_DIM ** -0.5 * math.log2(math.e), 1.0).astype(F32)
            cos = cos_ref[...] * qscale
            sina = sina_ref[...] * qscale
            sinb = sinb_ref[...] * qscale
            for cb in range(acc.shape[1] // HEAD_W):
                v = acc[:, cb * HEAD_W:(cb + 1) * HEAD_W]
                up = pltpu.roll(v, HEAD_W - ATT_DIM // 4, 1)
                dn = pltpu.roll(v, ATT_DIM // 4, 1)
                o_ref[:, cb * HEAD_W:(cb + 1) * HEAD_W] = (v * cos + up * sina + dn * sinb).astype(o_ref.dtype)

        @pl.when(j >= 2)
        def _():
            o_ref[...] = acc.astype(o_ref.dtype)
    else:
        o_ref[...] = acc.astype(o_ref.dtype)


def _in_proj(x2d, scale, shift, norm_g, w_bf, col_blocks, tn, seq_len, rope_tables=None, tm=1024):
    m, d = x2d.shape
    nj = len(col_blocks)
    rope = rope_tables is not None
    seq_blocks = max(seq_len // tm, 1)
    per_batch_mod = scale.shape[0] > 1

    def mod_idx(i, j):
        return ((i // seq_blocks) if per_batch_mod else 0, 0, 0)

    contiguous = all(cb == col_blocks[0] + k for k, cb in enumerate(col_blocks))
    if contiguous:
        w_map = lambda i, j: (0, j + col_blocks[0])
    else:
        cbs = col_blocks
        def w_map(i, j):
            idx = jnp.int32(cbs[-1])
            for k in range(nj - 2, -1, -1):
                idx = jnp.where(j == k, jnp.int32(cbs[k]), idx)
            return (0, idx)

    in_specs = [pl.BlockSpec((tm, d), lambda i, j: (i, 0)),
                pl.BlockSpec((1, 1, d), mod_idx),
                pl.BlockSpec((1, 1, d), mod_idx),
                pl.BlockSpec((1, d), lambda i, j: (0, 0)),
                pl.BlockSpec((d, tn), w_map)]
    args = [x2d, scale, shift, norm_g, w_bf]
    if rope:
        for t in rope_tables:
            in_specs.append(pl.BlockSpec((tm, HEAD_W), lambda i, j: (i % seq_blocks, 0)))
            args.append(t)
    est = (2 * tm * d * 4 + tm * d * 2 + 2 * d * tn * 2 + 2 * tm * tn * 2 + 2 * tm * tn * 4
           + (6 * tm * HEAD_W * 4 if rope else 0))
    return pl.pallas_call(
        functools.partial(_in_proj_kernel, rope=rope, norm_rows=256),
        grid=(m // tm, nj),
        in_specs=in_specs,
        out_specs=pl.BlockSpec((tm, tn), lambda i, j: (i, j)),
        out_shape=jax.ShapeDtypeStruct((m, nj * tn), BF16),
        scratch_shapes=[pltpu.VMEM((tm, d), BF16)],
        compiler_params=pltpu.CompilerParams(dimension_semantics=("arbitrary", "arbitrary"),
                                             vmem_limit_bytes=_vmem_limit(est)),
        name="in_proj_rope" if rope else "in_proj_ctx",
    )(*args)


def _rope_tables(seq):
    rows = seq // GRID_W
    row = np.repeat(np.arange(rows), GRID_W).astype(np.float64)
    col = np.tile(np.arange(GRID_W), rows).astype(np.float64)
    nf = ATT_DIM // 4
    inv = ROPE_BASE ** (-np.arange(nf, dtype=np.float64) / nf)

    def axis_tables(pos):
        ang = pos[:, None] * inv[None, :]
        c, s = np.cos(ang), np.sin(ang)
        z = np.zeros_like(s)
        return (np.concatenate([c, c], -1), np.concatenate([-s, z], -1), np.concatenate([z, s], -1))

    parts = [axis_tables(row), axis_tables(col)]
    out = []
    for k in range(3):
        comp = np.concatenate([parts[0][k], parts[1][k]], -1)
        out.append(jnp.asarray(np.concatenate([comp, comp], -1), dtype=F32))
    return out


def _attn_kernel(q_ref, k_ref, v_ref, kc_ref, vc_ref, ga_ref, lam_ref, gcol_ref, o_ref,
                 qt_scr, kall_scr, vt_scr, s_scr, *, rows):
    seq = k_ref.shape[0]
    keys = seq + kc_ref.shape[0]
    nblk = q_ref.shape[0] // rows
    nch = keys // KEY_CHUNK
    sub = KEY_CHUNK // SUBLANES
    lamv = lam_ref[...]
    lam = (jnp.exp(jnp.sum(lamv[0:1] * lamv[1:2], axis=-1, keepdims=True))
           - jnp.exp(jnp.sum(lamv[2:3] * lamv[3:4], axis=-1, keepdims=True)) + LAM_INIT)

    kall_scr[0:seq, :] = k_ref[...]
    kall_scr[seq:keys, :] = kc_ref[...]
    vt_scr[0:HEAD_W, 0:seq] = v_ref[...].T
    vt_scr[0:HEAD_W, seq:keys] = vc_ref[...].T
    pad_row = lax.broadcasted_iota(jnp.int32, (VT_PAD, keys), 0)
    vt_scr[HEAD_W:HEAD_W + VT_PAD, :] = jnp.where(pad_row == 0, 1.0, 0.0).astype(BF16)
    drow = lax.broadcasted_iota(jnp.int32, (HEAD_W, rows), 0)
    for i in range(nblk):
        qt = q_ref[i * rows:(i + 1) * rows, :].T
        qt_scr[0, i] = jnp.where(drow < ATT_DIM, qt, jnp.zeros_like(qt))
        qt_scr[1, i] = jnp.where(drow >= ATT_DIM, qt, jnp.zeros_like(qt))

    def colmax(x):
        return jnp.max(x.reshape(sub, SUBLANES, rows), axis=0)

    def step(t, slot, do_qk, do_sm, mfin):
        mpart = [None, None]
        acc = [None, None]
        for c in range(nch):
            ks = slice(c * KEY_CHUNK, (c + 1) * KEY_CHUNK)
            if do_qk:
                kc = kall_scr[ks, :]
                for comp in range(2):
                    s = jnp.dot(kc, qt_scr[comp, t], preferred_element_type=F32)
                    s_scr[slot, comp, c] = s
                    cm = colmax(s)
                    mpart[comp] = cm if mpart[comp] is None else jnp.maximum(mpart[comp], cm)
            if do_sm:
                vt = vt_scr[:, ks]
                for comp in range(2):
                    p = jnp.exp2(s_scr[1 - slot, comp, c] - mfin[comp]).astype(BF16)
                    d = jnp.dot(vt, p, preferred_element_type=F32)
                    acc[comp] = d if acc[comp] is None else acc[comp] + d
        if do_sm:
            r0 = pl.multiple_of((t - 1) * rows, rows)
            l1 = acc[0][HEAD_W:HEAD_W + 1]
            l2 = acc[1][HEAD_W:HEAD_W + 1]
            o = acc[0][0:HEAD_W] * (1.0 / l1) - acc[1][0:HEAD_W] * (lam / l2)
            ms = jnp.mean(o * o, axis=0, keepdims=True)
            on = (o * lax.rsqrt(ms + NORM_EPS) * gcol_ref[...] * (1.0 - LAM_INIT)).T
            o_ref[pl.ds(r0, rows), :] = (on * _silu(ga_ref[pl.ds(r0, rows), :].astype(F32))).astype(o_ref.dtype)
        return [jnp.max(m, axis=0, keepdims=True) for m in mpart] if do_qk else mfin

    m = step(0, 0, True, False, None)

    def body(tt, carry):
        t = 2 * tt + 1
        mm = step(t, 1, True, True, list(carry))
        mm = step(t + 1, 0, True, True, mm)
        return mm[0], mm[1]

    m = lax.fori_loop(0, (nblk - 2) // 2, body, (m[0], m[1]))
    m = step(nblk - 1, 1, True, True, list(m))
    step(nblk, 0, False, True, m)


def _attention(qkv, kvc, lam_vecs, subln_g, batch, seq, ctx_len, rows=256):
    kcol = ATT_WIDTH // HEAD_W
    keys = seq + ctx_len
    nch = keys // KEY_CHUNK
    assert keys % KEY_CHUNK == 0 and seq % (2 * rows) == 0
    gcol = jnp.broadcast_to(subln_g.reshape(HEAD_W, 1), (HEAD_W, rows))
    est = (2 * 2 * keys * rows * 4 + 2 * seq * HEAD_W * 2 + 2 * keys * HEAD_W * 2
           + 2 * (4 * seq + 2 * ctx_len) * HEAD_W * 2 + 64 * KEY_CHUNK * rows * 4)
    return pl.pallas_call(
        functools.partial(_attn_kernel, rows=rows),
        grid=(batch, ATT_HEADS),
        in_specs=[pl.BlockSpec((seq, HEAD_W), lambda b, h: (b, h)),
                  pl.BlockSpec((seq, HEAD_W), lambda b, h: (b, kcol + h)),
                  pl.BlockSpec((seq, HEAD_W), lambda b, h: (b, 2 * kcol + h)),
                  pl.BlockSpec((ctx_len, HEAD_W), lambda b, h: (b, h)),
                  pl.BlockSpec((ctx_len, HEAD_W), lambda b, h: (b, kcol + h)),
                  pl.BlockSpec((seq, HEAD_W), lambda b, h: (b, 3 * kcol + h)),
                  pl.BlockSpec((4, ATT_DIM), lambda b, h: (0, 0)),
                  pl.BlockSpec((HEAD_W, rows), lambda b, h: (0, 0))],
        out_specs=pl.BlockSpec((seq, HEAD_W), lambda b, h: (b, h)),
        out_shape=jax.ShapeDtypeStruct((batch * seq, ATT_WIDTH), BF16),
        scratch_shapes=[pltpu.VMEM((2, seq // rows, HEAD_W, rows), BF16),
                        pltpu.VMEM((keys, HEAD_W), BF16),
                        pltpu.VMEM((HEAD_W + VT_PAD, keys), BF16),
                        pltpu.VMEM((2, 2, nch, KEY_CHUNK, rows), F32)],
        compiler_params=pltpu.CompilerParams(dimension_semantics=("arbitrary",) * 2,
                                             vmem_limit_bytes=_vmem_limit(est)),
        name="diff_attention",
    )(qkv, qkv, qkv, kvc, kvc, qkv, lam_vecs, gcol)


def _ssm_prep_kernel(lre_ref, lim_ref, lstep_ref, bre_ref, bim_ref, cre_ref, cim_ref,
                     a_ref, bblk_ref, cblk_ref):
    lre = lre_ref[0]
    lim = lim_ref[0]
    dt = jnp.exp(lstep_ref[0])
    mag = jnp.exp(lre * dt)
    ar = mag * jnp.cos(lim * dt)
    ai = mag * jnp.sin(lim * dt)
    den = lre * lre + lim * lim
    fr = ((ar - 1.0) * lre + ai * lim) / den
    fi = (ai * lre - (ar - 1.0) * lim) / den
    a_ref[0, 0:1, :] = ar
    a_ref[0, 1:2, :] = ai
    br = bre_ref[0]
    bi = bim_ref[0]
    bbr = fr * br - fi * bi
    bbi = fr * bi + fi * br
    row_g = lax.broadcasted_iota(jnp.int32, (LANES, BLOCK_MODES), 0) // SSM_GROUP
    col_g = lax.broadcasted_iota(jnp.int32, (LANES, BLOCK_MODES), 1) // SSM_STATE
    keep_b = row_g == col_g
    row_g2 = lax.broadcasted_iota(jnp.int32, (BLOCK_MODES, LANES), 0) // SSM_STATE
    col_g2 = lax.broadcasted_iota(jnp.int32, (BLOCK_MODES, LANES), 1) // SSM_GROUP
    keep_c = row_g2 == col_g2
    for k in range(SSM_BLOCKS):
        sl = slice(k * BLOCK_MODES, (k + 1) * BLOCK_MODES)
        bblk_ref[0, k, :, 0:BLOCK_MODES] = jnp.where(keep_b, bbr[:, sl], 0.0).astype(BF16)
        bblk_ref[0, k, :, BLOCK_MODES:BLOCK_STATE] = jnp.where(keep_b, bbi[:, sl], 0.0).astype(BF16)
        cblk_ref[0, k, 0:BLOCK_MODES, :] = jnp.where(keep_c, cre_ref[0, sl, :], 0.0).astype(BF16)
        cblk_ref[0, k, BLOCK_MODES:BLOCK_STATE, :] = jnp.where(keep_c, -cim_ref[0, sl, :], 0.0).astype(BF16)


def _ssm_prep(lre, lim, lstep, bre_t, bim_t, cre_t, cim_t):
    nd = lre.shape[0]
    vec = pl.BlockSpec((1, 1, SSM_MODES), lambda d: (d, 0, 0))
    bsp = pl.BlockSpec((1, LANES, SSM_MODES), lambda d: (d, 0, 0))
    csp = pl.BlockSpec((1, SSM_MODES, LANES), lambda d: (d, 0, 0))
    est = 2 * (4 * LANES * SSM_MODES * 4 + 2 * SSM_BLOCKS * LANES * BLOCK_STATE * 2) + 8 * LANES * SSM_MODES * 4
    return pl.pallas_call(
        _ssm_prep_kernel,
        grid=(nd,),
        in_specs=[vec, vec, vec, bsp, bsp, csp, csp],
        out_specs=[pl.BlockSpec((1, 2, SSM_MODES), lambda d: (d, 0, 0)),
                   pl.BlockSpec((1, SSM_BLOCKS, LANES, BLOCK_STATE), lambda d: (d, 0, 0, 0)),
                   pl.BlockSpec((1, SSM_BLOCKS, BLOCK_STATE, LANES), lambda d: (d, 0, 0, 0))],
        out_shape=[jax.ShapeDtypeStruct((nd, 2, SSM_MODES), F32),
                   jax.ShapeDtypeStruct((nd, SSM_BLOCKS, LANES, BLOCK_STATE), BF16),
                   jax.ShapeDtypeStruct((nd, SSM_BLOCKS, BLOCK_STATE, LANES), BF16)],
        compiler_params=pltpu.CompilerParams(dimension_semantics=("arbitrary",),
                                             vmem_limit_bytes=_vmem_limit(est)),
        name="ssm_prep",
    )(lre, lim, lstep, bre_t, bim_t, cre_t, cim_t)


def _ssm_scan_kernel(*refs, reverse, batch, ctx_chunks):
    if reverse:
        uc_ref, ul_ref, a_ref, bblk_ref, cblk_ref, yf_ref, d_ref, y_ref, s_ref, h_ref = refs
    else:
        uc_ref, ul_ref, a_ref, bblk_ref, cblk_ref, y_ref, s_ref, h_ref = refs
    steps = ul_ref.shape[1]
    rows = steps * batch
    step = pl.program_id(0)

    @pl.when(step == 0)
    def _():
        h_ref[...] = jnp.zeros_like(h_ref)

    u_bt = jnp.where(step < ctx_chunks, uc_ref[...], ul_ref[...]).astype(F32)
    u32 = jnp.swapaxes(u_bt, 0, 1).reshape(rows, SSM_WIDTH)
    u = u32.astype(BF16)
    for k in range(SSM_BLOCKS):
        s_ref[:, k * BLOCK_STATE:(k + 1) * BLOCK_STATE] = jnp.dot(
            u[:, k * LANES:(k + 1) * LANES], bblk_ref[0, k], preferred_element_type=F32)

    for k in range(SSM_BLOCKS):
        c_re = k * BLOCK_STATE
        c_im = c_re + BLOCK_MODES
        ar = jnp.broadcast_to(a_ref[0, 0:1, k * BLOCK_MODES:(k + 1) * BLOCK_MODES], (batch, BLOCK_MODES))
        ai = jnp.broadcast_to(a_ref[0, 1:2, k * BLOCK_MODES:(k + 1) * BLOCK_MODES], (batch, BLOCK_MODES))
        hr = h_ref[:, c_re:c_re + BLOCK_MODES]
        hi = h_ref[:, c_im:c_im + BLOCK_MODES]
        for i in range(steps):
            r0 = ((steps - 1 - i) if reverse else i) * batch
            nhr = ar * hr - ai * hi + s_ref[r0:r0 + batch, c_re:c_re + BLOCK_MODES]
            nhi = ar * hi + ai * hr + s_ref[r0:r0 + batch, c_im:c_im + BLOCK_MODES]
            s_ref[r0:r0 + batch, c_re:c_re + BLOCK_MODES] = nhr
            s_ref[r0:r0 + batch, c_im:c_im + BLOCK_MODES] = nhi
            hr, hi = nhr, nhi
        h_ref[:, c_re:c_re + BLOCK_MODES] = hr
        h_ref[:, c_im:c_im + BLOCK_MODES] = hi

    ys = []
    for k in range(SSM_BLOCKS):
        yk = jnp.dot(s_ref[:, k * BLOCK_STATE:(k + 1) * BLOCK_STATE].astype(BF16), cblk_ref[0, k],
                     preferred_element_type=F32)
        cols = slice(k * LANES, (k + 1) * LANES)
        if reverse:
            yk = yk + yf_ref[:, cols] + d_ref[:, cols] * u32[:, cols]
        ys.append(yk)
    y = jnp.concatenate(ys, axis=1)
    if reverse:
        y_ref[...] = jnp.swapaxes(y.reshape(steps, batch, SSM_WIDTH), 0, 1)
    else:
        y_ref[...] = y


def _ssm_scan(u_ctx, u_lat, u_ctx_col, u_lat_col, a, bblk, cblk, direction, yf=None, dvec=None):
    batch = u_lat.shape[0]
    rows = SSM_CHUNK * batch
    cc = u_ctx.shape[1] // SSM_CHUNK
    lc = u_lat.shape[1] // SSM_CHUNK
    reverse = direction == 1
    if reverse:
        cidx = lambda s: jnp.maximum(cc - 1 - s, 0)
        lidx = lambda s: jnp.minimum(lc - 1, lc - 1 + cc - s)
    else:
        cidx = lambda s: jnp.minimum(s, cc - 1)
        lidx = lambda s: jnp.maximum(s - cc, 0)
    in_specs = [pl.BlockSpec((batch, SSM_CHUNK, SSM_WIDTH), lambda s: (0, cidx(s), u_ctx_col)),
                pl.BlockSpec((batch, SSM_CHUNK, SSM_WIDTH), lambda s: (0, lidx(s), u_lat_col)),
                pl.BlockSpec((1, 2, SSM_MODES), lambda s: (direction, 0, 0)),
                pl.BlockSpec((1, SSM_BLOCKS, LANES, BLOCK_STATE), lambda s: (direction, 0, 0, 0)),
                pl.BlockSpec((1, SSM_BLOCKS, BLOCK_STATE, LANES), lambda s: (direction, 0, 0, 0))]
    args = [u_ctx, u_lat, a, bblk, cblk]
    if reverse:
        in_specs += [pl.BlockSpec((rows, SSM_WIDTH), lambda s: (lidx(s), 0)),
                     pl.BlockSpec((1, SSM_WIDTH), lambda s: (0, 0))]
        args += [yf, dvec]
        out_spec = pl.BlockSpec((batch, SSM_CHUNK, SSM_WIDTH), lambda s: (0, lidx(s), 0))
        out_shape = jax.ShapeDtypeStruct((batch, lc * SSM_CHUNK, SSM_WIDTH), F32)
    else:
        out_spec = pl.BlockSpec((rows, SSM_WIDTH), lambda s: (lidx(s), 0))
        out_shape = jax.ShapeDtypeStruct((lc * rows, SSM_WIDTH), F32)
    state_w = SSM_BLOCKS * BLOCK_STATE
    est = (rows * state_w * 4 + 2 * rows * state_w // SSM_BLOCKS * 4 + 10 * rows * SSM_WIDTH * 4
           + 4 * SSM_BLOCKS * LANES * BLOCK_STATE * 2)
    return pl.pallas_call(
        functools.partial(_ssm_scan_kernel, reverse=reverse, batch=batch, ctx_chunks=cc),
        grid=(cc + lc,),
        in_specs=in_specs,
        out_specs=out_spec,
        out_shape=out_shape,
        scratch_shapes=[pltpu.VMEM((rows, state_w), F32), pltpu.VMEM((batch, state_w), F32)],
        compiler_params=pltpu.CompilerParams(dimension_semantics=("arbitrary",),
                                             vmem_limit_bytes=_vmem_limit(est)),
        name="ssm_scan_bwd" if reverse else "ssm_scan_fwd",
    )(*args)


def _gelu_tanh(v):
    return 0.5 * v * (1.0 + jnp.tanh(math.sqrt(2.0 / math.pi) * (v + 0.044715 * v * v * v)))


def _merge_kernel(a_ref, y_ref, gs_ref, gma0_ref, gma1_ref, gms0_ref, gms1_ref, x_ref, gate_ref,
                  wglu_ref, bglu_ref, wpa_ref, wps_ref, wout_ref, fg_ref, o_ref):
    yg = _gelu_tanh(y_ref[...])
    z = yg * _sigmoid(jnp.dot(yg.astype(BF16), wglu_ref[...], preferred_element_type=F32) + bglu_ref[...])
    s_br = (z * _silu(gs_ref[...].astype(F32))).astype(BF16)
    ta = jnp.dot(a_ref[...], wpa_ref[...], preferred_element_type=F32)
    ts = jnp.dot(s_br, wps_ref[...], preferred_element_type=F32)
    half = D_MODEL // 2
    parts = []
    for p, (ga_r, gs_r) in enumerate(((gma0_ref, gms0_ref), (gma1_ref, gms1_ref))):
        cols = slice(p * half, (p + 1) * half)
        parts.append((_sigmoid(ga_r[...].astype(F32)) * ta[:, cols]
                      + _sigmoid(gs_r[...].astype(F32)) * ts[:, cols]).astype(BF16))
    t = jnp.concatenate(parts, axis=1)
    out = jnp.dot(t, wout_ref[...], preferred_element_type=F32)
    xn = x_ref[...] + gate_ref[0] * out
    ms = jnp.mean(xn * xn, axis=-1, keepdims=True)
    o_ref[...] = xn * lax.rsqrt(ms + NORM_EPS) * fg_ref[...]


def _merge_out(a_br, y_bm, qkv, x2d, gate, w_glu, b_glu, w_pa, w_ps, w_out, final_g, seq, tm=256):
    m, d = x2d.shape
    half = d // 2
    per_batch = seq // tm
    gs_col = (4 * ATT_WIDTH + SSM_WIDTH) // SSM_WIDTH
    gm_col = (4 * ATT_WIDTH + 2 * SSM_WIDTH) // half
    const = lambda i: (0, 0)
    est = (2 * (tm * ATT_WIDTH * 2 + tm * SSM_WIDTH * 6 + 4 * tm * half * 2 + 2 * tm * d * 4)
           + 2 * (SSM_WIDTH * SSM_WIDTH + ATT_WIDTH * d + SSM_WIDTH * d + d * d) * 2 + 6 * tm * d * 4)
    return pl.pallas_call(
        _merge_kernel,
        grid=(m // tm,),
        in_specs=[pl.BlockSpec((tm, ATT_WIDTH), lambda i: (i, 0)),
                  pl.BlockSpec((tm, SSM_WIDTH), lambda i: (i, 0)),
                  pl.BlockSpec((tm, SSM_WIDTH), lambda i: (i, gs_col)),
                  pl.BlockSpec((tm, half), lambda i: (i, gm_col)),
                  pl.BlockSpec((tm, half), lambda i: (i, gm_col + 1)),
                  pl.BlockSpec((tm, half), lambda i: (i, gm_col + 2)),
                  pl.BlockSpec((tm, half), lambda i: (i, gm_col + 3)),
                  pl.BlockSpec((tm, d), lambda i: (i, 0)),
                  pl.BlockSpec((1, 1, d), lambda i: (i // per_batch, 0, 0)),
                  pl.BlockSpec((SSM_WIDTH, SSM_WIDTH), const),
                  pl.BlockSpec((1, SSM_WIDTH), const),
                  pl.BlockSpec((ATT_WIDTH, d), const),
                  pl.BlockSpec((SSM_WIDTH, d), const),
                  pl.BlockSpec((d, d), const),
                  pl.BlockSpec((1, d), const)],
        out_specs=pl.BlockSpec((tm, d), lambda i: (i, 0)),
        out_shape=jax.ShapeDtypeStruct((m, d), F32),
        compiler_params=pltpu.CompilerParams(dimension_semantics=("arbitrary",),
                                             vmem_limit_bytes=_vmem_limit(est)),
        name="merge_out",
    )(a_br, y_bm, qkv, qkv, qkv, qkv, qkv, x2d, gate, w_glu, b_glu, w_pa, w_ps, w_out, final_g)


def kernel(x, c, ctx, c_ctx, w_ada, b_ada, norm_g, w_in, lambda_q1, lambda_k1, lambda_q2, lambda_k2,
           subln_g, ssm_lambda_re, ssm_lambda_im, ssm_log_step, ssm_b_re, ssm_b_im, ssm_c_re, ssm_c_im,
           ssm_d, w_glu, b_glu, w_pa, w_ps, w_out, final_g):
    batch, seq, d = x.shape
    ctx_len = ctx.shape[1]
    assert w_ada.shape[0] == 1, "single-layer trunk"
    assert (d, seq % 1024, ctx_len % SSM_CHUNK, seq % SSM_CHUNK) == (D_MODEL, 0, 0, 0)

    pad = (-(batch + 1)) % SUBLANES
    cc = jnp.concatenate([c, c_ctx[None, :], jnp.zeros((pad, d), F32)], axis=0)
    mod = _adaln(cc, w_ada[0], b_ada[0][None, :])
    shift, scale, gate = (mod[:, k * d:(k + 1) * d] for k in range(3))
    as_mod = lambda v, lo, hi: v[lo:hi].reshape(hi - lo, 1, d)

    w_in_bf = w_in[0].astype(BF16)
    x2d = x.reshape(batch * seq, d)
    ctx2d = ctx.reshape(batch * ctx_len, d)
    ng = norm_g[0][None, :]

    tn = 1024
    u_col = 4 * ATT_WIDTH
    qkv = _in_proj(x2d, as_mod(scale, 0, batch), as_mod(shift, 0, batch), ng, w_in_bf,
                   list(range(IN_WIDTH // tn)), tn, seq, rope_tables=_rope_tables(seq))
    tnc = SSM_WIDTH
    k0 = ATT_WIDTH // tnc
    ctx_cols = list(range(k0, 3 * k0)) + [u_col // tnc]
    kvc = _in_proj(ctx2d, as_mod(scale, batch, batch + 1), as_mod(shift, batch, batch + 1), ng, w_in_bf,
                   ctx_cols, tnc, ctx_len)

    lam_vecs = jnp.stack([lambda_q1[0], lambda_k1[0], lambda_q2[0], lambda_k2[0]]).astype(F32)
    a_br = _attention(qkv, kvc, lam_vecs, subln_g[0].astype(F32), batch, seq, ctx_len)

    nd = ssm_lambda_re.shape[1]
    modes = lambda v: v[0].reshape(nd, 1, SSM_MODES)
    lstep = jnp.repeat(ssm_log_step[0], SSM_STATE, axis=-1).reshape(nd, 1, SSM_MODES)
    b_t = lambda v: jnp.tile(v[0].reshape(nd, SSM_MODES, SSM_GROUP).transpose(0, 2, 1), (1, GROUPS_PER_BLOCK, 1))
    c_t = lambda v: jnp.tile(v[0].transpose(0, 1, 3, 2).reshape(nd, SSM_MODES, SSM_GROUP), (1, 1, GROUPS_PER_BLOCK))
    a_disc, bblk, cblk = _ssm_prep(modes(ssm_lambda_re), modes(ssm_lambda_im), lstep,
                                   b_t(ssm_b_re), b_t(ssm_b_im), c_t(ssm_c_re), c_t(ssm_c_im))

    qkv3 = qkv.reshape(batch, seq, IN_WIDTH)
    kvc3 = kvc.reshape(batch, ctx_len, len(ctx_cols) * tnc)
    scan = functools.partial(_ssm_scan, kvc3, qkv3, len(ctx_cols) - 1, u_col // SSM_WIDTH, a_disc, bblk, cblk)
    yf = scan(0)
    y_bm = scan(1, yf=yf, dvec=ssm_d[0].reshape(1, SSM_WIDTH).astype(F32))

    out = _merge_out(a_br, y_bm.reshape(batch * seq, SSM_WIDTH), qkv, x2d, as_mod(gate, 0, batch),
                     w_glu[0].astype(BF16), b_glu[0][None, :], w_pa[0].astype(BF16), w_ps[0].astype(BF16),
                     w_out[0].astype(BF16), final_g[None, :], seq)
    return out.reshape(batch, seq, d)
```

```python
import functools
import math

import jax
import jax.numpy as jnp
import numpy as np
from jax import lax
from jax.experimental import pallas as pl
from jax.experimental.pallas import tpu as pltpu

F32 = jnp.float32
BF16 = jnp.bfloat16

D_MODEL = 2048
GRID_W = 64
ATT_HEADS = 8
ATT_DIM = 64
HEAD_W = 2 * ATT_DIM
ATT_WIDTH = ATT_HEADS * HEAD_W
SSM_GROUP = 16
SSM_GROUPS = 32
SSM_WIDTH = SSM_GROUP * SSM_GROUPS
SSM_STATE = 64
SSM_MODES = SSM_GROUPS * SSM_STATE
IN_WIDTH = 4 * ATT_WIDTH + 2 * SSM_WIDTH + 2 * D_MODEL
ROPE_BASE = 10000.0
NORM_EPS = 1e-6
LAM_INIT = 0.8 - 0.6 * math.exp(0.0)

LANES = 128
SUBLANES = 8
VMEM_BYTES_V7X = 64 * 1024 * 1024
MIB = 1024 * 1024

GROUPS_PER_BLOCK = LANES // SSM_GROUP
SSM_BLOCKS = SSM_GROUPS // GROUPS_PER_BLOCK
BLOCK_MODES = GROUPS_PER_BLOCK * SSM_STATE
BLOCK_STATE = 2 * BLOCK_MODES
SSM_CHUNK = 128
KEY_CHUNK = 256
VT_PAD = 16


def _vmem_limit(nbytes):
    return int(min(nbytes + nbytes // 4, VMEM_BYTES_V7X - 8 * MIB))


def _sigmoid(v):
    return 1.0 / (1.0 + jnp.exp(-v))


def _silu(v):
    return v * _sigmoid(v)


def _adaln_kernel(c_ref, w_ref, b_ref, o_ref):
    cv = c_ref[...]
    o_ref[...] = jnp.dot(_silu(cv), w_ref[...], preferred_element_type=F32,
                         precision=lax.Precision.HIGHEST) + b_ref[...]


def _adaln(cc, w_ada, b_ada):
    rows, d = cc.shape
    n = w_ada.shape[1]
    tn = 768
    est = 2 * (d * tn * 4) + 2 * rows * tn * 4 + rows * d * 4 * 2
    return pl.pallas_call(
        _adaln_kernel,
        grid=(n // tn,),
        in_specs=[pl.BlockSpec((rows, d), lambda j: (0, 0)),
                  pl.BlockSpec((d, tn), lambda j: (0, j)),
                  pl.BlockSpec((1, tn), lambda j: (0, j))],
        out_specs=pl.BlockSpec((rows, tn), lambda j: (0, j)),
        out_shape=jax.ShapeDtypeStruct((rows, n), F32),
        compiler_params=pltpu.CompilerParams(dimension_semantics=("arbitrary",),
                                             vmem_limit_bytes=_vmem_limit(est)),
        name="adaln",
    )(cc, w_ada, b_ada)


def _in_proj_kernel(*refs, rope, norm_rows):
    if rope:
        x_ref, scale_ref, shift_ref, g_ref, w_ref, cos_ref, sina_ref, sinb_ref, o_ref, h_ref, acc_ref = refs
    else:
        x_ref, scale_ref, shift_ref, g_ref, w_ref, o_ref, h_ref = refs
    j = pl.program_id(1)
    tm = x_ref.shape[0]

    @pl.when(j == 0)
    def _():
        gmod = g_ref[...] * (1.0 + scale_ref[0])
        shift = shift_ref[0]

        def body(r, carry):
            r0 = pl.multiple_of(r * norm_rows, norm_rows)
            xf = x_ref[pl.ds(r0, norm_rows), :]
            ms = jnp.mean(xf * xf, axis=-1, keepdims=True)
            y = xf * lax.rsqrt(ms + NORM_EPS) * gmod + shift
            h_ref[pl.ds(r0, norm_rows), :] = y.astype(BF16)
            return carry

        lax.fori_loop(0, tm // norm_rows, body, 0)

    acc = jnp.dot(h_ref[...], w_ref[...], preferred_element_type=F32)
    o_ref[...] = acc.astype(o_ref.dtype)

    if rope:
        acc_ref[...] = acc

        @pl.when(j < 2)
        def _():
            qscale = jnp.where(j == 0, ATT_DIM ** -0.5 * math.log2(math.e), 1.0).astype(F32)
            cos = (cos_ref[...] * qscale).astype(BF16)
            sina = (sina_ref[...] * qscale).astype(BF16)
            sinb = (sinb_ref[...] * qscale).astype(BF16)
            for cb in range(acc_ref.shape[1] // HEAD_W):
                v = acc_ref[:, cb * HEAD_W:(cb + 1) * HEAD_W].astype(BF16)
                up = pltpu.roll(v, HEAD_W - ATT_DIM // 4, 1)
                dn = pltpu.roll(v, ATT_DIM // 4, 1)
                o_ref[:, cb * HEAD_W:(cb + 1) * HEAD_W] = v * cos + up * sina + dn * sinb


def _in_proj(x2d, scale, shift, norm_g, w_bf, col_blocks, tn, seq_len, rope_tables=None, tm=1024):
    m, d = x2d.shape
    nj = len(col_blocks)
    rope = rope_tables is not None
    seq_blocks = max(seq_len // tm, 1)
    per_batch_mod = scale.shape[0] > 1

    def mod_idx(i, j):
        return ((i // seq_blocks) if per_batch_mod else 0, 0, 0)

    contiguous = all(cb == col_blocks[0] + k for k, cb in enumerate(col_blocks))
    if contiguous:
        w_map = lambda i, j: (0, j + col_blocks[0])
    else:
        cbs = col_blocks
        def w_map(i, j):
            idx = jnp.int32(cbs[-1])
            for k in range(nj - 2, -1, -1):
                idx = jnp.where(j == k, jnp.int32(cbs[k]), idx)
            return (0, idx)

    in_specs = [pl.BlockSpec((tm, d), lambda i, j: (i, 0)),
                pl.BlockSpec((1, 1, d), mod_idx),
                pl.BlockSpec((1, 1, d), mod_idx),
                pl.BlockSpec((1, d), lambda i, j: (0, 0)),
                pl.BlockSpec((d, tn), w_map)]
    args = [x2d, scale, shift, norm_g, w_bf]
    if rope:
        for t in rope_tables:
            in_specs.append(pl.BlockSpec((tm, HEAD_W), lambda i, j: (i % seq_blocks, 0)))
            args.append(t)
    est = (2 * tm * d * 4 + tm * d * 2 + 2 * d * tn * 2 + 2 * tm * tn * 2 + 2 * tm * tn * 4
           + (6 * tm * HEAD_W * 4 + tm * tn * 4 if rope else 0))
    return pl.pallas_call(
        functools.partial(_in_proj_kernel, rope=rope, norm_rows=256),
        grid=(m // tm, nj),
        in_specs=in_specs,
        out_specs=pl.BlockSpec((tm, tn), lambda i, j: (i, j)),
        out_shape=jax.ShapeDtypeStruct((m, nj * tn), BF16),
        scratch_shapes=[pltpu.VMEM((tm, d), BF16)] + ([pltpu.VMEM((tm, tn), F32)] if rope else []),
        compiler_params=pltpu.CompilerParams(dimension_semantics=("arbitrary", "arbitrary"),
                                             vmem_limit_bytes=_vmem_limit(est)),
        name="in_proj_rope" if rope else "in_proj_ctx",
    )(*args)


def _rope_tables(seq):
    rows = seq // GRID_W
    row = np.repeat(np.arange(rows), GRID_W).astype(np.float64)
    col = np.tile(np.arange(GRID_W), rows).astype(np.float64)
    nf = ATT_DIM // 4
    inv = ROPE_BASE ** (-np.arange(nf, dtype=np.float64) / nf)

    def axis_tables(pos):
        ang = pos[:, None] * inv[None, :]
        c, s = np.cos(ang), np.sin(ang)
        z = np.zeros_like(s)
        return (np.concatenate([c, c], -1), np.concatenate([-s, z], -1), np.concatenate([z, s], -1))

    parts = [axis_tables(row), axis_tables(col)]
    out = []
    for k in range(3):
        comp = np.concatenate([parts[0][k], parts[1][k]], -1)
        out.append(jnp.asarray(np.concatenate([comp, comp], -1), dtype=F32))
    return out


def _attn_kernel(q_ref, k_ref, v_ref, kc_ref, vc_ref, ga_ref, lam_ref, gcol_ref, o_ref,
                 qt_scr, kall_scr, vt_scr, s_scr, *, rows):
    seq = k_ref.shape[0]
    keys = seq + kc_ref.shape[0]
    nblk = q_ref.shape[0] // rows
    nch = keys // KEY_CHUNK
    sub = KEY_CHUNK // SUBLANES
    lamv = lam_ref[...]
    lam = (jnp.exp(jnp.sum(lamv[0:1] * lamv[1:2], axis=-1, keepdims=True))
           - jnp.exp(jnp.sum(lamv[2:3] * lamv[3:4], axis=-1, keepdims=True)) + LAM_INIT)

    kall_scr[0:seq, :] = k_ref[...]
    kall_scr[seq:keys, :] = kc_ref[...]
    vt_scr[0:HEAD_W, 0:seq] = v_ref[...].T
    vt_scr[0:HEAD_W, seq:keys] = vc_ref[...].T
    pad_row = lax.broadcasted_iota(jnp.int32, (VT_PAD, keys), 0)
    vt_scr[HEAD_W:HEAD_W + VT_PAD, :] = jnp.where(pad_row == 0, 1.0, 0.0).astype(BF16)
    drow = lax.broadcasted_iota(jnp.int32, (HEAD_W, rows), 0)
    for i in range(nblk):
        qt = q_ref[i * rows:(i + 1) * rows, :].T
        qt_scr[0, i] = jnp.where(drow < ATT_DIM, qt, jnp.zeros_like(qt))
        qt_scr[1, i] = jnp.where(drow >= ATT_DIM, qt, jnp.zeros_like(qt))

    def colmax(x):
        return jnp.max(x.reshape(sub, SUBLANES, rows), axis=0)

    def step(t, slot, do_qk, do_sm, mfin):
        mpart = [None, None]
        acc = [None, None]
        for c in range(nch):
            ks = slice(c * KEY_CHUNK, (c + 1) * KEY_CHUNK)
            if do_qk:
                kc = kall_scr[ks, :]
                for comp in range(2):
                    s = jnp.dot(kc, qt_scr[comp, t], preferred_element_type=F32)
                    s_scr[slot, comp, c] = s
                    cm = colmax(s)
                    mpart[comp] = cm if mpart[comp] is None else jnp.maximum(mpart[comp], cm)
            if do_sm:
                vt = vt_scr[:, ks]
                for comp in range(2):
                    p = jnp.exp2(s_scr[1 - slot, comp, c] - mfin[comp]).astype(BF16)
                    d = jnp.dot(vt, p, preferred_element_type=F32)
                    acc[comp] = d if acc[comp] is None else acc[comp] + d
        if do_sm:
            r0 = pl.multiple_of((t - 1) * rows, rows)
            l1 = acc[0][HEAD_W:HEAD_W + 1]
            l2 = acc[1][HEAD_W:HEAD_W + 1]
            o = acc[0][0:HEAD_W] * (1.0 / l1) - acc[1][0:HEAD_W] * (lam / l2)
            ms = jnp.mean(o * o, axis=0, keepdims=True)
            on = (o * lax.rsqrt(ms + NORM_EPS) * gcol_ref[...] * (1.0 - LAM_INIT)).T
            o_ref[pl.ds(r0, rows), :] = (on * _silu(ga_ref[pl.ds(r0, rows), :].astype(F32))).astype(o_ref.dtype)
        return [jnp.max(m, axis=0, keepdims=True) for m in mpart] if do_qk else mfin

    m = step(0, 0, True, False, None)

    def body(tt, carry):
        t = 2 * tt + 1
        mm = step(t, 1, True, True, list(carry))
        mm = step(t + 1, 0, True, True, mm)
        return mm[0], mm[1]

    m = lax.fori_loop(0, (nblk - 2) // 2, body, (m[0], m[1]))
    m = step(nblk - 1, 1, True, True, list(m))
    step(nblk, 0, False, True, m)


def _attention(qkv, kvc, lam_vecs, subln_g, batch, seq, ctx_len, rows=256):
    kcol = ATT_WIDTH // HEAD_W
    keys = seq + ctx_len
    nch = keys // KEY_CHUNK
    assert keys % KEY_CHUNK == 0 and seq % (2 * rows) == 0
    gcol = jnp.broadcast_to(subln_g.reshape(HEAD_W, 1), (HEAD_W, rows))
    est = (2 * 2 * keys * rows * 4 + 2 * seq * HEAD_W * 2 + 2 * keys * HEAD_W * 2
           + 2 * (4 * seq + 2 * ctx_len) * HEAD_W * 2 + 64 * KEY_CHUNK * rows * 4)
    return pl.pallas_call(
        functools.partial(_attn_kernel, rows=rows),
        grid=(batch, ATT_HEADS),
        in_specs=[pl.BlockSpec((seq, HEAD_W), lambda b, h: (b, h)),
                  pl.BlockSpec((seq, HEAD_W), lambda b, h: (b, kcol + h)),
                  pl.BlockSpec((seq, HEAD_W), lambda b, h: (b, 2 * kcol + h)),
                  pl.BlockSpec((ctx_len, HEAD_W), lambda b, h: (b, h)),
                  pl.BlockSpec((ctx_len, HEAD_W), lambda b, h: (b, kcol + h)),
                  pl.BlockSpec((seq, HEAD_W), lambda b, h: (b, 3 * kcol + h)),
                  pl.BlockSpec((4, ATT_DIM), lambda b, h: (0, 0)),
                  pl.BlockSpec((HEAD_W, rows), lambda b, h: (0, 0))],
        out_specs=pl.BlockSpec((seq, HEAD_W), lambda b, h: (b, h)),
        out_shape=jax.ShapeDtypeStruct((batch * seq, ATT_WIDTH), BF16),
        scratch_shapes=[pltpu.VMEM((2, seq // rows, HEAD_W, rows), BF16),
                        pltpu.VMEM((keys, HEAD_W), BF16),
                        pltpu.VMEM((HEAD_W + VT_PAD, keys), BF16),
                        pltpu.VMEM((2, 2, nch, KEY_CHUNK, rows), F32)],
        compiler_params=pltpu.CompilerParams(dimension_semantics=("arbitrary",) * 2,
                                             vmem_limit_bytes=_vmem_limit(est)),
        name="diff_attention",
    )(qkv, qkv, qkv, kvc, kvc, qkv, lam_vecs, gcol)


def _ssm_prep_kernel(lre_ref, lim_ref, lstep_ref, bre_ref, bim_ref, cre_ref, cim_ref,
                     a_ref, bblk_ref, cblk_ref):
    lre = lre_ref[0]
    lim = lim_ref[0]
    dt = jnp.exp(lstep_ref[0])
    mag = jnp.exp(lre * dt)
    ar = mag * jnp.cos(lim * dt)
    ai = mag * jnp.sin(lim * dt)
    den = lre * lre + lim * lim
    fr = ((ar - 1.0) * lre + ai * lim) / den
    fi = (ai * lre - (ar - 1.0) * lim) / den
    a_ref[0, 0:1, :] = ar
    a_ref[0, 1:2, :] = ai
    br = bre_ref[0]
    bi = bim_ref[0]
    bbr = fr * br - fi * bi
    bbi = fr * bi + fi * br
    row_g = lax.broadcasted_iota(jnp.int32, (LANES, BLOCK_MODES), 0) // SSM_GROUP
    col_g = lax.broadcasted_iota(jnp.int32, (LANES, BLOCK_MODES), 1) // SSM_STATE
    keep_b = row_g == col_g
    row_g2 = lax.broadcasted_iota(jnp.int32, (BLOCK_MODES, LANES), 0) // SSM_STATE
    col_g2 = lax.broadcasted_iota(jnp.int32, (BLOCK_MODES, LANES), 1) // SSM_GROUP
    keep_c = row_g2 == col_g2
    for k in range(SSM_BLOCKS):
        sl = slice(k * BLOCK_MODES, (k + 1) * BLOCK_MODES)
        bblk_ref[0, k, :, 0:BLOCK_MODES] = jnp.where(keep_b, bbr[:, sl], 0.0).astype(BF16)
        bblk_ref[0, k, :, BLOCK_MODES:BLOCK_STATE] = jnp.where(keep_b, bbi[:, sl], 0.0).astype(BF16)
        cblk_ref[0, k, 0:BLOCK_MODES, :] = jnp.where(keep_c, cre_ref[0, sl, :], 0.0).astype(BF16)
        cblk_ref[0, k, BLOCK_MODES:BLOCK_STATE, :] = jnp.where(keep_c, -cim_ref[0, sl, :], 0.0).astype(BF16)


def _ssm_prep(lre, lim, lstep, bre_t, bim_t, cre_t, cim_t):
    nd = lre.shape[0]
    vec = pl.BlockSpec((1, 1, SSM_MODES), lambda d: (d, 0, 0))
    bsp = pl.BlockSpec((1, LANES, SSM_MODES), lambda d: (d, 0, 0))
    csp = pl.BlockSpec((1, SSM_MODES, LANES), lambda d: (d, 0, 0))
    est = 2 * (4 * LANES * SSM_MODES * 4 + 2 * SSM_BLOCKS * LANES * BLOCK_STATE * 2) + 8 * LANES * SSM_MODES * 4
    return pl.pallas_call(
        _ssm_prep_kernel,
        grid=(nd,),
        in_specs=[vec, vec, vec, bsp, bsp, csp, csp],
        out_specs=[pl.BlockSpec((1, 2, SSM_MODES), lambda d: (d, 0, 0)),
                   pl.BlockSpec((1, SSM_BLOCKS, LANES, BLOCK_STATE), lambda d: (d, 0, 0, 0)),
                   pl.BlockSpec((1, SSM_BLOCKS, BLOCK_STATE, LANES), lambda d: (d, 0, 0, 0))],
        out_shape=[jax.ShapeDtypeStruct((nd, 2, SSM_MODES), F32),
                   jax.ShapeDtypeStruct((nd, SSM_BLOCKS, LANES, BLOCK_STATE), BF16),
                   jax.ShapeDtypeStruct((nd, SSM_BLOCKS, BLOCK_STATE, LANES), BF16)],
        compiler_params=pltpu.CompilerParams(dimension_semantics=("arbitrary",),
                                             vmem_limit_bytes=_vmem_limit(est)),
        name="ssm_prep",
    )(lre, lim, lstep, bre_t, bim_t, cre_t, cim_t)


def _ssm_scan_kernel(*refs, reverse, batch, ctx_chunks):
    if reverse:
        uc_ref, ul_ref, a_ref, bblk_ref, cblk_ref, yf_ref, d_ref, y_ref, s_ref, hb_ref, h_ref = refs
    else:
        uc_ref, ul_ref, a_ref, bblk_ref, cblk_ref, y_ref, s_ref, hb_ref, h_ref = refs
    steps = ul_ref.shape[1]
    rows = steps * batch
    step = pl.program_id(0)

    @pl.when(step == 0)
    def _():
        h_ref[...] = jnp.zeros_like(h_ref)

    u_bt = jnp.where(step < ctx_chunks, uc_ref[...], ul_ref[...]).astype(F32)
    u32 = jnp.swapaxes(u_bt, 0, 1).reshape(rows, SSM_WIDTH)
    u = u32.astype(BF16)
    for k in range(SSM_BLOCKS):
        s_ref[:, k * BLOCK_STATE:(k + 1) * BLOCK_STATE] = jnp.dot(
            u[:, k * LANES:(k + 1) * LANES], bblk_ref[0, k], preferred_element_type=F32)

    for k in range(SSM_BLOCKS):
        c_re = k * BLOCK_STATE
        c_im = c_re + BLOCK_MODES
        ar = jnp.broadcast_to(a_ref[0, 0:1, k * BLOCK_MODES:(k + 1) * BLOCK_MODES], (batch, BLOCK_MODES))
        ai = jnp.broadcast_to(a_ref[0, 1:2, k * BLOCK_MODES:(k + 1) * BLOCK_MODES], (batch, BLOCK_MODES))
        hr = h_ref[:, c_re:c_re + BLOCK_MODES]
        hi = h_ref[:, c_im:c_im + BLOCK_MODES]
        for i in range(0, steps, 2):
            pair = []
            for t in ((steps - 1 - i, steps - 2 - i) if reverse else (i, i + 1)):
                r0 = t * batch
                nhr = ar * hr - ai * hi + s_ref[r0:r0 + batch, c_re:c_re + BLOCK_MODES]
                nhi = ar * hi + ai * hr + s_ref[r0:r0 + batch, c_im:c_im + BLOCK_MODES]
                pair.append((nhr, nhi))
                hr, hi = nhr, nhi
            lo, hi2 = (pair[1], pair[0]) if reverse else (pair[0], pair[1])
            p0 = min(steps - 2 - i, steps - 1 - i) * batch if reverse else i * batch
            hb_ref[p0:p0 + 2 * batch, c_re:c_re + BLOCK_MODES] = jnp.concatenate([lo[0], hi2[0]], axis=0).astype(BF16)
            hb_ref[p0:p0 + 2 * batch, c_im:c_im + BLOCK_MODES] = jnp.concatenate([lo[1], hi2[1]], axis=0).astype(BF16)
        h_ref[:, c_re:c_re + BLOCK_MODES] = hr
        h_ref[:, c_im:c_im + BLOCK_MODES] = hi

    ys = []
    for k in range(SSM_BLOCKS):
        yk = jnp.dot(hb_ref[:, k * BLOCK_STATE:(k + 1) * BLOCK_STATE], cblk_ref[0, k],
                     preferred_element_type=F32)
        cols = slice(k * LANES, (k + 1) * LANES)
        if reverse:
            yk = yk + yf_ref[:, cols] + d_ref[:, cols] * u32[:, cols]
        ys.append(yk)
    y = jnp.concatenate(ys, axis=1)
    if reverse:
        y_ref[...] = jnp.swapaxes(y.reshape(steps, batch, SSM_WIDTH), 0, 1)
    else:
        y_ref[...] = y


def _ssm_scan(u_ctx, u_lat, u_ctx_col, u_lat_col, a, bblk, cblk, direction, yf=None, dvec=None):
    batch = u_lat.shape[0]
    rows = SSM_CHUNK * batch
    cc = u_ctx.shape[1] // SSM_CHUNK
    lc = u_lat.shape[1] // SSM_CHUNK
    reverse = direction == 1
    if reverse:
        cidx = lambda s: jnp.maximum(cc - 1 - s, 0)
        lidx = lambda s: jnp.minimum(lc - 1, lc - 1 + cc - s)
    else:
        cidx = lambda s: jnp.minimum(s, cc - 1)
        lidx = lambda s: jnp.maximum(s - cc, 0)
    in_specs = [pl.BlockSpec((batch, SSM_CHUNK, SSM_WIDTH), lambda s: (0, cidx(s), u_ctx_col)),
                pl.BlockSpec((batch, SSM_CHUNK, SSM_WIDTH), lambda s: (0, lidx(s), u_lat_col)),
                pl.BlockSpec((1, 2, SSM_MODES), lambda s: (direction, 0, 0)),
                pl.BlockSpec((1, SSM_BLOCKS, LANES, BLOCK_STATE), lambda s: (direction, 0, 0, 0)),
                pl.BlockSpec((1, SSM_BLOCKS, BLOCK_STATE, LANES), lambda s: (direction, 0, 0, 0))]
    args = [u_ctx, u_lat, a, bblk, cblk]
    if reverse:
        in_specs += [pl.BlockSpec((rows, SSM_WIDTH), lambda s: (lidx(s), 0)),
                     pl.BlockSpec((1, SSM_WIDTH), lambda s: (0, 0))]
        args += [yf, dvec]
        out_spec = pl.BlockSpec((batch, SSM_CHUNK, SSM_WIDTH), lambda s: (0, lidx(s), 0))
        out_shape = jax.ShapeDtypeStruct((batch, lc * SSM_CHUNK, SSM_WIDTH), F32)
    else:
        out_spec = pl.BlockSpec((rows, SSM_WIDTH), lambda s: (lidx(s), 0))
        out_shape = jax.ShapeDtypeStruct((lc * rows, SSM_WIDTH), F32)
    state_w = SSM_BLOCKS * BLOCK_STATE
    est = (rows * state_w * 6 + 2 * rows * state_w // SSM_BLOCKS * 4 + 10 * rows * SSM_WIDTH * 4
           + 4 * SSM_BLOCKS * LANES * BLOCK_STATE * 2)
    return pl.pallas_call(
        functools.partial(_ssm_scan_kernel, reverse=reverse, batch=batch, ctx_chunks=cc),
        grid=(cc + lc,),
        in_specs=in_specs,
        out_specs=out_spec,
        out_shape=out_shape,
        scratch_shapes=[pltpu.VMEM((rows, state_w), F32), pltpu.VMEM((rows, state_w), BF16),
                        pltpu.VMEM((batch, state_w), F32)],
        compiler_params=pltpu.CompilerParams(dimension_semantics=("arbitrary",),
                                             vmem_limit_bytes=_vmem_limit(est)),
        name="ssm_scan_bwd" if reverse else "ssm_scan_fwd",
    )(*args)


def _gelu_tanh(v):
    return 0.5 * v * (1.0 + jnp.tanh(math.sqrt(2.0 / math.pi) * (v + 0.044715 * v * v * v)))


def _merge_kernel(a_ref, y_ref, gs_ref, gma0_ref, gma1_ref, gms0_ref, gms1_ref, x_ref, gate_ref,
                  wglu_ref, bglu_ref, wpa_ref, wps_ref, wout_ref, fg_ref, o_ref):
    yg = _gelu_tanh(y_ref[...])
    z = yg * _sigmoid(jnp.dot(yg.astype(BF16), wglu_ref[...], preferred_element_type=F32) + bglu_ref[...])
    s_br = (z * _silu(gs_ref[...].astype(F32))).astype(BF16)
    ta = jnp.dot(a_ref[...], wpa_ref[...], preferred_element_type=F32)
    ts = jnp.dot(s_br, wps_ref[...], preferred_element_type=F32)
    half = D_MODEL // 2
    parts = []
    for p, (ga_r, gs_r) in enumerate(((gma0_ref, gms0_ref), (gma1_ref, gms1_ref))):
        cols = slice(p * half, (p + 1) * half)
        parts.append((_sigmoid(ga_r[...].astype(F32)) * ta[:, cols]
                      + _sigmoid(gs_r[...].astype(F32)) * ts[:, cols]).astype(BF16))
    t = jnp.concatenate(parts, axis=1)
    out = jnp.dot(t, wout_ref[...], preferred_element_type=F32)
    xn = x_ref[...] + gate_ref[0] * out
    ms = jnp.mean(xn * xn, axis=-1, keepdims=True)
    o_ref[...] = xn * lax.rsqrt(ms + NORM_EPS) * fg_ref[...]


def _merge_out(a_br, y_bm, qkv, x2d, gate, w_glu, b_glu, w_pa, w_ps, w_out, final_g, seq, tm=256):
    m, d = x2d.shape
    half = d // 2
    per_batch = seq // tm
    gs_col = (4 * ATT_WIDTH + SSM_WIDTH) // SSM_WIDTH
    gm_col = (4 * ATT_WIDTH + 2 * SSM_WIDTH) // half
    const = lambda i: (0, 0)
    est = (2 * (tm * ATT_WIDTH * 2 + tm * SSM_WIDTH * 6 + 4 * tm * half * 2 + 2 * tm * d * 4)
           + 2 * (SSM_WIDTH * SSM_WIDTH + ATT_WIDTH * d + SSM_WIDTH * d + d * d) * 2 + 6 * tm * d * 4)
    return pl.pallas_call(
        _merge_kernel,
        grid=(m // tm,),
        in_specs=[pl.BlockSpec((tm, ATT_WIDTH), lambda i: (i, 0)),
                  pl.BlockSpec((tm, SSM_WIDTH), lambda i: (i, 0)),
                  pl.BlockSpec((tm, SSM_WIDTH), lambda i: (i, gs_col)),
                  pl.BlockSpec((tm, half), lambda i: (i, gm_col)),
                  pl.BlockSpec((tm, half), lambda i: (i, gm_col + 1)),
                  pl.BlockSpec((tm, half), lambda i: (i, gm_col + 2)),
                  pl.BlockSpec((tm, half), lambda i: (i, gm_col + 3)),
                  pl.BlockSpec((tm, d), lambda i: (i, 0)),
                  pl.BlockSpec((1, 1, d), lambda i: (i // per_batch, 0, 0)),
                  pl.BlockSpec((SSM_WIDTH, SSM_WIDTH), const),
                  pl.BlockSpec((1, SSM_WIDTH), const),
                  pl.BlockSpec((ATT_WIDTH, d), const),
                  pl.BlockSpec((SSM_WIDTH, d), const),
                  pl.BlockSpec((d, d), const),
                  pl.BlockSpec((1, d), const)],
        out_specs=pl.BlockSpec((tm, d), lambda i: (i, 0)),
        out_shape=jax.ShapeDtypeStruct((m, d), F32),
        compiler_params=pltpu.CompilerParams(dimension_semantics=("arbitrary",),
                                             vmem_limit_bytes=_vmem_limit(est)),
        name="merge_out",
    )(a_br, y_bm, qkv, qkv, qkv, qkv, qkv, x2d, gate, w_glu, b_glu, w_pa, w_ps, w_out, final_g)


def kernel(x, c, ctx, c_ctx, w_ada, b_ada, norm_g, w_in, lambda_q1, lambda_k1, lambda_q2, lambda_k2,
           subln_g, ssm_lambda_re, ssm_lambda_im, ssm_log_step, ssm_b_re, ssm_b_im, ssm_c_re, ssm_c_im,
           ssm_d, w_glu, b_glu, w_pa, w_ps, w_out, final_g):
    batch, seq, d = x.shape
    ctx_len = ctx.shape[1]
    assert w_ada.shape[0] == 1, "single-layer trunk"
    assert (d, seq % 1024, ctx_len % SSM_CHUNK, seq % SSM_CHUNK) == (D_MODEL, 0, 0, 0)

    pad = (-(batch + 1)) % SUBLANES
    cc = jnp.concatenate([c, c_ctx[None, :], jnp.zeros((pad, d), F32)], axis=0)
    mod = _adaln(cc, w_ada[0], b_ada[0][None, :])
    shift, scale, gate = (mod[:, k * d:(k + 1) * d] for k in range(3))
    as_mod = lambda v, lo, hi: v[lo:hi].reshape(hi - lo, 1, d)

    w_in_bf = w_in[0].astype(BF16)
    x2d = x.reshape(batch * seq, d)
    ctx2d = ctx.reshape(batch * ctx_len, d)
    ng = norm_g[0][None, :]

    tn = 1024
    u_col = 4 * ATT_WIDTH
    qkv = _in_proj(x2d, as_mod(scale, 0, batch), as_mod(shift, 0, batch), ng, w_in_bf,
                   list(range(IN_WIDTH // tn)), tn, seq, rope_tables=_rope_tables(seq))
    tnc = SSM_WIDTH
    k0 = ATT_WIDTH // tnc
    ctx_cols = list(range(k0, 3 * k0)) + [u_col // tnc]
    kvc = _in_proj(ctx2d, as_mod(scale, batch, batch + 1), as_mod(shift, batch, batch + 1), ng, w_in_bf,
                   ctx_cols, tnc, ctx_len)

    lam_vecs = jnp.stack([lambda_q1[0], lambda_k1[0], lambda_q2[0], lambda_k2[0]]).astype(F32)
    a_br = _attention(qkv, kvc, lam_vecs, subln_g[0].astype(F32), batch, seq, ctx_len)

    nd = ssm_lambda_re.shape[1]
    modes = lambda v: v[0].reshape(nd, 1, SSM_MODES)
    lstep = jnp.repeat(ssm_log_step[0], SSM_STATE, axis=-1).reshape(nd, 1, SSM_MODES)
    b_t = lambda v: jnp.tile(v[0].reshape(nd, SSM_MODES, SSM_GROUP).transpose(0, 2, 1), (1, GROUPS_PER_BLOCK, 1))
    c_t = lambda v: jnp.tile(v[0].transpose(0, 1, 3, 2).reshape(nd, SSM_MODES, SSM_GROUP), (1, 1, GROUPS_PER_BLOCK))
    a_disc, bblk, cblk = _ssm_prep(modes(ssm_lambda_re), modes(ssm_lambda_im), lstep,
                                   b_t(ssm_b_re), b_t(ssm_b_im), c_t(ssm_c_re), c_t(ssm_c_im))

    qkv3 = qkv.reshape(batch, seq, IN_WIDTH)
    kvc3 = kvc.reshape(batch, ctx_len, len(ctx_cols) * tnc)
    scan = functools.partial(_ssm_scan, kvc3, qkv3, len(ctx_cols) - 1, u_col // SSM_WIDTH, a_disc, bblk, cblk)
    yf = scan(0)
    y_bm = scan(1, yf=yf, dvec=ssm_d[0].reshape(1, SSM_WIDTH).astype(F32))

    out = _merge_out(a_br, y_bm.reshape(batch * seq, SSM_WIDTH), qkv, x2d, as_mod(gate, 0, batch),
                     w_glu[0].astype(BF16), b_glu[0][None, :], w_pa[0].astype(BF16), w_ps[0].astype(BF16),
                     w_out[0].astype(BF16), final_g[None, :], seq)
    return out.reshape(batch, seq, d)
```

```python
import functools
import math

import jax
import jax.numpy as jnp
import numpy as np
from jax import lax
from jax.experimental import pallas as pl
from jax.experimental.pallas import tpu as pltpu

F32 = jnp.float32
BF16 = jnp.bfloat16

D_MODEL = 2048
GRID_W = 64
ATT_HEADS = 8
ATT_DIM = 64
HEAD_W = 2 * ATT_DIM
ATT_WIDTH = ATT_HEADS * HEAD_W
SSM_GROUP = 16
SSM_GROUPS = 32
SSM_WIDTH = SSM_GROUP * SSM_GROUPS
SSM_STATE = 64
SSM_MODES = SSM_GROUPS * SSM_STATE
IN_WIDTH = 4 * ATT_WIDTH + 2 * SSM_WIDTH + 2 * D_MODEL
ROPE_BASE = 10000.0
NORM_EPS = 1e-6
LAM_INIT = 0.8 - 0.6 * math.exp(0.0)

LANES = 128
SUBLANES = 8
VMEM_BYTES_V7X = 64 * 1024 * 1024
MIB = 1024 * 1024

GROUPS_PER_BLOCK = LANES // SSM_GROUP
SSM_BLOCKS = SSM_GROUPS // GROUPS_PER_BLOCK
BLOCK_MODES = GROUPS_PER_BLOCK * SSM_STATE
BLOCK_STATE = 2 * BLOCK_MODES
SSM_CHUNK = 128
KEY_CHUNK = 256
VT_PAD = 16


def _vmem_limit(nbytes):
    return int(min(nbytes + nbytes // 4, VMEM_BYTES_V7X - 8 * MIB))


def _sigmoid(v):
    return 1.0 / (1.0 + jnp.exp(-v))


def _silu(v):
    return v * _sigmoid(v)


def _adaln_kernel(c_ref, w_ref, b_ref, o_ref):
    o_ref[...] = jnp.dot(_silu(c_ref[...]).astype(BF16), w_ref[...].astype(BF16),
                         preferred_element_type=F32) + b_ref[...]


def _adaln(cc, w_ada, b_ada):
    rows, d = cc.shape
    n = w_ada.shape[1]
    tn = 768
    est = 2 * (d * tn * 4) + 2 * rows * tn * 4 + rows * d * 4 * 2
    return pl.pallas_call(
        _adaln_kernel,
        grid=(n // tn,),
        in_specs=[pl.BlockSpec((rows, d), lambda j: (0, 0)),
                  pl.BlockSpec((d, tn), lambda j: (0, j)),
                  pl.BlockSpec((1, tn), lambda j: (0, j))],
        out_specs=pl.BlockSpec((rows, tn), lambda j: (0, j)),
        out_shape=jax.ShapeDtypeStruct((rows, n), F32),
        compiler_params=pltpu.CompilerParams(dimension_semantics=("arbitrary",),
                                             vmem_limit_bytes=_vmem_limit(est)),
        name="adaln",
    )(cc, w_ada, b_ada)


def _in_proj_kernel(*refs, rope, norm_rows):
    if rope:
        x_ref, scale_ref, shift_ref, g_ref, w_ref, cos_ref, sina_ref, sinb_ref, o_ref, h_ref, acc_ref = refs
    else:
        x_ref, scale_ref, shift_ref, g_ref, w_ref, o_ref, h_ref = refs
    j = pl.program_id(1)
    tm = x_ref.shape[0]

    @pl.when(j == 0)
    def _():
        gmod = g_ref[...] * (1.0 + scale_ref[0])
        shift = shift_ref[0]

        def body(r, carry):
            r0 = pl.multiple_of(r * norm_rows, norm_rows)
            xf = x_ref[pl.ds(r0, norm_rows), :]
            ms = jnp.mean(xf * xf, axis=-1, keepdims=True)
            y = xf * lax.rsqrt(ms + NORM_EPS) * gmod + shift
            h_ref[pl.ds(r0, norm_rows), :] = y.astype(BF16)
            return carry

        lax.fori_loop(0, tm // norm_rows, body, 0)

    acc = jnp.dot(h_ref[...], w_ref[...], preferred_element_type=F32)
    o_ref[...] = acc.astype(o_ref.dtype)

    if rope:
        acc_ref[...] = acc

        @pl.when(j < 2)
        def _():
            qscale = jnp.where(j == 0, ATT_DIM ** -0.5 * math.log2(math.e), 1.0).astype(F32)
            cos = (cos_ref[...] * qscale).astype(BF16)
            sina = (sina_ref[...] * qscale).astype(BF16)
            sinb = (sinb_ref[...] * qscale).astype(BF16)
            for cb in range(acc_ref.shape[1] // HEAD_W):
                v = acc_ref[:, cb * HEAD_W:(cb + 1) * HEAD_W].astype(BF16)
                up = pltpu.roll(v, HEAD_W - ATT_DIM // 4, 1)
                dn = pltpu.roll(v, ATT_DIM // 4, 1)
                o_ref[:, cb * HEAD_W:(cb + 1) * HEAD_W] = v * cos + up * sina + dn * sinb


def _in_proj(x2d, scale, shift, norm_g, w_bf, col_blocks, tn, seq_len, rope_tables=None, tm=1024):
    m, d = x2d.shape
    nj = len(col_blocks)
    rope = rope_tables is not None
    seq_blocks = max(seq_len // tm, 1)
    per_batch_mod = scale.shape[0] > 1

    def mod_idx(i, j):
        return ((i // seq_blocks) if per_batch_mod else 0, 0, 0)

    contiguous = all(cb == col_blocks[0] + k for k, cb in enumerate(col_blocks))
    if contiguous:
        w_map = lambda i, j: (0, j + col_blocks[0])
    else:
        cbs = col_blocks
        def w_map(i, j):
            idx = jnp.int32(cbs[-1])
            for k in range(nj - 2, -1, -1):
                idx = jnp.where(j == k, jnp.int32(cbs[k]), idx)
            return (0, idx)

    in_specs = [pl.BlockSpec((tm, d), lambda i, j: (i, 0)),
                pl.BlockSpec((1, 1, d), mod_idx),
                pl.BlockSpec((1, 1, d), mod_idx),
                pl.BlockSpec((1, d), lambda i, j: (0, 0)),
                pl.BlockSpec((d, tn), w_map)]
    args = [x2d, scale, shift, norm_g, w_bf]
    if rope:
        for t in rope_tables:
            in_specs.append(pl.BlockSpec((tm, HEAD_W), lambda i, j: (i % seq_blocks, 0)))
            args.append(t)
    est = (2 * tm * d * 4 + tm * d * 2 + 2 * d * tn * 2 + 2 * tm * tn * 2 + 2 * tm * tn * 4
           + (6 * tm * HEAD_W * 4 + tm * tn * 4 if rope else 0))
    return pl.pallas_call(
        functools.partial(_in_proj_kernel, rope=rope, norm_rows=256),
        grid=(m // tm, nj),
        in_specs=in_specs,
        out_specs=pl.BlockSpec((tm, tn), lambda i, j: (i, j)),
        out_shape=jax.ShapeDtypeStruct((m, nj * tn), BF16),
        scratch_shapes=[pltpu.VMEM((tm, d), BF16)] + ([pltpu.VMEM((tm, tn), F32)] if rope else []),
        compiler_params=pltpu.CompilerParams(dimension_semantics=("arbitrary", "arbitrary"),
                                             vmem_limit_bytes=_vmem_limit(est)),
        name="in_proj_rope" if rope else "in_proj_ctx",
    )(*args)


def _rope_tables(seq):
    rows = seq // GRID_W
    row = np.repeat(np.arange(rows), GRID_W).astype(np.float64)
    col = np.tile(np.arange(GRID_W), rows).astype(np.float64)
    nf = ATT_DIM // 4
    inv = ROPE_BASE ** (-np.arange(nf, dtype=np.float64) / nf)

    def axis_tables(pos):
        ang = pos[:, None] * inv[None, :]
        c, s = np.cos(ang), np.sin(ang)
        z = np.zeros_like(s)
        return (np.concatenate([c, c], -1), np.concatenate([-s, z], -1), np.concatenate([z, s], -1))

    parts = [axis_tables(row), axis_tables(col)]
    out = []
    for k in range(3):
        comp = np.concatenate([parts[0][k], parts[1][k]], -1)
        out.append(jnp.asarray(np.concatenate([comp, comp], -1), dtype=F32))
    return out


def _attn_kernel(q_ref, k_ref, v_ref, kc_ref, vc_ref, ga_ref, lam_ref, gcol_ref, o_ref,
                 qt_scr, kall_scr, vt_scr, s_scr, *, rows):
    seq = k_ref.shape[0]
    keys = seq + kc_ref.shape[0]
    nblk = q_ref.shape[0] // rows
    nch = keys // KEY_CHUNK
    sub = KEY_CHUNK // SUBLANES
    lamv = lam_ref[...]
    lam = (jnp.exp(jnp.sum(lamv[0:1] * lamv[1:2], axis=-1, keepdims=True))
           - jnp.exp(jnp.sum(lamv[2:3] * lamv[3:4], axis=-1, keepdims=True)) + LAM_INIT)

    kall_scr[0:seq, :] = k_ref[...]
    kall_scr[seq:keys, :] = kc_ref[...]
    vt_scr[0:HEAD_W, 0:seq] = v_ref[...].T
    vt_scr[0:HEAD_W, seq:keys] = vc_ref[...].T
    pad_row = lax.broadcasted_iota(jnp.int32, (VT_PAD, keys), 0)
    vt_scr[HEAD_W:HEAD_W + VT_PAD, :] = jnp.where(pad_row == 0, 1.0, 0.0).astype(BF16)
    drow = lax.broadcasted_iota(jnp.int32, (HEAD_W, rows), 0)
    for i in range(nblk):
        qt = q_ref[i * rows:(i + 1) * rows, :].T
        qt_scr[0, i] = jnp.where(drow < ATT_DIM, qt, jnp.zeros_like(qt))
        qt_scr[1, i] = jnp.where(drow >= ATT_DIM, qt, jnp.zeros_like(qt))

    def colmax(x):
        return jnp.max(x.reshape(sub, SUBLANES, rows), axis=0)

    def step(t, slot, do_qk, do_sm, mfin):
        mpart = [None, None]
        acc = [None, None]
        for c in range(nch):
            ks = slice(c * KEY_CHUNK, (c + 1) * KEY_CHUNK)
            if do_qk:
                kc = kall_scr[ks, :]
                for comp in range(2):
                    s = jnp.dot(kc, qt_scr[comp, t], preferred_element_type=F32)
                    s_scr[slot, comp, c] = s
                    cm = colmax(s)
                    mpart[comp] = cm if mpart[comp] is None else jnp.maximum(mpart[comp], cm)
            if do_sm:
                vt = vt_scr[:, ks]
                for comp in range(2):
                    p = jnp.exp2(s_scr[1 - slot, comp, c] - mfin[comp]).astype(BF16)
                    d = jnp.dot(vt, p, preferred_element_type=F32)
                    acc[comp] = d if acc[comp] is None else acc[comp] + d
        if do_sm:
            r0 = pl.multiple_of((t - 1) * rows, rows)
            l1 = acc[0][HEAD_W:HEAD_W + 1]
            l2 = acc[1][HEAD_W:HEAD_W + 1]
            o = acc[0][0:HEAD_W] * (1.0 / l1) - acc[1][0:HEAD_W] * (lam / l2)
            ms = jnp.mean(o * o, axis=0, keepdims=True)
            on = (o * lax.rsqrt(ms + NORM_EPS) * gcol_ref[...] * (1.0 - LAM_INIT)).T
            o_ref[pl.ds(r0, rows), :] = (on * _silu(ga_ref[pl.ds(r0, rows), :].astype(F32))).astype(o_ref.dtype)
        return [jnp.max(m, axis=0, keepdims=True) for m in mpart] if do_qk else mfin

    m = step(0, 0, True, False, None)

    def body(tt, carry):
        t = 2 * tt + 1
        mm = step(t, 1, True, True, list(carry))
        mm = step(t + 1, 0, True, True, mm)
        return mm[0], mm[1]

    m = lax.fori_loop(0, (nblk - 2) // 2, body, (m[0], m[1]))
    m = step(nblk - 1, 1, True, True, list(m))
    step(nblk, 0, False, True, m)


def _attention(qkv, kvc, lam_vecs, subln_g, batch, seq, ctx_len, rows=256):
    kcol = ATT_WIDTH // HEAD_W
    keys = seq + ctx_len
    nch = keys // KEY_CHUNK
    assert keys % KEY_CHUNK == 0 and seq % (2 * rows) == 0
    gcol = jnp.broadcast_to(subln_g.reshape(HEAD_W, 1), (HEAD_W, rows))
    est = (2 * 2 * keys * rows * 4 + 2 * seq * HEAD_W * 2 + 2 * keys * HEAD_W * 2
           + 2 * (4 * seq + 2 * ctx_len) * HEAD_W * 2 + 64 * KEY_CHUNK * rows * 4)
    return pl.pallas_call(
        functools.partial(_attn_kernel, rows=rows),
        grid=(batch, ATT_HEADS),
        in_specs=[pl.BlockSpec((seq, HEAD_W), lambda b, h: (b, h)),
                  pl.BlockSpec((seq, HEAD_W), lambda b, h: (b, kcol + h)),
                  pl.BlockSpec((seq, HEAD_W), lambda b, h: (b, 2 * kcol + h)),
                  pl.BlockSpec((ctx_len, HEAD_W), lambda b, h: (b, h)),
                  pl.BlockSpec((ctx_len, HEAD_W), lambda b, h: (b, kcol + h)),
                  pl.BlockSpec((seq, HEAD_W), lambda b, h: (b, 3 * kcol + h)),
                  pl.BlockSpec((4, ATT_DIM), lambda b, h: (0, 0)),
                  pl.BlockSpec((HEAD_W, rows), lambda b, h: (0, 0))],
        out_specs=pl.BlockSpec((seq, HEAD_W), lambda b, h: (b, h)),
        out_shape=jax.ShapeDtypeStruct((batch * seq, ATT_WIDTH), BF16),
        scratch_shapes=[pltpu.VMEM((2, seq // rows, HEAD_W, rows), BF16),
                        pltpu.VMEM((keys, HEAD_W), BF16),
                        pltpu.VMEM((HEAD_W + VT_PAD, keys), BF16),
                        pltpu.VMEM((2, 2, nch, KEY_CHUNK, rows), F32)],
        compiler_params=pltpu.CompilerParams(dimension_semantics=("arbitrary",) * 2,
                                             vmem_limit_bytes=_vmem_limit(est)),
        name="diff_attention",
    )(qkv, qkv, qkv, kvc, kvc, qkv, lam_vecs, gcol)


def _ssm_prep_kernel(lre_ref, lim_ref, lstep_ref, bre_ref, bim_ref, cre_ref, cim_ref,
                     a_ref, bblk_ref, cblk_ref):
    lre = lre_ref[0]
    lim = lim_ref[0]
    dt = jnp.exp(lstep_ref[0])
    mag = jnp.exp(lre * dt)
    ar = mag * jnp.cos(lim * dt)
    ai = mag * jnp.sin(lim * dt)
    den = lre * lre + lim * lim
    fr = ((ar - 1.0) * lre + ai * lim) / den
    fi = (ai * lre - (ar - 1.0) * lim) / den
    a_ref[0, 0:1, :] = ar
    a_ref[0, 1:2, :] = ai
    br = bre_ref[0]
    bi = bim_ref[0]
    bbr = fr * br - fi * bi
    bbi = fr * bi + fi * br
    row_g = lax.broadcasted_iota(jnp.int32, (LANES, BLOCK_MODES), 0) // SSM_GROUP
    col_g = lax.broadcasted_iota(jnp.int32, (LANES, BLOCK_MODES), 1) // SSM_STATE
    keep_b = row_g == col_g
    row_g2 = lax.broadcasted_iota(jnp.int32, (BLOCK_MODES, LANES), 0) // SSM_STATE
    col_g2 = lax.broadcasted_iota(jnp.int32, (BLOCK_MODES, LANES), 1) // SSM_GROUP
    keep_c = row_g2 == col_g2
    for k in range(SSM_BLOCKS):
        sl = slice(k * BLOCK_MODES, (k + 1) * BLOCK_MODES)
        bblk_ref[0, k, :, 0:BLOCK_MODES] = jnp.where(keep_b, bbr[:, sl], 0.0).astype(BF16)
        bblk_ref[0, k, :, BLOCK_MODES:BLOCK_STATE] = jnp.where(keep_b, bbi[:, sl], 0.0).astype(BF16)
        cblk_ref[0, k, 0:BLOCK_MODES, :] = jnp.where(keep_c, cre_ref[0, sl, :], 0.0).astype(BF16)
        cblk_ref[0, k, BLOCK_MODES:BLOCK_STATE, :] = jnp.where(keep_c, -cim_ref[0, sl, :], 0.0).astype(BF16)


def _ssm_prep(lre, lim, lstep, bre_t, bim_t, cre_t, cim_t):
    nd = lre.shape[0]
    vec = pl.BlockSpec((1, 1, SSM_MODES), lambda d: (d, 0, 0))
    bsp = pl.BlockSpec((1, LANES, SSM_MODES), lambda d: (d, 0, 0))
    csp = pl.BlockSpec((1, SSM_MODES, LANES), lambda d: (d, 0, 0))
    est = 2 * (4 * LANES * SSM_MODES * 4 + 2 * SSM_BLOCKS * LANES * BLOCK_STATE * 2) + 8 * LANES * SSM_MODES * 4
    return pl.pallas_call(
        _ssm_prep_kernel,
        grid=(nd,),
        in_specs=[vec, vec, vec, bsp, bsp, csp, csp],
        out_specs=[pl.BlockSpec((1, 2, SSM_MODES), lambda d: (d, 0, 0)),
                   pl.BlockSpec((1, SSM_BLOCKS, LANES, BLOCK_STATE), lambda d: (d, 0, 0, 0)),
                   pl.BlockSpec((1, SSM_BLOCKS, BLOCK_STATE, LANES), lambda d: (d, 0, 0, 0))],
        out_shape=[jax.ShapeDtypeStruct((nd, 2, SSM_MODES), F32),
                   jax.ShapeDtypeStruct((nd, SSM_BLOCKS, LANES, BLOCK_STATE), BF16),
                   jax.ShapeDtypeStruct((nd, SSM_BLOCKS, BLOCK_STATE, LANES), BF16)],
        compiler_params=pltpu.CompilerParams(dimension_semantics=("arbitrary",),
                                             vmem_limit_bytes=_vmem_limit(est)),
        name="ssm_prep",
    )(lre, lim, lstep, bre_t, bim_t, cre_t, cim_t)


def _ssm_scan_kernel(*refs, reverse, batch, ctx_chunks):
    if reverse:
        uc_ref, ul_ref, a_ref, bblk_ref, cblk_ref, yf_ref, d_ref, y_ref, s_ref, hb_ref, h_ref = refs
    else:
        uc_ref, ul_ref, a_ref, bblk_ref, cblk_ref, y_ref, s_ref, hb_ref, h_ref = refs
    steps = ul_ref.shape[1]
    rows = steps * batch
    step = pl.program_id(0)

    @pl.when(step == 0)
    def _():
        h_ref[...] = jnp.zeros_like(h_ref)

    u_bt = jnp.where(step < ctx_chunks, uc_ref[...], ul_ref[...]).astype(F32)
    u32 = jnp.swapaxes(u_bt, 0, 1).reshape(rows, SSM_WIDTH)
    u = u32.astype(BF16)
    for k in range(SSM_BLOCKS):
        s_ref[:, k * BLOCK_STATE:(k + 1) * BLOCK_STATE] = jnp.dot(
            u[:, k * LANES:(k + 1) * LANES], bblk_ref[0, k], preferred_element_type=F32)

    for k in range(SSM_BLOCKS):
        c_re = k * BLOCK_STATE
        c_im = c_re + BLOCK_MODES
        ar = jnp.broadcast_to(a_ref[0, 0:1, k * BLOCK_MODES:(k + 1) * BLOCK_MODES], (batch, BLOCK_MODES))
        ai = jnp.broadcast_to(a_ref[0, 1:2, k * BLOCK_MODES:(k + 1) * BLOCK_MODES], (batch, BLOCK_MODES))
        hr = h_ref[:, c_re:c_re + BLOCK_MODES]
        hi = h_ref[:, c_im:c_im + BLOCK_MODES]
        for i in range(0, steps, 2):
            pair = []
            for t in ((steps - 1 - i, steps - 2 - i) if reverse else (i, i + 1)):
                r0 = t * batch
                nhr = ar * hr - ai * hi + s_ref[r0:r0 + batch, c_re:c_re + BLOCK_MODES]
                nhi = ar * hi + ai * hr + s_ref[r0:r0 + batch, c_im:c_im + BLOCK_MODES]
                pair.append((nhr, nhi))
                hr, hi = nhr, nhi
            lo, hi2 = (pair[1], pair[0]) if reverse else (pair[0], pair[1])
            p0 = min(steps - 2 - i, steps - 1 - i) * batch if reverse else i * batch
            hb_ref[p0:p0 + 2 * batch, c_re:c_re + BLOCK_MODES] = jnp.concatenate([lo[0], hi2[0]], axis=0).astype(BF16)
            hb_ref[p0:p0 + 2 * batch, c_im:c_im + BLOCK_MODES] = jnp.concatenate([lo[1], hi2[1]], axis=0).astype(BF16)
        h_ref[:, c_re:c_re + BLOCK_MODES] = hr
        h_ref[:, c_im:c_im + BLOCK_MODES] = hi

    ys = []
    for k in range(SSM_BLOCKS):
        yk = jnp.dot(hb_ref[:, k * BLOCK_STATE:(k + 1) * BLOCK_STATE], cblk_ref[0, k],
                     preferred_element_type=F32)
        cols = slice(k * LANES, (k + 1) * LANES)
        if reverse:
            yk = yk + yf_ref[:, cols] + d_ref[:, cols] * u32[:, cols]
        ys.append(yk)
    y = jnp.concatenate(ys, axis=1)
    if reverse:
        y_ref[...] = jnp.swapaxes(y.reshape(steps, batch, SSM_WIDTH), 0, 1)
    else:
        y_ref[...] = y


def _ssm_scan(u_ctx, u_lat, u_ctx_col, u_lat_col, a, bblk, cblk, direction, yf=None, dvec=None):
    batch = u_lat.shape[0]
    rows = SSM_CHUNK * batch
    cc = u_ctx.shape[1] // SSM_CHUNK
    lc = u_lat.shape[1] // SSM_CHUNK
    reverse = direction == 1
    if reverse:
        cidx = lambda s: jnp.maximum(cc - 1 - s, 0)
        lidx = lambda s: jnp.minimum(lc - 1, lc - 1 + cc - s)
    else:
        cidx = lambda s: jnp.minimum(s, cc - 1)
        lidx = lambda s: jnp.maximum(s - cc, 0)
    in_specs = [pl.BlockSpec((batch, SSM_CHUNK, SSM_WIDTH), lambda s: (0, cidx(s), u_ctx_col)),
                pl.BlockSpec((batch, SSM_CHUNK, SSM_WIDTH), lambda s: (0, lidx(s), u_lat_col)),
                pl.BlockSpec((1, 2, SSM_MODES), lambda s: (direction, 0, 0)),
                pl.BlockSpec((1, SSM_BLOCKS, LANES, BLOCK_STATE), lambda s: (direction, 0, 0, 0)),
                pl.BlockSpec((1, SSM_BLOCKS, BLOCK_STATE, LANES), lambda s: (direction, 0, 0, 0))]
    args = [u_ctx, u_lat, a, bblk, cblk]
    if reverse:
        in_specs += [pl.BlockSpec((rows, SSM_WIDTH), lambda s: (lidx(s), 0)),
                     pl.BlockSpec((1, SSM_WIDTH), lambda s: (0, 0))]
        args += [yf, dvec]
        out_spec = pl.BlockSpec((batch, SSM_CHUNK, SSM_WIDTH), lambda s: (0, lidx(s), 0))
        out_shape = jax.ShapeDtypeStruct((batch, lc * SSM_CHUNK, SSM_WIDTH), F32)
    else:
        out_spec = pl.BlockSpec((rows, SSM_WIDTH), lambda s: (lidx(s), 0))
        out_shape = jax.ShapeDtypeStruct((lc * rows, SSM_WIDTH), F32)
    state_w = SSM_BLOCKS * BLOCK_STATE
    est = (rows * state_w * 6 + 2 * rows * state_w // SSM_BLOCKS * 4 + 10 * rows * SSM_WIDTH * 4
           + 4 * SSM_BLOCKS * LANES * BLOCK_STATE * 2)
    return pl.pallas_call(
        functools.partial(_ssm_scan_kernel, reverse=reverse, batch=batch, ctx_chunks=cc),
        grid=(cc + lc,),
        in_specs=in_specs,
        out_specs=out_spec,
        out_shape=out_shape,
        scratch_shapes=[pltpu.VMEM((rows, state_w), F32), pltpu.VMEM((rows, state_w), BF16),
                        pltpu.VMEM((batch, state_w), F32)],
        compiler_params=pltpu.CompilerParams(dimension_semantics=("arbitrary",),
                                             vmem_limit_bytes=_vmem_limit(est)),
        name="ssm_scan_bwd" if reverse else "ssm_scan_fwd",
    )(*args)


def _gelu_tanh(v):
    return 0.5 * v * (1.0 + jnp.tanh(math.sqrt(2.0 / math.pi) * (v + 0.044715 * v * v * v)))


def _merge_kernel(a_ref, y_ref, gs_ref, gma0_ref, gma1_ref, gms0_ref, gms1_ref, x_ref, gate_ref,
                  wglu_ref, bglu_ref, wpa_ref, wps_ref, wout_ref, fg_ref, o_ref):
    yg = _gelu_tanh(y_ref[...])
    z = yg * _sigmoid(jnp.dot(yg.astype(BF16), wglu_ref[...], preferred_element_type=F32) + bglu_ref[...])
    s_br = (z * _silu(gs_ref[...].astype(F32))).astype(BF16)
    ta = jnp.dot(a_ref[...], wpa_ref[...], preferred_element_type=F32)
    ts = jnp.dot(s_br, wps_ref[...], preferred_element_type=F32)
    half = D_MODEL // 2
    parts = []
    for p, (ga_r, gs_r) in enumerate(((gma0_ref, gms0_ref), (gma1_ref, gms1_ref))):
        cols = slice(p * half, (p + 1) * half)
        parts.append((_sigmoid(ga_r[...].astype(F32)) * ta[:, cols]
                      + _sigmoid(gs_r[...].astype(F32)) * ts[:, cols]).astype(BF16))
    t = jnp.concatenate(parts, axis=1)
    out = jnp.dot(t, wout_ref[...], preferred_element_type=F32)
    xn = x_ref[...] + gate_ref[0] * out
    ms = jnp.mean(xn * xn, axis=-1, keepdims=True)
    o_ref[...] = xn * lax.rsqrt(ms + NORM_EPS) * fg_ref[...]


def _merge_out(a_br, y_bm, qkv, x2d, gate, w_glu, b_glu, w_pa, w_ps, w_out, final_g, seq, tm=256):
    m, d = x2d.shape
    half = d // 2
    per_batch = seq // tm
    gs_col = (4 * ATT_WIDTH + SSM_WIDTH) // SSM_WIDTH
    gm_col = (4 * ATT_WIDTH + 2 * SSM_WIDTH) // half
    const = lambda i: (0, 0)
    est = (2 * (tm * ATT_WIDTH * 2 + tm * SSM_WIDTH * 6 + 4 * tm * half * 2 + 2 * tm * d * 4)
           + 2 * (SSM_WIDTH * SSM_WIDTH + ATT_WIDTH * d + SSM_WIDTH * d + d * d) * 2 + 6 * tm * d * 4)
    return pl.pallas_call(
        _merge_kernel,
        grid=(m // tm,),
        in_specs=[pl.BlockSpec((tm, ATT_WIDTH), lambda i: (i, 0)),
                  pl.BlockSpec((tm, SSM_WIDTH), lambda i: (i, 0)),
                  pl.BlockSpec((tm, SSM_WIDTH), lambda i: (i, gs_col)),
                  pl.BlockSpec((tm, half), lambda i: (i, gm_col)),
                  pl.BlockSpec((tm, half), lambda i: (i, gm_col + 1)),
                  pl.BlockSpec((tm, half), lambda i: (i, gm_col + 2)),
                  pl.BlockSpec((tm, half), lambda i: (i, gm_col + 3)),
                  pl.BlockSpec((tm, d), lambda i: (i, 0)),
                  pl.BlockSpec((1, 1, d), lambda i: (i // per_batch, 0, 0)),
                  pl.BlockSpec((SSM_WIDTH, SSM_WIDTH), const),
                  pl.BlockSpec((1, SSM_WIDTH), const),
                  pl.BlockSpec((ATT_WIDTH, d), const),
                  pl.BlockSpec((SSM_WIDTH, d), const),
                  pl.BlockSpec((d, d), const),
                  pl.BlockSpec((1, d), const)],
        out_specs=pl.BlockSpec((tm, d), lambda i: (i, 0)),
        out_shape=jax.ShapeDtypeStruct((m, d), F32),
        compiler_params=pltpu.CompilerParams(dimension_semantics=("arbitrary",),
                                             vmem_limit_bytes=_vmem_limit(est)),
        name="merge_out",
    )(a_br, y_bm, qkv, qkv, qkv, qkv, qkv, x2d, gate, w_glu, b_glu, w_pa, w_ps, w_out, final_g)


def kernel(x, c, ctx, c_ctx, w_ada, b_ada, norm_g, w_in, lambda_q1, lambda_k1, lambda_q2, lambda_k2,
           subln_g, ssm_lambda_re, ssm_lambda_im, ssm_log_step, ssm_b_re, ssm_b_im, ssm_c_re, ssm_c_im,
           ssm_d, w_glu, b_glu, w_pa, w_ps, w_out, final_g):
    batch, seq, d = x.shape
    ctx_len = ctx.shape[1]
    assert w_ada.shape[0] == 1, "single-layer trunk"
    assert (d, seq % 1024, ctx_len % SSM_CHUNK, seq % SSM_CHUNK) == (D_MODEL, 0, 0, 0)

    pad = (-(batch + 1)) % SUBLANES
    cc = jnp.concatenate([c, c_ctx[None, :], jnp.zeros((pad, d), F32)], axis=0)
    mod = _adaln(cc, w_ada[0], b_ada[0][None, :])
    shift, scale, gate = (mod[:, k * d:(k + 1) * d] for k in range(3))
    as_mod = lambda v, lo, hi: v[lo:hi].reshape(hi - lo, 1, d)

    w_in_bf = w_in[0].astype(BF16)
    x2d = x.reshape(batch * seq, d)
    ctx2d = ctx.reshape(batch * ctx_len, d)
    ng = norm_g[0][None, :]

    tn = 1024
    u_col = 4 * ATT_WIDTH
    qkv = _in_proj(x2d, as_mod(scale, 0, batch), as_mod(shift, 0, batch), ng, w_in_bf,
                   list(range(IN_WIDTH // tn)), tn, seq, rope_tables=_rope_tables(seq))
    tnc = SSM_WIDTH
    k0 = ATT_WIDTH // tnc
    ctx_cols = list(range(k0, 3 * k0)) + [u_col // tnc]
    kvc = _in_proj(ctx2d, as_mod(scale, batch, batch + 1), as_mod(shift, batch, batch + 1), ng, w_in_bf,
                   ctx_cols, tnc, ctx_len)

    lam_vecs = jnp.stack([lambda_q1[0], lambda_k1[0], lambda_q2[0], lambda_k2[0]]).astype(F32)
    a_br = _attention(qkv, kvc, lam_vecs, subln_g[0].astype(F32), batch, seq, ctx_len)

    nd = ssm_lambda_re.shape[1]
    modes = lambda v: v[0].reshape(nd, 1, SSM_MODES)
    lstep = jnp.repeat(ssm_log_step[0], SSM_STATE, axis=-1).reshape(nd, 1, SSM_MODES)
    b_t = lambda v: jnp.tile(v[0].reshape(nd, SSM_MODES, SSM_GROUP).transpose(0, 2, 1), (1, GROUPS_PER_BLOCK, 1))
    c_t = lambda v: jnp.tile(v[0].transpose(0, 1, 3, 2).reshape(nd, SSM_MODES, SSM_GROUP), (1, 1, GROUPS_PER_BLOCK))
    a_disc, bblk, cblk = _ssm_prep(modes(ssm_lambda_re), modes(ssm_lambda_im), lstep,
                                   b_t(ssm_b_re), b_t(ssm_b_im), c_t(ssm_c_re), c_t(ssm_c_im))

    qkv3 = qkv.reshape(batch, seq, IN_WIDTH)
    kvc3 = kvc.reshape(batch, ctx_len, len(ctx_cols) * tnc)
    scan = functools.partial(_ssm_scan, kvc3, qkv3, len(ctx_cols) - 1, u_col // SSM_WIDTH, a_disc, bblk, cblk)
    yf = scan(0)
    y_bm = scan(1, yf=yf, dvec=ssm_d[0].reshape(1, SSM_WIDTH).astype(F32))

    out = _merge_out(a_br, y_bm.reshape(batch * seq, SSM_WIDTH), qkv, x2d, as_mod(gate, 0, batch),
                     w_glu[0].astype(BF16), b_glu[0][None, :], w_pa[0].astype(BF16), w_ps[0].astype(BF16),
                     w_out[0].astype(BF16), final_g[None, :], seq)
    return out.reshape(batch, seq, d)
```

```python
import functools
import math

import jax
import jax.numpy as jnp
import numpy as np
from jax import lax
from jax.experimental import pallas as pl
from jax.experimental.pallas import tpu as pltpu

F32 = jnp.float32
BF16 = jnp.bfloat16

D_MODEL = 2048
GRID_W = 64
ATT_HEADS = 8
ATT_DIM = 64
HEAD_W = 2 * ATT_DIM
ATT_WIDTH = ATT_HEADS * HEAD_W
SSM_GROUP = 16
SSM_GROUPS = 32
SSM_WIDTH = SSM_GROUP * SSM_GROUPS
SSM_STATE = 64
SSM_MODES = SSM_GROUPS * SSM_STATE
IN_WIDTH = 4 * ATT_WIDTH + 2 * SSM_WIDTH + 2 * D_MODEL
ROPE_BASE = 10000.0
NORM_EPS = 1e-6
LAM_INIT = 0.8 - 0.6 * math.exp(0.0)

LANES = 128
SUBLANES = 8
VMEM_BYTES_V7X = 64 * 1024 * 1024
MIB = 1024 * 1024

GROUPS_PER_BLOCK = LANES // SSM_GROUP
SSM_BLOCKS = SSM_GROUPS // GROUPS_PER_BLOCK
BLOCK_MODES = GROUPS_PER_BLOCK * SSM_STATE
BLOCK_STATE = 2 * BLOCK_MODES
SSM_CHUNK = 128
KEY_CHUNK = 256
VT_PAD = 16


def _vmem_limit(nbytes):
    return int(min(nbytes + nbytes // 4, VMEM_BYTES_V7X - 8 * MIB))


def _sigmoid(v):
    return 1.0 / (1.0 + jnp.exp(-v))


def _silu(v):
    return v * _sigmoid(v)


def _adaln_kernel(c_ref, w_ref, b_ref, o_ref):
    o_ref[...] = jnp.dot(_silu(c_ref[...]).astype(BF16), w_ref[...].astype(BF16),
                         preferred_element_type=F32) + b_ref[...]


def _adaln(cc, w_ada, b_ada):
    rows, d = cc.shape
    n = w_ada.shape[1]
    tn = 768
    est = 2 * (d * tn * 4) + 2 * rows * tn * 4 + rows * d * 4 * 2
    return pl.pallas_call(
        _adaln_kernel,
        grid=(n // tn,),
        in_specs=[pl.BlockSpec((rows, d), lambda j: (0, 0)),
                  pl.BlockSpec((d, tn), lambda j: (0, j)),
                  pl.BlockSpec((1, tn), lambda j: (0, j))],
        out_specs=pl.BlockSpec((rows, tn), lambda j: (0, j)),
        out_shape=jax.ShapeDtypeStruct((rows, n), F32),
        compiler_params=pltpu.CompilerParams(dimension_semantics=("arbitrary",),
                                             vmem_limit_bytes=_vmem_limit(est)),
        name="adaln",
    )(cc, w_ada, b_ada)


def _in_proj_ctx_kernel(x_ref, scale_ref, shift_ref, g_ref, w_ref, o_ref, h_ref, *, norm_rows):
    j = pl.program_id(1)
    tm = x_ref.shape[0]

    @pl.when(j == 0)
    def _():
        gmod = g_ref[...] * (1.0 + scale_ref[0])
        shift = shift_ref[0]

        def body(r, carry):
            r0 = pl.multiple_of(r * norm_rows, norm_rows)
            xf = x_ref[pl.ds(r0, norm_rows), :]
            ms = jnp.mean(xf * xf, axis=-1, keepdims=True)
            h_ref[pl.ds(r0, norm_rows), :] = (xf * lax.rsqrt(ms + NORM_EPS) * gmod + shift).astype(BF16)
            return carry

        lax.fori_loop(0, tm // norm_rows, body, 0)

    o_ref[...] = jnp.dot(h_ref[...], w_ref[...], preferred_element_type=F32).astype(o_ref.dtype)


def _in_proj_lat_kernel(x_ref, scale_ref, shift_ref, nscale_ref, nshift_ref, g_ref, w_ref,
                        cos_ref, sina_ref, sinb_ref, o_ref, h_ref, acc_ref, *, norm_rows, fill_rows):
    i = pl.program_id(0)
    j = pl.program_id(1)
    tm = x_ref.shape[0]

    def normalise(r0, nrows, gmod, shift, slot):
        xf = x_ref[pl.ds(r0, nrows), :]
        ms = jnp.mean(xf * xf, axis=-1, keepdims=True)
        h_ref[slot, pl.ds(r0, nrows), :] = (xf * lax.rsqrt(ms + NORM_EPS) * gmod + shift).astype(BF16)

    @pl.when((i == 0) & (j == 0))
    def _():
        gmod = g_ref[...] * (1.0 + scale_ref[0])
        shift = shift_ref[0]

        def body(r, carry):
            normalise(pl.multiple_of(r * norm_rows, norm_rows), norm_rows, gmod, shift, 0)
            return carry

        lax.fori_loop(0, tm // norm_rows, body, 0)

    def step(cur):
        acc = jnp.dot(h_ref[cur], w_ref[...], preferred_element_type=F32)
        o_ref[...] = acc.astype(o_ref.dtype)
        acc_ref[...] = acc
        r0 = pl.multiple_of(jnp.maximum(j - 1, 0) * fill_rows, fill_rows)
        normalise(r0, fill_rows, g_ref[...] * (1.0 + nscale_ref[0]), nshift_ref[0], 1 - cur)

    for par in range(2):
        pl.when(i % 2 == par)(functools.partial(step, par))

    @pl.when(j < 2)
    def _():
        qscale = jnp.where(j == 0, ATT_DIM ** -0.5 * math.log2(math.e), 1.0).astype(F32)
        cos = (cos_ref[...] * qscale).astype(BF16)
        sina = (sina_ref[...] * qscale).astype(BF16)
        sinb = (sinb_ref[...] * qscale).astype(BF16)
        for cb in range(acc_ref.shape[1] // HEAD_W):
            v = acc_ref[:, cb * HEAD_W:(cb + 1) * HEAD_W].astype(BF16)
            up = pltpu.roll(v, HEAD_W - ATT_DIM // 4, 1)
            dn = pltpu.roll(v, ATT_DIM // 4, 1)
            o_ref[:, cb * HEAD_W:(cb + 1) * HEAD_W] = v * cos + up * sina + dn * sinb


def _in_proj_latent(x2d, scale, shift, norm_g, w_bf, tn, seq_len, rope_tables, tm=1024):
    m, d = x2d.shape
    nj = w_bf.shape[1] // tn
    nblk = m // tm
    seq_blocks = seq_len // tm
    assert tm % (nj - 1) == 0
    nxt = lambda i: jnp.minimum(i + 1, nblk - 1)
    in_specs = [pl.BlockSpec((tm, d), lambda i, j: (jnp.minimum(i + jnp.minimum(j, 1), nblk - 1), 0)),
                pl.BlockSpec((1, 1, d), lambda i, j: (i // seq_blocks, 0, 0)),
                pl.BlockSpec((1, 1, d), lambda i, j: (i // seq_blocks, 0, 0)),
                pl.BlockSpec((1, 1, d), lambda i, j: (nxt(i) // seq_blocks, 0, 0)),
                pl.BlockSpec((1, 1, d), lambda i, j: (nxt(i) // seq_blocks, 0, 0)),
                pl.BlockSpec((1, d), lambda i, j: (0, 0)),
                pl.BlockSpec((d, tn), lambda i, j: (0, j))]
    in_specs += [pl.BlockSpec((tm, HEAD_W), lambda i, j: (i % seq_blocks, 0)) for _ in rope_tables]
    est = (2 * tm * d * 4 + 2 * tm * d * 2 + 2 * d * tn * 2 + 2 * tm * tn * 2 + 2 * tm * tn * 4
           + 6 * tm * HEAD_W * 4)
    return pl.pallas_call(
        functools.partial(_in_proj_lat_kernel, norm_rows=256, fill_rows=tm // (nj - 1)),
        grid=(nblk, nj),
        in_specs=in_specs,
        out_specs=pl.BlockSpec((tm, tn), lambda i, j: (i, j)),
        out_shape=jax.ShapeDtypeStruct((m, nj * tn), BF16),
        scratch_shapes=[pltpu.VMEM((2, tm, d), BF16), pltpu.VMEM((tm, tn), F32)],
        compiler_params=pltpu.CompilerParams(dimension_semantics=("arbitrary", "arbitrary"),
                                             vmem_limit_bytes=_vmem_limit(est)),
        name="in_proj_rope",
    )(x2d, scale, shift, scale, shift, norm_g, w_bf, *rope_tables)


def _in_proj_ctx(x2d, scale, shift, norm_g, w_bf, col_blocks, tn, tm=1024):
    m, d = x2d.shape
    nj = len(col_blocks)

    def w_map(i, j):
        idx = jnp.int32(col_blocks[-1])
        for k in range(nj - 2, -1, -1):
            idx = jnp.where(j == k, jnp.int32(col_blocks[k]), idx)
        return (0, idx)

    est = 2 * tm * d * 4 + tm * d * 2 + 2 * d * tn * 2 + 2 * tm * tn * 2 + 2 * tm * tn * 4
    return pl.pallas_call(
        functools.partial(_in_proj_ctx_kernel, norm_rows=256),
        grid=(m // tm, nj),
        in_specs=[pl.BlockSpec((tm, d), lambda i, j: (i, 0)),
                  pl.BlockSpec((1, 1, d), lambda i, j: (0, 0, 0)),
                  pl.BlockSpec((1, 1, d), lambda i, j: (0, 0, 0)),
                  pl.BlockSpec((1, d), lambda i, j: (0, 0)),
                  pl.BlockSpec((d, tn), w_map)],
        out_specs=pl.BlockSpec((tm, tn), lambda i, j: (i, j)),
        out_shape=jax.ShapeDtypeStruct((m, nj * tn), BF16),
        scratch_shapes=[pltpu.VMEM((tm, d), BF16)],
        compiler_params=pltpu.CompilerParams(dimension_semantics=("arbitrary", "arbitrary"),
                                             vmem_limit_bytes=_vmem_limit(est)),
        name="in_proj_ctx",
    )(x2d, scale, shift, norm_g, w_bf)


def _rope_tables(seq):
    rows = seq // GRID_W
    row = np.repeat(np.arange(rows), GRID_W).astype(np.float64)
    col = np.tile(np.arange(GRID_W), rows).astype(np.float64)
    nf = ATT_DIM // 4
    inv = ROPE_BASE ** (-np.arange(nf, dtype=np.float64) / nf)

    def axis_tables(pos):
        ang = pos[:, None] * inv[None, :]
        c, s = np.cos(ang), np.sin(ang)
        z = np.zeros_like(s)
        return (np.concatenate([c, c], -1), np.concatenate([-s, z], -1), np.concatenate([z, s], -1))

    parts = [axis_tables(row), axis_tables(col)]
    out = []
    for k in range(3):
        comp = np.concatenate([parts[0][k], parts[1][k]], -1)
        out.append(jnp.asarray(np.concatenate([comp, comp], -1), dtype=F32))
    return out


def _attn_kernel(q_ref, k_ref, v_ref, kc_ref, vc_ref, ga_ref, lam_ref, gcol_ref, o_ref,
                 qt_scr, kall_scr, vt_scr, s_scr, *, rows):
    seq = k_ref.shape[0]
    keys = seq + kc_ref.shape[0]
    nblk = q_ref.shape[0] // rows
    nch = keys // KEY_CHUNK
    sub = KEY_CHUNK // SUBLANES
    lamv = lam_ref[...]
    lam = (jnp.exp(jnp.sum(lamv[0:1] * lamv[1:2], axis=-1, keepdims=True))
           - jnp.exp(jnp.sum(lamv[2:3] * lamv[3:4], axis=-1, keepdims=True)) + LAM_INIT)

    kall_scr[0:seq, :] = k_ref[...]
    kall_scr[seq:keys, :] = kc_ref[...]
    vt_scr[0:HEAD_W, 0:seq] = v_ref[...].T
    vt_scr[0:HEAD_W, seq:keys] = vc_ref[...].T
    pad_row = lax.broadcasted_iota(jnp.int32, (VT_PAD, keys), 0)
    vt_scr[HEAD_W:HEAD_W + VT_PAD, :] = jnp.where(pad_row == 0, 1.0, 0.0).astype(BF16)
    drow = lax.broadcasted_iota(jnp.int32, (HEAD_W, rows), 0)
    for i in range(nblk):
        qt = q_ref[i * rows:(i + 1) * rows, :].T
        qt_scr[0, i] = jnp.where(drow < ATT_DIM, qt, jnp.zeros_like(qt))
        qt_scr[1, i] = jnp.where(drow >= ATT_DIM, qt, jnp.zeros_like(qt))

    def colmax(x):
        return jnp.max(x.reshape(sub, SUBLANES, rows), axis=0)

    def step(t, slot, do_qk, do_sm, mfin):
        mpart = [None, None]
        acc = [None, None]
        for c in range(nch):
            ks = slice(c * KEY_CHUNK, (c + 1) * KEY_CHUNK)
            if do_qk:
                kc = kall_scr[ks, :]
                for comp in range(2):
                    s = jnp.dot(kc, qt_scr[comp, t], preferred_element_type=F32)
                    s_scr[slot, comp, c] = s
                    cm = colmax(s)
                    mpart[comp] = cm if mpart[comp] is None else jnp.maximum(mpart[comp], cm)
            if do_sm:
                vt = vt_scr[:, ks]
                for comp in range(2):
                    p = jnp.exp2(s_scr[1 - slot, comp, c] - mfin[comp]).astype(BF16)
                    d = jnp.dot(vt, p, preferred_element_type=F32)
                    acc[comp] = d if acc[comp] is None else acc[comp] + d
        if do_sm:
            r0 = pl.multiple_of((t - 1) * rows, rows)
            l1 = acc[0][HEAD_W:HEAD_W + 1]
            l2 = acc[1][HEAD_W:HEAD_W + 1]
            o = acc[0][0:HEAD_W] * (1.0 / l1) - acc[1][0:HEAD_W] * (lam / l2)
            ms = jnp.mean(o * o, axis=0, keepdims=True)
            on = (o * lax.rsqrt(ms + NORM_EPS) * gcol_ref[...] * (1.0 - LAM_INIT)).T
            o_ref[pl.ds(r0, rows), :] = (on * _silu(ga_ref[pl.ds(r0, rows), :].astype(F32))).astype(o_ref.dtype)
        return [jnp.max(m, axis=0, keepdims=True) for m in mpart] if do_qk else mfin

    m = step(0, 0, True, False, None)

    def body(tt, carry):
        t = 2 * tt + 1
        mm = step(t, 1, True, True, list(carry))
        mm = step(t + 1, 0, True, True, mm)
        return mm[0], mm[1]

    m = lax.fori_loop(0, (nblk - 2) // 2, body, (m[0], m[1]))
    m = step(nblk - 1, 1, True, True, list(m))
    step(nblk, 0, False, True, m)


def _attention(qkv, kvc, lam_vecs, subln_g, batch, seq, ctx_len, rows=256):
    kcol = ATT_WIDTH // HEAD_W
    keys = seq + ctx_len
    nch = keys // KEY_CHUNK
    assert keys % KEY_CHUNK == 0 and seq % (2 * rows) == 0
    gcol = jnp.broadcast_to(subln_g.reshape(HEAD_W, 1), (HEAD_W, rows))
    est = (2 * 2 * keys * rows * 4 + 2 * seq * HEAD_W * 2 + 2 * keys * HEAD_W * 2
           + 2 * (4 * seq + 2 * ctx_len) * HEAD_W * 2 + 64 * KEY_CHUNK * rows * 4)
    return pl.pallas_call(
        functools.partial(_attn_kernel, rows=rows),
        grid=(batch, ATT_HEADS),
        in_specs=[pl.BlockSpec((seq, HEAD_W), lambda b, h: (b, h)),
                  pl.BlockSpec((seq, HEAD_W), lambda b, h: (b, kcol + h)),
                  pl.BlockSpec((seq, HEAD_W), lambda b, h: (b, 2 * kcol + h)),
                  pl.BlockSpec((ctx_len, HEAD_W), lambda b, h: (b, h)),
                  pl.BlockSpec((ctx_len, HEAD_W), lambda b, h: (b, kcol + h)),
                  pl.BlockSpec((seq, HEAD_W), lambda b, h: (b, 3 * kcol + h)),
                  pl.BlockSpec((4, ATT_DIM), lambda b, h: (0, 0)),
                  pl.BlockSpec((HEAD_W, rows), lambda b, h: (0, 0))],
        out_specs=pl.BlockSpec((seq, HEAD_W), lambda b, h: (b, h)),
        out_shape=jax.ShapeDtypeStruct((batch * seq, ATT_WIDTH), BF16),
        scratch_shapes=[pltpu.VMEM((2, seq // rows, HEAD_W, rows), BF16),
                        pltpu.VMEM((keys, HEAD_W), BF16),
                        pltpu.VMEM((HEAD_W + VT_PAD, keys), BF16),
                        pltpu.VMEM((2, 2, nch, KEY_CHUNK, rows), F32)],
        compiler_params=pltpu.CompilerParams(dimension_semantics=("arbitrary",) * 2,
                                             vmem_limit_bytes=_vmem_limit(est)),
        name="diff_attention",
    )(qkv, qkv, qkv, kvc, kvc, qkv, lam_vecs, gcol)


def _ssm_prep_kernel(lre_ref, lim_ref, lstep_ref, bre_ref, bim_ref, cre_ref, cim_ref,
                     a_ref, bblk_ref, cblk_ref):
    lre = lre_ref[0]
    lim = lim_ref[0]
    dt = jnp.exp(lstep_ref[0])
    mag = jnp.exp(lre * dt)
    ar = mag * jnp.cos(lim * dt)
    ai = mag * jnp.sin(lim * dt)
    den = lre * lre + lim * lim
    fr = ((ar - 1.0) * lre + ai * lim) / den
    fi = (ai * lre - (ar - 1.0) * lim) / den
    a_ref[0, 0:1, :] = ar
    a_ref[0, 1:2, :] = ai
    br = bre_ref[0]
    bi = bim_ref[0]
    bbr = fr * br - fi * bi
    bbi = fr * bi + fi * br
    row_g = lax.broadcasted_iota(jnp.int32, (LANES, BLOCK_MODES), 0) // SSM_GROUP
    col_g = lax.broadcasted_iota(jnp.int32, (LANES, BLOCK_MODES), 1) // SSM_STATE
    keep_b = row_g == col_g
    row_g2 = lax.broadcasted_iota(jnp.int32, (BLOCK_MODES, LANES), 0) // SSM_STATE
    col_g2 = lax.broadcasted_iota(jnp.int32, (BLOCK_MODES, LANES), 1) // SSM_GROUP
    keep_c = row_g2 == col_g2
    for k in range(SSM_BLOCKS):
        sl = slice(k * BLOCK_MODES, (k + 1) * BLOCK_MODES)
        bblk_ref[0, k, :, 0:BLOCK_MODES] = jnp.where(keep_b, bbr[:, sl], 0.0).astype(BF16)
        bblk_ref[0, k, :, BLOCK_MODES:BLOCK_STATE] = jnp.where(keep_b, bbi[:, sl], 0.0).astype(BF16)
        cblk_ref[0, k, 0:BLOCK_MODES, :] = jnp.where(keep_c, cre_ref[0, sl, :], 0.0).astype(BF16)
        cblk_ref[0, k, BLOCK_MODES:BLOCK_STATE, :] = jnp.where(keep_c, -cim_ref[0, sl, :], 0.0).astype(BF16)


def _ssm_prep(lre, lim, lstep, bre_t, bim_t, cre_t, cim_t):
    nd = lre.shape[0]
    vec = pl.BlockSpec((1, 1, SSM_MODES), lambda d: (d, 0, 0))
    bsp = pl.BlockSpec((1, LANES, SSM_MODES), lambda d: (d, 0, 0))
    csp = pl.BlockSpec((1, SSM_MODES, LANES), lambda d: (d, 0, 0))
    est = 2 * (4 * LANES * SSM_MODES * 4 + 2 * SSM_BLOCKS * LANES * BLOCK_STATE * 2) + 8 * LANES * SSM_MODES * 4
    return pl.pallas_call(
        _ssm_prep_kernel,
        grid=(nd,),
        in_specs=[vec, vec, vec, bsp, bsp, csp, csp],
        out_specs=[pl.BlockSpec((1, 2, SSM_MODES), lambda d: (d, 0, 0)),
                   pl.BlockSpec((1, SSM_BLOCKS, LANES, BLOCK_STATE), lambda d: (d, 0, 0, 0)),
                   pl.BlockSpec((1, SSM_BLOCKS, BLOCK_STATE, LANES), lambda d: (d, 0, 0, 0))],
        out_shape=[jax.ShapeDtypeStruct((nd, 2, SSM_MODES), F32),
                   jax.ShapeDtypeStruct((nd, SSM_BLOCKS, LANES, BLOCK_STATE), BF16),
                   jax.ShapeDtypeStruct((nd, SSM_BLOCKS, BLOCK_STATE, LANES), BF16)],
        compiler_params=pltpu.CompilerParams(dimension_semantics=("arbitrary",),
                                             vmem_limit_bytes=_vmem_limit(est)),
        name="ssm_prep",
    )(lre, lim, lstep, bre_t, bim_t, cre_t, cim_t)


def _ssm_scan_kernel(*refs, reverse, batch, ctx_chunks):
    if reverse:
        uc_ref, ul_ref, a_ref, bblk_ref, cblk_ref, yf_ref, d_ref, y_ref, s_ref, hb_ref, h_ref = refs
    else:
        uc_ref, ul_ref, a_ref, bblk_ref, cblk_ref, y_ref, s_ref, hb_ref, h_ref = refs
    steps = ul_ref.shape[1]
    rows = steps * batch
    step = pl.program_id(0)

    @pl.when(step == 0)
    def _():
        h_ref[...] = jnp.zeros_like(h_ref)

    u_bt = jnp.where(step < ctx_chunks, uc_ref[...], ul_ref[...]).astype(F32)
    u32 = jnp.swapaxes(u_bt, 0, 1).reshape(rows, SSM_WIDTH)
    u = u32.astype(BF16)
    for k in range(SSM_BLOCKS):
        s_ref[:, k * BLOCK_STATE:(k + 1) * BLOCK_STATE] = jnp.dot(
            u[:, k * LANES:(k + 1) * LANES], bblk_ref[0, k], preferred_element_type=F32)

    for k in range(SSM_BLOCKS):
        c_re = k * BLOCK_STATE
        c_im = c_re + BLOCK_MODES
        ar = jnp.broadcast_to(a_ref[0, 0:1, k * BLOCK_MODES:(k + 1) * BLOCK_MODES], (batch, BLOCK_MODES))
        ai = jnp.broadcast_to(a_ref[0, 1:2, k * BLOCK_MODES:(k + 1) * BLOCK_MODES], (batch, BLOCK_MODES))
        hr = h_ref[:, c_re:c_re + BLOCK_MODES]
        hi = h_ref[:, c_im:c_im + BLOCK_MODES]
        for i in range(0, steps, 2):
            pair = []
            for t in ((steps - 1 - i, steps - 2 - i) if reverse else (i, i + 1)):
                r0 = t * batch
                nhr = ar * hr - ai * hi + s_ref[r0:r0 + batch, c_re:c_re + BLOCK_MODES]
                nhi = ar * hi + ai * hr + s_ref[r0:r0 + batch, c_im:c_im + BLOCK_MODES]
                pair.append((nhr, nhi))
                hr, hi = nhr, nhi
            lo, hi2 = (pair[1], pair[0]) if reverse else (pair[0], pair[1])
            p0 = min(steps - 2 - i, steps - 1 - i) * batch if reverse else i * batch
            hb_ref[p0:p0 + 2 * batch, c_re:c_re + BLOCK_MODES] = jnp.concatenate([lo[0], hi2[0]], axis=0).astype(BF16)
            hb_ref[p0:p0 + 2 * batch, c_im:c_im + BLOCK_MODES] = jnp.concatenate([lo[1], hi2[1]], axis=0).astype(BF16)
        h_ref[:, c_re:c_re + BLOCK_MODES] = hr
        h_ref[:, c_im:c_im + BLOCK_MODES] = hi

    ys = []
    for k in range(SSM_BLOCKS):
        yk = jnp.dot(hb_ref[:, k * BLOCK_STATE:(k + 1) * BLOCK_STATE], cblk_ref[0, k],
                     preferred_element_type=F32)
        cols = slice(k * LANES, (k + 1) * LANES)
        if reverse:
            yk = yk + yf_ref[:, cols] + d_ref[:, cols] * u32[:, cols]
        ys.append(yk)
    y = jnp.concatenate(ys, axis=1)
    if reverse:
        y_ref[...] = jnp.swapaxes(y.reshape(steps, batch, SSM_WIDTH), 0, 1)
    else:
        y_ref[...] = y


def _ssm_scan(u_ctx, u_lat, u_ctx_col, u_lat_col, a, bblk, cblk, direction, yf=None, dvec=None):
    batch = u_lat.shape[0]
    rows = SSM_CHUNK * batch
    cc = u_ctx.shape[1] // SSM_CHUNK
    lc = u_lat.shape[1] // SSM_CHUNK
    reverse = direction == 1
    if reverse:
        cidx = lambda s: jnp.maximum(cc - 1 - s, 0)
        lidx = lambda s: jnp.minimum(lc - 1, lc - 1 + cc - s)
    else:
        cidx = lambda s: jnp.minimum(s, cc - 1)
        lidx = lambda s: jnp.maximum(s - cc, 0)
    in_specs = [pl.BlockSpec((batch, SSM_CHUNK, SSM_WIDTH), lambda s: (0, cidx(s), u_ctx_col)),
                pl.BlockSpec((batch, SSM_CHUNK, SSM_WIDTH), lambda s: (0, lidx(s), u_lat_col)),
                pl.BlockSpec((1, 2, SSM_MODES), lambda s: (direction, 0, 0)),
                pl.BlockSpec((1, SSM_BLOCKS, LANES, BLOCK_STATE), lambda s: (direction, 0, 0, 0)),
                pl.BlockSpec((1, SSM_BLOCKS, BLOCK_STATE, LANES), lambda s: (direction, 0, 0, 0))]
    args = [u_ctx, u_lat, a, bblk, cblk]
    if reverse:
        in_specs += [pl.BlockSpec((rows, SSM_WIDTH), lambda s: (lidx(s), 0)),
                     pl.BlockSpec((1, SSM_WIDTH), lambda s: (0, 0))]
        args += [yf, dvec]
        out_spec = pl.BlockSpec((batch, SSM_CHUNK, SSM_WIDTH), lambda s: (0, lidx(s), 0))
        out_shape = jax.ShapeDtypeStruct((batch, lc * SSM_CHUNK, SSM_WIDTH), F32)
    else:
        out_spec = pl.BlockSpec((rows, SSM_WIDTH), lambda s: (lidx(s), 0))
        out_shape = jax.ShapeDtypeStruct((lc * rows, SSM_WIDTH), F32)
    state_w = SSM_BLOCKS * BLOCK_STATE
    est = (rows * state_w * 6 + 2 * rows * state_w // SSM_BLOCKS * 4 + 10 * rows * SSM_WIDTH * 4
           + 4 * SSM_BLOCKS * LANES * BLOCK_STATE * 2)
    return pl.pallas_call(
        functools.partial(_ssm_scan_kernel, reverse=reverse, batch=batch, ctx_chunks=cc),
        grid=(cc + lc,),
        in_specs=in_specs,
        out_specs=out_spec,
        out_shape=out_shape,
        scratch_shapes=[pltpu.VMEM((rows, state_w), F32), pltpu.VMEM((rows, state_w), BF16),
                        pltpu.VMEM((batch, state_w), F32)],
        compiler_params=pltpu.CompilerParams(dimension_semantics=("arbitrary",),
                                             vmem_limit_bytes=_vmem_limit(est)),
        name="ssm_scan_bwd" if reverse else "ssm_scan_fwd",
    )(*args)


def _gelu_tanh(v):
    return 0.5 * v * (1.0 + jnp.tanh(math.sqrt(2.0 / math.pi) * (v + 0.044715 * v * v * v)))


def _merge_kernel(a_ref, y_ref, gs_ref, gma0_ref, gma1_ref, gms0_ref, gms1_ref, x_ref, gate_ref,
                  wglu_ref, bglu_ref, wpa_ref, wps_ref, wout_ref, fg_ref, o_ref):
    yg = _gelu_tanh(y_ref[...])
    z = yg * _sigmoid(jnp.dot(yg.astype(BF16), wglu_ref[...], preferred_element_type=F32) + bglu_ref[...])
    s_br = (z * _silu(gs_ref[...].astype(F32))).astype(BF16)
    ta = jnp.dot(a_ref[...], wpa_ref[...], preferred_element_type=F32)
    ts = jnp.dot(s_br, wps_ref[...], preferred_element_type=F32)
    half = D_MODEL // 2
    parts = []
    for p, (ga_r, gs_r) in enumerate(((gma0_ref, gms0_ref), (gma1_ref, gms1_ref))):
        cols = slice(p * half, (p + 1) * half)
        parts.append((_sigmoid(ga_r[...].astype(F32)) * ta[:, cols]
                      + _sigmoid(gs_r[...].astype(F32)) * ts[:, cols]).astype(BF16))
    t = jnp.concatenate(parts, axis=1)
    out = jnp.dot(t, wout_ref[...], preferred_element_type=F32)
    xn = x_ref[...] + gate_ref[0] * out
    ms = jnp.mean(xn * xn, axis=-1, keepdims=True)
    o_ref[...] = xn * lax.rsqrt(ms + NORM_EPS) * fg_ref[...]


def _merge_out(a_br, y_bm, qkv, x2d, gate, w_glu, b_glu, w_pa, w_ps, w_out, final_g, seq, tm=256):
    m, d = x2d.shape
    half = d // 2
    per_batch = seq // tm
    gs_col = (4 * ATT_WIDTH + SSM_WIDTH) // SSM_WIDTH
    gm_col = (4 * ATT_WIDTH + 2 * SSM_WIDTH) // half
    const = lambda i: (0, 0)
    est = (2 * (tm * ATT_WIDTH * 2 + tm * SSM_WIDTH * 6 + 4 * tm * half * 2 + 2 * tm * d * 4)
           + 2 * (SSM_WIDTH * SSM_WIDTH + ATT_WIDTH * d + SSM_WIDTH * d + d * d) * 2 + 6 * tm * d * 4)
    return pl.pallas_call(
        _merge_kernel,
        grid=(m // tm,),
        in_specs=[pl.BlockSpec((tm, ATT_WIDTH), lambda i: (i, 0)),
                  pl.BlockSpec((tm, SSM_WIDTH), lambda i: (i, 0)),
                  pl.BlockSpec((tm, SSM_WIDTH), lambda i: (i, gs_col)),
                  pl.BlockSpec((tm, half), lambda i: (i, gm_col)),
                  pl.BlockSpec((tm, half), lambda i: (i, gm_col + 1)),
                  pl.BlockSpec((tm, half), lambda i: (i, gm_col + 2)),
                  pl.BlockSpec((tm, half), lambda i: (i, gm_col + 3)),
                  pl.BlockSpec((tm, d), lambda i: (i, 0)),
                  pl.BlockSpec((1, 1, d), lambda i: (i // per_batch, 0, 0)),
                  pl.BlockSpec((SSM_WIDTH, SSM_WIDTH), const),
                  pl.BlockSpec((1, SSM_WIDTH), const),
                  pl.BlockSpec((ATT_WIDTH, d), const),
                  pl.BlockSpec((SSM_WIDTH, d), const),
                  pl.BlockSpec((d, d), const),
                  pl.BlockSpec((1, d), const)],
        out_specs=pl.BlockSpec((tm, d), lambda i: (i, 0)),
        out_shape=jax.ShapeDtypeStruct((m, d), F32),
        compiler_params=pltpu.CompilerParams(dimension_semantics=("arbitrary",),
                                             vmem_limit_bytes=_vmem_limit(est)),
        name="merge_out",
    )(a_br, y_bm, qkv, qkv, qkv, qkv, qkv, x2d, gate, w_glu, b_glu, w_pa, w_ps, w_out, final_g)


def kernel(x, c, ctx, c_ctx, w_ada, b_ada, norm_g, w_in, lambda_q1, lambda_k1, lambda_q2, lambda_k2,
           subln_g, ssm_lambda_re, ssm_lambda_im, ssm_log_step, ssm_b_re, ssm_b_im, ssm_c_re, ssm_c_im,
           ssm_d, w_glu, b_glu, w_pa, w_ps, w_out, final_g):
    batch, seq, d = x.shape
    ctx_len = ctx.shape[1]
    assert w_ada.shape[0] == 1, "single-layer trunk"
    assert (d, seq % 1024, ctx_len % SSM_CHUNK, seq % SSM_CHUNK) == (D_MODEL, 0, 0, 0)

    pad = (-(batch + 1)) % SUBLANES
    cc = jnp.concatenate([c, c_ctx[None, :], jnp.zeros((pad, d), F32)], axis=0)
    mod = _adaln(cc, w_ada[0], b_ada[0][None, :])
    shift, scale, gate = (mod[:, k * d:(k + 1) * d] for k in range(3))
    as_mod = lambda v, lo, hi: v[lo:hi].reshape(hi - lo, 1, d)

    w_in_bf = w_in[0].astype(BF16)
    x2d = x.reshape(batch * seq, d)
    ctx2d = ctx.reshape(batch * ctx_len, d)
    ng = norm_g[0][None, :]

    tn = 1024
    u_col = 4 * ATT_WIDTH
    qkv = _in_proj_latent(x2d, as_mod(scale, 0, batch), as_mod(shift, 0, batch), ng, w_in_bf,
                          tn, seq, _rope_tables(seq))
    tnc = SSM_WIDTH
    k0 = ATT_WIDTH // tnc
    ctx_cols = list(range(k0, 3 * k0)) + [u_col // tnc]
    kvc = _in_proj_ctx(ctx2d, as_mod(scale, batch, batch + 1), as_mod(shift, batch, batch + 1), ng, w_in_bf,
                       ctx_cols, tnc)

    lam_vecs = jnp.stack([lambda_q1[0], lambda_k1[0], lambda_q2[0], lambda_k2[0]]).astype(F32)
    a_br = _attention(qkv, kvc, lam_vecs, subln_g[0].astype(F32), batch, seq, ctx_len)

    nd = ssm_lambda_re.shape[1]
    modes = lambda v: v[0].reshape(nd, 1, SSM_MODES)
    lstep = jnp.repeat(ssm_log_step[0], SSM_STATE, axis=-1).reshape(nd, 1, SSM_MODES)
    b_t = lambda v: jnp.tile(v[0].reshape(nd, SSM_MODES, SSM_GROUP).transpose(0, 2, 1), (1, GROUPS_PER_BLOCK, 1))
    c_t = lambda v: jnp.tile(v[0].transpose(0, 1, 3, 2).reshape(nd, SSM_MODES, SSM_GROUP), (1, 1, GROUPS_PER_BLOCK))
    a_disc, bblk, cblk = _ssm_prep(modes(ssm_lambda_re), modes(ssm_lambda_im), lstep,
                                   b_t(ssm_b_re), b_t(ssm_b_im), c_t(ssm_c_re), c_t(ssm_c_im))

    qkv3 = qkv.reshape(batch, seq, IN_WIDTH)
    kvc3 = kvc.reshape(batch, ctx_len, len(ctx_cols) * tnc)
    scan = functools.partial(_ssm_scan, kvc3, qkv3, len(ctx_cols) - 1, u_col // SSM_WIDTH, a_disc, bblk, cblk)
    yf = scan(0)
    y_bm = scan(1, yf=yf, dvec=ssm_d[0].reshape(1, SSM_WIDTH).astype(F32))

    out = _merge_out(a_br, y_bm.reshape(batch * seq, SSM_WIDTH), qkv, x2d, as_mod(gate, 0, batch),
                     w_glu[0].astype(BF16), b_glu[0][None, :], w_pa[0].astype(BF16), w_ps[0].astype(BF16),
                     w_out[0].astype(BF16), final_g[None, :], seq)
    return out.reshape(batch, seq, d)
```

```python
import functools
import math

import jax
import jax.numpy as jnp
import numpy as np
from jax import lax
from jax.experimental import pallas as pl
from jax.experimental.pallas import tpu as pltpu

F32 = jnp.float32
BF16 = jnp.bfloat16

D_MODEL = 2048
GRID_W = 64
ATT_HEADS = 8
ATT_DIM = 64
HEAD_W = 2 * ATT_DIM
ATT_WIDTH = ATT_HEADS * HEAD_W
SSM_GROUP = 16
SSM_GROUPS = 32
SSM_WIDTH = SSM_GROUP * SSM_GROUPS
SSM_STATE = 64
SSM_MODES = SSM_GROUPS * SSM_STATE
IN_WIDTH = 4 * ATT_WIDTH + 2 * SSM_WIDTH + 2 * D_MODEL
ROPE_BASE = 10000.0
NORM_EPS = 1e-6
LAM_INIT = 0.8 - 0.6 * math.exp(0.0)

LANES = 128
SUBLANES = 8
VMEM_BYTES_V7X = 64 * 1024 * 1024
MIB = 1024 * 1024

GROUPS_PER_BLOCK = LANES // SSM_GROUP
SSM_BLOCKS = SSM_GROUPS // GROUPS_PER_BLOCK
BLOCK_MODES = GROUPS_PER_BLOCK * SSM_STATE
BLOCK_STATE = 2 * BLOCK_MODES
SSM_CHUNK = 128
KEY_CHUNK = 256
VT_PAD = 16
HEADS_PER_STEP = 2


def _vmem_limit(nbytes):
    return int(min(nbytes + nbytes // 4, VMEM_BYTES_V7X - 8 * MIB))


def _sigmoid(v):
    return 1.0 / (1.0 + jnp.exp(-v))


def _silu(v):
    return v * _sigmoid(v)


def _adaln_kernel(c_ref, w_ref, b_ref, o_ref):
    o_ref[...] = jnp.dot(_silu(c_ref[...]).astype(BF16), w_ref[...].astype(BF16),
                         preferred_element_type=F32) + b_ref[...]


def _adaln(cc, w_ada, b_ada):
    rows, d = cc.shape
    n = w_ada.shape[1]
    tn = 768
    est = 2 * (d * tn * 4) + 2 * rows * tn * 4 + rows * d * 4 * 2
    return pl.pallas_call(
        _adaln_kernel,
        grid=(n // tn,),
        in_specs=[pl.BlockSpec((rows, d), lambda j: (0, 0)),
                  pl.BlockSpec((d, tn), lambda j: (0, j)),
                  pl.BlockSpec((1, tn), lambda j: (0, j))],
        out_specs=pl.BlockSpec((rows, tn), lambda j: (0, j)),
        out_shape=jax.ShapeDtypeStruct((rows, n), F32),
        compiler_params=pltpu.CompilerParams(dimension_semantics=("arbitrary",),
                                             vmem_limit_bytes=_vmem_limit(est)),
        name="adaln",
    )(cc, w_ada, b_ada)


def _in_proj_ctx_kernel(x_ref, scale_ref, shift_ref, g_ref, w_ref, o_ref, h_ref, *, norm_rows):
    j = pl.program_id(1)
    tm = x_ref.shape[0]

    @pl.when(j == 0)
    def _():
        gmod = g_ref[...] * (1.0 + scale_ref[0])
        shift = shift_ref[0]

        def body(r, carry):
            r0 = pl.multiple_of(r * norm_rows, norm_rows)
            xf = x_ref[pl.ds(r0, norm_rows), :]
            ms = jnp.mean(xf * xf, axis=-1, keepdims=True)
            h_ref[pl.ds(r0, norm_rows), :] = (xf * lax.rsqrt(ms + NORM_EPS) * gmod + shift).astype(BF16)
            return carry

        lax.fori_loop(0, tm // norm_rows, body, 0)

    o_ref[...] = jnp.dot(h_ref[...], w_ref[...], preferred_element_type=F32).astype(o_ref.dtype)


def _in_proj_lat_kernel(x_ref, scale_ref, shift_ref, nscale_ref, nshift_ref, g_ref, w_ref,
                        cos_ref, sina_ref, sinb_ref, o_ref, h_ref, acc_ref, *, norm_rows, fill_rows):
    i = pl.program_id(0)
    j = pl.program_id(1)
    tm = x_ref.shape[0]

    def normalise(r0, nrows, gmod, shift, slot):
        xf = x_ref[pl.ds(r0, nrows), :]
        ms = jnp.mean(xf * xf, axis=-1, keepdims=True)
        h_ref[slot, pl.ds(r0, nrows), :] = (xf * lax.rsqrt(ms + NORM_EPS) * gmod + shift).astype(BF16)

    @pl.when((i == 0) & (j == 0))
    def _():
        gmod = g_ref[...] * (1.0 + scale_ref[0])
        shift = shift_ref[0]

        def body(r, carry):
            normalise(pl.multiple_of(r * norm_rows, norm_rows), norm_rows, gmod, shift, 0)
            return carry

        lax.fori_loop(0, tm // norm_rows, body, 0)

    def step(cur):
        acc = jnp.dot(h_ref[cur], w_ref[...], preferred_element_type=F32)
        o_ref[...] = acc.astype(o_ref.dtype)
        acc_ref[...] = acc
        r0 = pl.multiple_of(jnp.maximum(j - 1, 0) * fill_rows, fill_rows)
        normalise(r0, fill_rows, g_ref[...] * (1.0 + nscale_ref[0]), nshift_ref[0], 1 - cur)

    for par in range(2):
        pl.when(i % 2 == par)(functools.partial(step, par))

    @pl.when(j < 2)
    def _():
        qscale = jnp.where(j == 0, ATT_DIM ** -0.5 * math.log2(math.e), 1.0).astype(F32)
        cos = (cos_ref[...] * qscale).astype(BF16)
        sina = (sina_ref[...] * qscale).astype(BF16)
        sinb = (sinb_ref[...] * qscale).astype(BF16)
        for cb in range(acc_ref.shape[1] // HEAD_W):
            v = acc_ref[:, cb * HEAD_W:(cb + 1) * HEAD_W].astype(BF16)
            up = pltpu.roll(v, HEAD_W - ATT_DIM // 4, 1)
            dn = pltpu.roll(v, ATT_DIM // 4, 1)
            o_ref[:, cb * HEAD_W:(cb + 1) * HEAD_W] = v * cos + up * sina + dn * sinb


def _in_proj_latent(x2d, scale, shift, norm_g, w_bf, tn, seq_len, rope_tables, tm=1024):
    m, d = x2d.shape
    nj = w_bf.shape[1] // tn
    nblk = m // tm
    seq_blocks = seq_len // tm
    assert tm % (nj - 1) == 0
    nxt = lambda i: jnp.minimum(i + 1, nblk - 1)
    in_specs = [pl.BlockSpec((tm, d), lambda i, j: (jnp.minimum(i + jnp.minimum(j, 1), nblk - 1), 0)),
                pl.BlockSpec((1, 1, d), lambda i, j: (i // seq_blocks, 0, 0)),
                pl.BlockSpec((1, 1, d), lambda i, j: (i // seq_blocks, 0, 0)),
                pl.BlockSpec((1, 1, d), lambda i, j: (nxt(i) // seq_blocks, 0, 0)),
                pl.BlockSpec((1, 1, d), lambda i, j: (nxt(i) // seq_blocks, 0, 0)),
                pl.BlockSpec((1, d), lambda i, j: (0, 0)),
                pl.BlockSpec((d, tn), lambda i, j: (0, j))]
    in_specs += [pl.BlockSpec((tm, HEAD_W), lambda i, j: (i % seq_blocks, 0)) for _ in rope_tables]
    est = (2 * tm * d * 4 + 2 * tm * d * 2 + 2 * d * tn * 2 + 2 * tm * tn * 2 + 2 * tm * tn * 4
           + 6 * tm * HEAD_W * 4)
    return pl.pallas_call(
        functools.partial(_in_proj_lat_kernel, norm_rows=256, fill_rows=tm // (nj - 1)),
        grid=(nblk, nj),
        in_specs=in_specs,
        out_specs=pl.BlockSpec((tm, tn), lambda i, j: (i, j)),
        out_shape=jax.ShapeDtypeStruct((m, nj * tn), BF16),
        scratch_shapes=[pltpu.VMEM((2, tm, d), BF16), pltpu.VMEM((tm, tn), F32)],
        compiler_params=pltpu.CompilerParams(dimension_semantics=("arbitrary", "arbitrary"),
                                             vmem_limit_bytes=_vmem_limit(est)),
        name="in_proj_rope",
    )(x2d, scale, shift, scale, shift, norm_g, w_bf, *rope_tables)


def _in_proj_ctx(x2d, scale, shift, norm_g, w_bf, col_blocks, tn, tm=1024):
    m, d = x2d.shape
    nj = len(col_blocks)

    def w_map(i, j):
        idx = jnp.int32(col_blocks[-1])
        for k in range(nj - 2, -1, -1):
            idx = jnp.where(j == k, jnp.int32(col_blocks[k]), idx)
        return (0, idx)

    est = 2 * tm * d * 4 + tm * d * 2 + 2 * d * tn * 2 + 2 * tm * tn * 2 + 2 * tm * tn * 4
    return pl.pallas_call(
        functools.partial(_in_proj_ctx_kernel, norm_rows=256),
        grid=(m // tm, nj),
        in_specs=[pl.BlockSpec((tm, d), lambda i, j: (i, 0)),
                  pl.BlockSpec((1, 1, d), lambda i, j: (0, 0, 0)),
                  pl.BlockSpec((1, 1, d), lambda i, j: (0, 0, 0)),
                  pl.BlockSpec((1, d), lambda i, j: (0, 0)),
                  pl.BlockSpec((d, tn), w_map)],
        out_specs=pl.BlockSpec((tm, tn), lambda i, j: (i, j)),
        out_shape=jax.ShapeDtypeStruct((m, nj * tn), BF16),
        scratch_shapes=[pltpu.VMEM((tm, d), BF16)],
        compiler_params=pltpu.CompilerParams(dimension_semantics=("arbitrary", "arbitrary"),
                                             vmem_limit_bytes=_vmem_limit(est)),
        name="in_proj_ctx",
    )(x2d, scale, shift, norm_g, w_bf)


def _rope_tables(seq):
    rows = seq // GRID_W
    row = np.repeat(np.arange(rows), GRID_W).astype(np.float64)
    col = np.tile(np.arange(GRID_W), rows).astype(np.float64)
    nf = ATT_DIM // 4
    inv = ROPE_BASE ** (-np.arange(nf, dtype=np.float64) / nf)

    def axis_tables(pos):
        ang = pos[:, None] * inv[None, :]
        c, s = np.cos(ang), np.sin(ang)
        z = np.zeros_like(s)
        return (np.concatenate([c, c], -1), np.concatenate([-s, z], -1), np.concatenate([z, s], -1))

    parts = [axis_tables(row), axis_tables(col)]
    out = []
    for k in range(3):
        comp = np.concatenate([parts[0][k], parts[1][k]], -1)
        out.append(jnp.asarray(np.concatenate([comp, comp], -1), dtype=F32))
    return out


def _attn_kernel(q_ref, k_ref, v_ref, kc_ref, vc_ref, ga_ref, lam_ref, gcol_ref, o_ref,
                 qt_scr, kall_scr, vt_scr, s_scr, *, rows):
    seq = k_ref.shape[0]
    keys = seq + kc_ref.shape[0]
    nblk = q_ref.shape[0] // rows
    nch = keys // KEY_CHUNK
    sub = KEY_CHUNK // SUBLANES
    lamv = lam_ref[...]
    lam = (jnp.exp(jnp.sum(lamv[0:1] * lamv[1:2], axis=-1, keepdims=True))
           - jnp.exp(jnp.sum(lamv[2:3] * lamv[3:4], axis=-1, keepdims=True)) + LAM_INIT)
    pad_row = lax.broadcasted_iota(jnp.int32, (VT_PAD, keys), 0)
    drow = lax.broadcasted_iota(jnp.int32, (HEAD_W, rows), 0)

    def colmax(x):
        return jnp.max(x.reshape(sub, SUBLANES, rows), axis=0)

    def head(hh):
        hs = slice(hh * HEAD_W, (hh + 1) * HEAD_W)
        kall_scr[hh, 0:seq, :] = k_ref[:, hs]
        kall_scr[hh, seq:keys, :] = kc_ref[:, hs]
        vt_scr[hh, 0:HEAD_W, 0:seq] = v_ref[:, hs].T
        vt_scr[hh, 0:HEAD_W, seq:keys] = vc_ref[:, hs].T
        vt_scr[hh, HEAD_W:HEAD_W + VT_PAD, :] = jnp.where(pad_row == 0, 1.0, 0.0).astype(BF16)
        for i in range(nblk):
            qt = q_ref[i * rows:(i + 1) * rows, hs].T
            qt_scr[hh, 0, i] = jnp.where(drow < ATT_DIM, qt, jnp.zeros_like(qt))
            qt_scr[hh, 1, i] = jnp.where(drow >= ATT_DIM, qt, jnp.zeros_like(qt))

        def step(t, slot, do_qk, do_sm, mfin):
            mpart = [None, None]
            acc = [None, None]
            for c in range(nch):
                ks = slice(c * KEY_CHUNK, (c + 1) * KEY_CHUNK)
                if do_qk:
                    kc = kall_scr[hh, ks, :]
                    for comp in range(2):
                        s = jnp.dot(kc, qt_scr[hh, comp, t], preferred_element_type=F32)
                        s_scr[slot, comp, c] = s
                        cm = colmax(s)
                        mpart[comp] = cm if mpart[comp] is None else jnp.maximum(mpart[comp], cm)
                if do_sm:
                    vt = vt_scr[hh, :, ks]
                    for comp in range(2):
                        p = jnp.exp2(s_scr[1 - slot, comp, c] - mfin[comp]).astype(BF16)
                        d = jnp.dot(vt, p, preferred_element_type=F32)
                        acc[comp] = d if acc[comp] is None else acc[comp] + d
            if do_sm:
                r0 = pl.multiple_of((t - 1) * rows, rows)
                l1 = acc[0][HEAD_W:HEAD_W + 1]
                l2 = acc[1][HEAD_W:HEAD_W + 1]
                o = acc[0][0:HEAD_W] * (1.0 / l1) - acc[1][0:HEAD_W] * (lam / l2)
                ms = jnp.mean(o * o, axis=0, keepdims=True)
                on = (o * lax.rsqrt(ms + NORM_EPS) * gcol_ref[...] * (1.0 - LAM_INIT)).T
                o_ref[pl.ds(r0, rows), hs] = (on * _silu(ga_ref[pl.ds(r0, rows), hs].astype(F32))).astype(o_ref.dtype)
            return [jnp.max(m, axis=0, keepdims=True) for m in mpart] if do_qk else mfin

        m = step(0, 0, True, False, None)

        def body(tt, carry):
            t = 2 * tt + 1
            mm = step(t, 1, True, True, list(carry))
            mm = step(t + 1, 0, True, True, mm)
            return mm[0], mm[1]

        m = lax.fori_loop(0, (nblk - 2) // 2, body, (m[0], m[1]))
        m = step(nblk - 1, 1, True, True, list(m))
        step(nblk, 0, False, True, m)

    for hh in range(HEADS_PER_STEP):
        head(hh)


def _attention(qkv, kvc, lam_vecs, subln_g, batch, seq, ctx_len, rows=256):
    wblk = HEADS_PER_STEP * HEAD_W
    kcol = ATT_WIDTH // wblk
    keys = seq + ctx_len
    nch = keys // KEY_CHUNK
    assert keys % KEY_CHUNK == 0 and seq % (2 * rows) == 0 and ATT_HEADS % HEADS_PER_STEP == 0
    gcol = jnp.broadcast_to(subln_g.reshape(HEAD_W, 1), (HEAD_W, rows))
    est = (2 * 2 * keys * rows * 4 + HEADS_PER_STEP * (2 * seq * HEAD_W * 2 + 2 * keys * HEAD_W * 2)
           + 2 * (4 * seq + 2 * ctx_len) * wblk * 2 + 64 * KEY_CHUNK * rows * 4)
    return pl.pallas_call(
        functools.partial(_attn_kernel, rows=rows),
        grid=(batch, ATT_HEADS // HEADS_PER_STEP),
        in_specs=[pl.BlockSpec((seq, wblk), lambda b, h: (b, h)),
                  pl.BlockSpec((seq, wblk), lambda b, h: (b, kcol + h)),
                  pl.BlockSpec((seq, wblk), lambda b, h: (b, 2 * kcol + h)),
                  pl.BlockSpec((ctx_len, wblk), lambda b, h: (b, h)),
                  pl.BlockSpec((ctx_len, wblk), lambda b, h: (b, kcol + h)),
                  pl.BlockSpec((seq, wblk), lambda b, h: (b, 3 * kcol + h)),
                  pl.BlockSpec((4, ATT_DIM), lambda b, h: (0, 0)),
                  pl.BlockSpec((HEAD_W, rows), lambda b, h: (0, 0))],
        out_specs=pl.BlockSpec((seq, wblk), lambda b, h: (b, h)),
        out_shape=jax.ShapeDtypeStruct((batch * seq, ATT_WIDTH), BF16),
        scratch_shapes=[pltpu.VMEM((HEADS_PER_STEP, 2, seq // rows, HEAD_W, rows), BF16),
                        pltpu.VMEM((HEADS_PER_STEP, keys, HEAD_W), BF16),
                        pltpu.VMEM((HEADS_PER_STEP, HEAD_W + VT_PAD, keys), BF16),
                        pltpu.VMEM((2, 2, nch, KEY_CHUNK, rows), F32)],
        compiler_params=pltpu.CompilerParams(dimension_semantics=("arbitrary",) * 2,
                                             vmem_limit_bytes=_vmem_limit(est)),
        name="diff_attention",
    )(qkv, qkv, qkv, kvc, kvc, qkv, lam_vecs, gcol)


def _ssm_prep_kernel(lre_ref, lim_ref, lstep_ref, bre_ref, bim_ref, cre_ref, cim_ref,
                     a_ref, bblk_ref, cblk_ref):
    lre = lre_ref[0]
    lim = lim_ref[0]
    dt = jnp.exp(lstep_ref[0])
    mag = jnp.exp(lre * dt)
    ar = mag * jnp.cos(lim * dt)
    ai = mag * jnp.sin(lim * dt)
    den = lre * lre + lim * lim
    fr = ((ar - 1.0) * lre + ai * lim) / den
    fi = (ai * lre - (ar - 1.0) * lim) / den
    a_ref[0, 0:1, :] = ar
    a_ref[0, 1:2, :] = ai
    br = bre_ref[0]
    bi = bim_ref[0]
    bbr = fr * br - fi * bi
    bbi = fr * bi + fi * br
    row_g = lax.broadcasted_iota(jnp.int32, (LANES, BLOCK_MODES), 0) // SSM_GROUP
    col_g = lax.broadcasted_iota(jnp.int32, (LANES, BLOCK_MODES), 1) // SSM_STATE
    keep_b = row_g == col_g
    row_g2 = lax.broadcasted_iota(jnp.int32, (BLOCK_MODES, LANES), 0) // SSM_STATE
    col_g2 = lax.broadcasted_iota(jnp.int32, (BLOCK_MODES, LANES), 1) // SSM_GROUP
    keep_c = row_g2 == col_g2
    for k in range(SSM_BLOCKS):
        sl = slice(k * BLOCK_MODES, (k + 1) * BLOCK_MODES)
        bblk_ref[0, k, :, 0:BLOCK_MODES] = jnp.where(keep_b, bbr[:, sl], 0.0).astype(BF16)
        bblk_ref[0, k, :, BLOCK_MODES:BLOCK_STATE] = jnp.where(keep_b, bbi[:, sl], 0.0).astype(BF16)
        cblk_ref[0, k, 0:BLOCK_MODES, :] = jnp.where(keep_c, cre_ref[0, sl, :], 0.0).astype(BF16)
        cblk_ref[0, k, BLOCK_MODES:BLOCK_STATE, :] = jnp.where(keep_c, -cim_ref[0, sl, :], 0.0).astype(BF16)


def _ssm_prep(lre, lim, lstep, bre_t, bim_t, cre_t, cim_t):
    nd = lre.shape[0]
    vec = pl.BlockSpec((1, 1, SSM_MODES), lambda d: (d, 0, 0))
    bsp = pl.BlockSpec((1, LANES, SSM_MODES), lambda d: (d, 0, 0))
    csp = pl.BlockSpec((1, SSM_MODES, LANES), lambda d: (d, 0, 0))
    est = 2 * (4 * LANES * SSM_MODES * 4 + 2 * SSM_BLOCKS * LANES * BLOCK_STATE * 2) + 8 * LANES * SSM_MODES * 4
    return pl.pallas_call(
        _ssm_prep_kernel,
        grid=(nd,),
        in_specs=[vec, vec, vec, bsp, bsp, csp, csp],
        out_specs=[pl.BlockSpec((1, 2, SSM_MODES), lambda d: (d, 0, 0)),
                   pl.BlockSpec((1, SSM_BLOCKS, LANES, BLOCK_STATE), lambda d: (d, 0, 0, 0)),
                   pl.BlockSpec((1, SSM_BLOCKS, BLOCK_STATE, LANES), lambda d: (d, 0, 0, 0))],
        out_shape=[jax.ShapeDtypeStruct((nd, 2, SSM_MODES), F32),
                   jax.ShapeDtypeStruct((nd, SSM_BLOCKS, LANES, BLOCK_STATE), BF16),
                   jax.ShapeDtypeStruct((nd, SSM_BLOCKS, BLOCK_STATE, LANES), BF16)],
        compiler_params=pltpu.CompilerParams(dimension_semantics=("arbitrary",),
                                             vmem_limit_bytes=_vmem_limit(est)),
        name="ssm_prep",
    )(lre, lim, lstep, bre_t, bim_t, cre_t, cim_t)


def _ssm_scan_kernel(*refs, reverse, batch, ctx_chunks):
    if reverse:
        uc_ref, ul_ref, a_ref, bblk_ref, cblk_ref, yf_ref, d_ref, y_ref, s_ref, hb_ref, h_ref = refs
    else:
        uc_ref, ul_ref, a_ref, bblk_ref, cblk_ref, y_ref, s_ref, hb_ref, h_ref = refs
    steps = ul_ref.shape[1]
    rows = steps * batch
    step = pl.program_id(0)

    @pl.when(step == 0)
    def _():
        h_ref[...] = jnp.zeros_like(h_ref)

    u_bt = jnp.where(step < ctx_chunks, uc_ref[...], ul_ref[...]).astype(F32)
    u32 = jnp.swapaxes(u_bt, 0, 1).reshape(rows, SSM_WIDTH)
    u = u32.astype(BF16)
    for k in range(SSM_BLOCKS):
        s_ref[:, k * BLOCK_STATE:(k + 1) * BLOCK_STATE] = jnp.dot(
            u[:, k * LANES:(k + 1) * LANES], bblk_ref[0, k], preferred_element_type=F32)

    for k in range(SSM_BLOCKS):
        c_re = k * BLOCK_STATE
        c_im = c_re + BLOCK_MODES
        ar = jnp.broadcast_to(a_ref[0, 0:1, k * BLOCK_MODES:(k + 1) * BLOCK_MODES], (batch, BLOCK_MODES))
        ai = jnp.broadcast_to(a_ref[0, 1:2, k * BLOCK_MODES:(k + 1) * BLOCK_MODES], (batch, BLOCK_MODES))
        hr = h_ref[:, c_re:c_re + BLOCK_MODES]
        hi = h_ref[:, c_im:c_im + BLOCK_MODES]
        for i in range(0, steps, 2):
            pair = []
            for t in ((steps - 1 - i, steps - 2 - i) if reverse else (i, i + 1)):
                r0 = t * batch
                nhr = ar * hr - ai * hi + s_ref[r0:r0 + batch, c_re:c_re + BLOCK_MODES]
                nhi = ar * hi + ai * hr + s_ref[r0:r0 + batch, c_im:c_im + BLOCK_MODES]
                pair.append((nhr, nhi))
                hr, hi = nhr, nhi
            lo, hi2 = (pair[1], pair[0]) if reverse else (pair[0], pair[1])
            p0 = min(steps - 2 - i, steps - 1 - i) * batch if reverse else i * batch
            hb_ref[p0:p0 + 2 * batch, c_re:c_re + BLOCK_MODES] = jnp.concatenate([lo[0], hi2[0]], axis=0).astype(BF16)
            hb_ref[p0:p0 + 2 * batch, c_im:c_im + BLOCK_MODES] = jnp.concatenate([lo[1], hi2[1]], axis=0).astype(BF16)
        h_ref[:, c_re:c_re + BLOCK_MODES] = hr
        h_ref[:, c_im:c_im + BLOCK_MODES] = hi

    ys = []
    for k in range(SSM_BLOCKS):
        yk = jnp.dot(hb_ref[:, k * BLOCK_STATE:(k + 1) * BLOCK_STATE], cblk_ref[0, k],
                     preferred_element_type=F32)
        cols = slice(k * LANES, (k + 1) * LANES)
        if reverse:
            yk = yk + yf_ref[:, cols] + d_ref[:, cols] * u32[:, cols]
        ys.append(yk)
    y = jnp.concatenate(ys, axis=1)
    if reverse:
        y_ref[...] = jnp.swapaxes(y.reshape(steps, batch, SSM_WIDTH), 0, 1)
    else:
        y_ref[...] = y


def _ssm_scan(u_ctx, u_lat, u_ctx_col, u_lat_col, a, bblk, cblk, direction, yf=None, dvec=None):
    batch = u_lat.shape[0]
    rows = SSM_CHUNK * batch
    cc = u_ctx.shape[1] // SSM_CHUNK
    lc = u_lat.shape[1] // SSM_CHUNK
    reverse = direction == 1
    if reverse:
        cidx = lambda s: jnp.maximum(cc - 1 - s, 0)
        lidx = lambda s: jnp.minimum(lc - 1, lc - 1 + cc - s)
    else:
        cidx = lambda s: jnp.minimum(s, cc - 1)
        lidx = lambda s: jnp.maximum(s - cc, 0)
    in_specs = [pl.BlockSpec((batch, SSM_CHUNK, SSM_WIDTH), lambda s: (0, cidx(s), u_ctx_col)),
                pl.BlockSpec((batch, SSM_CHUNK, SSM_WIDTH), lambda s: (0, lidx(s), u_lat_col)),
                pl.BlockSpec((1, 2, SSM_MODES), lambda s: (direction, 0, 0)),
                pl.BlockSpec((1, SSM_BLOCKS, LANES, BLOCK_STATE), lambda s: (direction, 0, 0, 0)),
                pl.BlockSpec((1, SSM_BLOCKS, BLOCK_STATE, LANES), lambda s: (direction, 0, 0, 0))]
    args = [u_ctx, u_lat, a, bblk, cblk]
    if reverse:
        in_specs += [pl.BlockSpec((rows, SSM_WIDTH), lambda s: (lidx(s), 0)),
                     pl.BlockSpec((1, SSM_WIDTH), lambda s: (0, 0))]
        args += [yf, dvec]
        out_spec = pl.BlockSpec((batch, SSM_CHUNK, SSM_WIDTH), lambda s: (0, lidx(s), 0))
        out_shape = jax.ShapeDtypeStruct((batch, lc * SSM_CHUNK, SSM_WIDTH), F32)
    else:
        out_spec = pl.BlockSpec((rows, SSM_WIDTH), lambda s: (lidx(s), 0))
        out_shape = jax.ShapeDtypeStruct((lc * rows, SSM_WIDTH), F32)
    state_w = SSM_BLOCKS * BLOCK_STATE
    est = (rows * state_w * 6 + 2 * rows * state_w // SSM_BLOCKS * 4 + 10 * rows * SSM_WIDTH * 4
           + 4 * SSM_BLOCKS * LANES * BLOCK_STATE * 2)
    return pl.pallas_call(
        functools.partial(_ssm_scan_kernel, reverse=reverse, batch=batch, ctx_chunks=cc),
        grid=(cc + lc,),
        in_specs=in_specs,
        out_specs=out_spec,
        out_shape=out_shape,
        scratch_shapes=[pltpu.VMEM((rows, state_w), F32), pltpu.VMEM((rows, state_w), BF16),
                        pltpu.VMEM((batch, state_w), F32)],
        compiler_params=pltpu.CompilerParams(dimension_semantics=("arbitrary",),
                                             vmem_limit_bytes=_vmem_limit(est)),
        name="ssm_scan_bwd" if reverse else "ssm_scan_fwd",
    )(*args)


def _gelu_tanh(v):
    return 0.5 * v * (1.0 + jnp.tanh(math.sqrt(2.0 / math.pi) * (v + 0.044715 * v * v * v)))


def _merge_kernel(a_ref, y_ref, gs_ref, gma0_ref, gma1_ref, gms0_ref, gms1_ref, x_ref, gate_ref,
                  wglu_ref, bglu_ref, wpa_ref, wps_ref, wout_ref, fg_ref, o_ref):
    yg = _gelu_tanh(y_ref[...])
    z = yg * _sigmoid(jnp.dot(yg.astype(BF16), wglu_ref[...], preferred_element_type=F32) + bglu_ref[...])
    s_br = (z * _silu(gs_ref[...].astype(F32))).astype(BF16)
    ta = jnp.dot(a_ref[...], wpa_ref[...], preferred_element_type=F32)
    ts = jnp.dot(s_br, wps_ref[...], preferred_element_type=F32)
    half = D_MODEL // 2
    parts = []
    for p, (ga_r, gs_r) in enumerate(((gma0_ref, gms0_ref), (gma1_ref, gms1_ref))):
        cols = slice(p * half, (p + 1) * half)
        parts.append((_sigmoid(ga_r[...].astype(F32)) * ta[:, cols]
                      + _sigmoid(gs_r[...].astype(F32)) * ts[:, cols]).astype(BF16))
    t = jnp.concatenate(parts, axis=1)
    out = jnp.dot(t, wout_ref[...], preferred_element_type=F32)
    xn = x_ref[...] + gate_ref[0] * out
    ms = jnp.mean(xn * xn, axis=-1, keepdims=True)
    o_ref[...] = xn * lax.rsqrt(ms + NORM_EPS) * fg_ref[...]


def _merge_out(a_br, y_bm, qkv, x2d, gate, w_glu, b_glu, w_pa, w_ps, w_out, final_g, seq, tm=256):
    m, d = x2d.shape
    half = d // 2
    per_batch = seq // tm
    gs_col = (4 * ATT_WIDTH + SSM_WIDTH) // SSM_WIDTH
    gm_col = (4 * ATT_WIDTH + 2 * SSM_WIDTH) // half
    const = lambda i: (0, 0)
    est = (2 * (tm * ATT_WIDTH * 2 + tm * SSM_WIDTH * 6 + 4 * tm * half * 2 + 2 * tm * d * 4)
           + 2 * (SSM_WIDTH * SSM_WIDTH + ATT_WIDTH * d + SSM_WIDTH * d + d * d) * 2 + 6 * tm * d * 4)
    return pl.pallas_call(
        _merge_kernel,
        grid=(m // tm,),
        in_specs=[pl.BlockSpec((tm, ATT_WIDTH), lambda i: (i, 0)),
                  pl.BlockSpec((tm, SSM_WIDTH), lambda i: (i, 0)),
                  pl.BlockSpec((tm, SSM_WIDTH), lambda i: (i, gs_col)),
                  pl.BlockSpec((tm, half), lambda i: (i, gm_col)),
                  pl.BlockSpec((tm, half), lambda i: (i, gm_col + 1)),
                  pl.BlockSpec((tm, half), lambda i: (i, gm_col + 2)),
                  pl.BlockSpec((tm, half), lambda i: (i, gm_col + 3)),
                  pl.BlockSpec((tm, d), lambda i: (i, 0)),
                  pl.BlockSpec((1, 1, d), lambda i: (i // per_batch, 0, 0)),
                  pl.BlockSpec((SSM_WIDTH, SSM_WIDTH), const),
                  pl.BlockSpec((1, SSM_WIDTH), const),
                  pl.BlockSpec((ATT_WIDTH, d), const),
                  pl.BlockSpec((SSM_WIDTH, d), const),
                  pl.BlockSpec((d, d), const),
                  pl.BlockSpec((1, d), const)],
        out_specs=pl.BlockSpec((tm, d), lambda i: (i, 0)),
        out_shape=jax.ShapeDtypeStruct((m, d), F32),
        compiler_params=pltpu.CompilerParams(dimension_semantics=("arbitrary",),
                                             vmem_limit_bytes=_vmem_limit(est)),
        name="merge_out",
    )(a_br, y_bm, qkv, qkv, qkv, qkv, qkv, x2d, gate, w_glu, b_glu, w_pa, w_ps, w_out, final_g)


def kernel(x, c, ctx, c_ctx, w_ada, b_ada, norm_g, w_in, lambda_q1, lambda_k1, lambda_q2, lambda_k2,
           subln_g, ssm_lambda_re, ssm_lambda_im, ssm_log_step, ssm_b_re, ssm_b_im, ssm_c_re, ssm_c_im,
           ssm_d, w_glu, b_glu, w_pa, w_ps, w_out, final_g):
    batch, seq, d = x.shape
    ctx_len = ctx.shape[1]
    assert w_ada.shape[0] == 1, "single-layer trunk"
    assert (d, seq % 1024, ctx_len % SSM_CHUNK, seq % SSM_CHUNK) == (D_MODEL, 0, 0, 0)

    pad = (-(batch + 1)) % SUBLANES
    cc = jnp.concatenate([c, c_ctx[None, :], jnp.zeros((pad, d), F32)], axis=0)
    mod = _adaln(cc, w_ada[0], b_ada[0][None, :])
    shift, scale, gate = (mod[:, k * d:(k + 1) * d] for k in range(3))
    as_mod = lambda v, lo, hi: v[lo:hi].reshape(hi - lo, 1, d)

    w_in_bf = w_in[0].astype(BF16)
    x2d = x.reshape(batch * seq, d)
    ctx2d = ctx.reshape(batch * ctx_len, d)
    ng = norm_g[0][None, :]

    tn = 1024
    u_col = 4 * ATT_WIDTH
    qkv = _in_proj_latent(x2d, as_mod(scale, 0, batch), as_mod(shift, 0, batch), ng, w_in_bf,
                          tn, seq, _rope_tables(seq))
    tnc = SSM_WIDTH
    k0 = ATT_WIDTH // tnc
    ctx_cols = list(range(k0, 3 * k0)) + [u_col // tnc]
    kvc = _in_proj_ctx(ctx2d, as_mod(scale, batch, batch + 1), as_mod(shift, batch, batch + 1), ng, w_in_bf,
                       ctx_cols, tnc)

    lam_vecs = jnp.stack([lambda_q1[0], lambda_k1[0], lambda_q2[0], lambda_k2[0]]).astype(F32)
    a_br = _attention(qkv, kvc, lam_vecs, subln_g[0].astype(F32), batch, seq, ctx_len)

    nd = ssm_lambda_re.shape[1]
    modes = lambda v: v[0].reshape(nd, 1, SSM_MODES)
    lstep = jnp.repeat(ssm_log_step[0], SSM_STATE, axis=-1).reshape(nd, 1, SSM_MODES)
    b_t = lambda v: jnp.tile(v[0].reshape(nd, SSM_MODES, SSM_GROUP).transpose(0, 2, 1), (1, GROUPS_PER_BLOCK, 1))
    c_t = lambda v: jnp.tile(v[0].transpose(0, 1, 3, 2).reshape(nd, SSM_MODES, SSM_GROUP), (1, 1, GROUPS_PER_BLOCK))
    a_disc, bblk, cblk = _ssm_prep(modes(ssm_lambda_re), modes(ssm_lambda_im), lstep,
                                   b_t(ssm_b_re), b_t(ssm_b_im), c_t(ssm_c_re), c_t(ssm_c_im))

    qkv3 = qkv.reshape(batch, seq, IN_WIDTH)
    kvc3 = kvc.reshape(batch, ctx_len, len(ctx_cols) * tnc)
    scan = functools.partial(_ssm_scan, kvc3, qkv3, len(ctx_cols) - 1, u_col // SSM_WIDTH, a_disc, bblk, cblk)
    yf = scan(0)
    y_bm = scan(1, yf=yf, dvec=ssm_d[0].reshape(1, SSM_WIDTH).astype(F32))

    out = _merge_out(a_br, y_bm.reshape(batch * seq, SSM_WIDTH), qkv, x2d, as_mod(gate, 0, batch),
                     w_glu[0].astype(BF16), b_glu[0][None, :], w_pa[0].astype(BF16), w_ps[0].astype(BF16),
                     w_out[0].astype(BF16), final_g[None, :], seq)
    return out.reshape(batch, seq, d)
```

```python
import functools
import math

import jax
import jax.numpy as jnp
import numpy as np
from jax import lax
from jax.experimental import pallas as pl
from jax.experimental.pallas import tpu as pltpu

F32 = jnp.float32
BF16 = jnp.bfloat16

D_MODEL = 2048
GRID_W = 64
ATT_HEADS = 8
ATT_DIM = 64
HEAD_W = 2 * ATT_DIM
ATT_WIDTH = ATT_HEADS * HEAD_W
SSM_GROUP = 16
SSM_GROUPS = 32
SSM_WIDTH = SSM_GROUP * SSM_GROUPS
SSM_STATE = 64
SSM_MODES = SSM_GROUPS * SSM_STATE
IN_WIDTH = 4 * ATT_WIDTH + 2 * SSM_WIDTH + 2 * D_MODEL
ROPE_BASE = 10000.0
NORM_EPS = 1e-6
LAM_INIT = 0.8 - 0.6 * math.exp(0.0)

LANES = 128
SUBLANES = 8
VMEM_BYTES_V7X = 64 * 1024 * 1024
MIB = 1024 * 1024

GROUPS_PER_BLOCK = LANES // SSM_GROUP
SSM_BLOCKS = SSM_GROUPS // GROUPS_PER_BLOCK
BLOCK_MODES = GROUPS_PER_BLOCK * SSM_STATE
BLOCK_STATE = 2 * BLOCK_MODES
SSM_CHUNK = 128
KEY_CHUNK = 256
VT_PAD = 16


def _vmem_limit(nbytes):
    return int(min(nbytes + nbytes // 4, VMEM_BYTES_V7X - 8 * MIB))


def _sigmoid(v):
    return 1.0 / (1.0 + jnp.exp(-v))


def _silu(v):
    return v * _sigmoid(v)


def _adaln_kernel(c_ref, w_ref, b_ref, o_ref):
    o_ref[...] = jnp.dot(_silu(c_ref[...]).astype(BF16), w_ref[...].astype(BF16),
                         preferred_element_type=F32) + b_ref[...]


def _adaln(cc, w_ada, b_ada):
    rows, d = cc.shape
    n = w_ada.shape[1]
    tn = 768
    est = 2 * (d * tn * 4) + 2 * rows * tn * 4 + rows * d * 4 * 2
    return pl.pallas_call(
        _adaln_kernel,
        grid=(n // tn,),
        in_specs=[pl.BlockSpec((rows, d), lambda j: (0, 0)),
                  pl.BlockSpec((d, tn), lambda j: (0, j)),
                  pl.BlockSpec((1, tn), lambda j: (0, j))],
        out_specs=pl.BlockSpec((rows, tn), lambda j: (0, j)),
        out_shape=jax.ShapeDtypeStruct((rows, n), F32),
        compiler_params=pltpu.CompilerParams(dimension_semantics=("arbitrary",),
                                             vmem_limit_bytes=_vmem_limit(est)),
        name="adaln",
    )(cc, w_ada, b_ada)


def _in_proj_ctx_kernel(x_ref, scale_ref, shift_ref, g_ref, w_ref, o_ref, h_ref, *, norm_rows):
    j = pl.program_id(1)
    tm = x_ref.shape[0]

    @pl.when(j == 0)
    def _():
        gmod = g_ref[...] * (1.0 + scale_ref[0])
        shift = shift_ref[0]

        def body(r, carry):
            r0 = pl.multiple_of(r * norm_rows, norm_rows)
            xf = x_ref[pl.ds(r0, norm_rows), :]
            ms = jnp.mean(xf * xf, axis=-1, keepdims=True)
            h_ref[pl.ds(r0, norm_rows), :] = (xf * lax.rsqrt(ms + NORM_EPS) * gmod + shift).astype(BF16)
            return carry

        lax.fori_loop(0, tm // norm_rows, body, 0)

    o_ref[...] = jnp.dot(h_ref[...], w_ref[...], preferred_element_type=F32).astype(o_ref.dtype)


def _in_proj_lat_kernel(x_ref, scale_ref, shift_ref, nscale_ref, nshift_ref, g_ref, w_ref,
                        cos_ref, sina_ref, sinb_ref, o_ref, h_ref, *, norm_rows, fill_rows):
    i = pl.program_id(0)
    j = pl.program_id(1)
    tm = x_ref.shape[0]

    def normalise(r0, nrows, gmod, shift, slot):
        xf = x_ref[pl.ds(r0, nrows), :]
        ms = jnp.mean(xf * xf, axis=-1, keepdims=True)
        h_ref[slot, pl.ds(r0, nrows), :] = (xf * lax.rsqrt(ms + NORM_EPS) * gmod + shift).astype(BF16)

    @pl.when((i == 0) & (j == 0))
    def _():
        gmod = g_ref[...] * (1.0 + scale_ref[0])
        shift = shift_ref[0]

        def body(r, carry):
            normalise(pl.multiple_of(r * norm_rows, norm_rows), norm_rows, gmod, shift, 0)
            return carry

        lax.fori_loop(0, tm // norm_rows, body, 0)

    def step(cur, rotary):
        acc = jnp.dot(h_ref[cur], w_ref[...], preferred_element_type=F32)
        if rotary:
            qscale = jnp.where(j == 0, ATT_DIM ** -0.5 * math.log2(math.e), 1.0).astype(F32)
            cos = (cos_ref[...] * qscale).astype(BF16)
            sina = (sina_ref[...] * qscale).astype(BF16)
            sinb = (sinb_ref[...] * qscale).astype(BF16)
            for cb in range(acc.shape[1] // HEAD_W):
                v = acc[:, cb * HEAD_W:(cb + 1) * HEAD_W].astype(BF16)
                up = pltpu.roll(v, HEAD_W - ATT_DIM // 4, 1)
                dn = pltpu.roll(v, ATT_DIM // 4, 1)
                o_ref[:, cb * HEAD_W:(cb + 1) * HEAD_W] = v * cos + up * sina + dn * sinb
        else:
            o_ref[...] = acc.astype(o_ref.dtype)
        r0 = pl.multiple_of(jnp.maximum(j - 1, 0) * fill_rows, fill_rows)
        normalise(r0, fill_rows, g_ref[...] * (1.0 + nscale_ref[0]), nshift_ref[0], 1 - cur)

    for par in range(2):
        for rotary in (True, False):
            pl.when((i % 2 == par) & ((j < 2) == rotary))(functools.partial(step, par, rotary))


def _in_proj_latent(x2d, scale, shift, norm_g, w_bf, tn, seq_len, rope_tables, tm=1024):
    m, d = x2d.shape
    nj = w_bf.shape[1] // tn
    nblk = m // tm
    seq_blocks = seq_len // tm
    assert tm % (nj - 1) == 0
    nxt = lambda i: jnp.minimum(i + 1, nblk - 1)
    in_specs = [pl.BlockSpec((tm, d), lambda i, j: (jnp.minimum(i + jnp.minimum(j, 1), nblk - 1), 0)),
                pl.BlockSpec((1, 1, d), lambda i, j: (i // seq_blocks, 0, 0)),
                pl.BlockSpec((1, 1, d), lambda i, j: (i // seq_blocks, 0, 0)),
                pl.BlockSpec((1, 1, d), lambda i, j: (nxt(i) // seq_blocks, 0, 0)),
                pl.BlockSpec((1, 1, d), lambda i, j: (nxt(i) // seq_blocks, 0, 0)),
                pl.BlockSpec((1, d), lambda i, j: (0, 0)),
                pl.BlockSpec((d, tn), lambda i, j: (0, j))]
    in_specs += [pl.BlockSpec((tm, HEAD_W), lambda i, j: (i % seq_blocks, 0)) for _ in rope_tables]
    est = (2 * tm * d * 4 + 2 * tm * d * 2 + 2 * d * tn * 2 + 2 * tm * tn * 2 + 2 * tm * tn * 4
           + 6 * tm * HEAD_W * 4)
    return pl.pallas_call(
        functools.partial(_in_proj_lat_kernel, norm_rows=256, fill_rows=tm // (nj - 1)),
        grid=(nblk, nj),
        in_specs=in_specs,
        out_specs=pl.BlockSpec((tm, tn), lambda i, j: (i, j)),
        out_shape=jax.ShapeDtypeStruct((m, nj * tn), BF16),
        scratch_shapes=[pltpu.VMEM((2, tm, d), BF16)],
        compiler_params=pltpu.CompilerParams(dimension_semantics=("arbitrary", "arbitrary"),
                                             vmem_limit_bytes=_vmem_limit(est)),
        name="in_proj_rope",
    )(x2d, scale, shift, scale, shift, norm_g, w_bf, *rope_tables)


def _in_proj_ctx(x2d, scale, shift, norm_g, w_bf, col_blocks, tn, tm=1024):
    m, d = x2d.shape
    nj = len(col_blocks)

    def w_map(i, j):
        idx = jnp.int32(col_blocks[-1])
        for k in range(nj - 2, -1, -1):
            idx = jnp.where(j == k, jnp.int32(col_blocks[k]), idx)
        return (0, idx)

    est = 2 * tm * d * 4 + tm * d * 2 + 2 * d * tn * 2 + 2 * tm * tn * 2 + 2 * tm * tn * 4
    return pl.pallas_call(
        functools.partial(_in_proj_ctx_kernel, norm_rows=256),
        grid=(m // tm, nj),
        in_specs=[pl.BlockSpec((tm, d), lambda i, j: (i, 0)),
                  pl.BlockSpec((1, 1, d), lambda i, j: (0, 0, 0)),
                  pl.BlockSpec((1, 1, d), lambda i, j: (0, 0, 0)),
                  pl.BlockSpec((1, d), lambda i, j: (0, 0)),
                  pl.BlockSpec((d, tn), w_map)],
        out_specs=pl.BlockSpec((tm, tn), lambda i, j: (i, j)),
        out_shape=jax.ShapeDtypeStruct((m, nj * tn), BF16),
        scratch_shapes=[pltpu.VMEM((tm, d), BF16)],
        compiler_params=pltpu.CompilerParams(dimension_semantics=("arbitrary", "arbitrary"),
                                             vmem_limit_bytes=_vmem_limit(est)),
        name="in_proj_ctx",
    )(x2d, scale, shift, norm_g, w_bf)


def _rope_tables(seq):
    rows = seq // GRID_W
    row = np.repeat(np.arange(rows), GRID_W).astype(np.float64)
    col = np.tile(np.arange(GRID_W), rows).astype(np.float64)
    nf = ATT_DIM // 4
    inv = ROPE_BASE ** (-np.arange(nf, dtype=np.float64) / nf)

    def axis_tables(pos):
        ang = pos[:, None] * inv[None, :]
        c, s = np.cos(ang), np.sin(ang)
        z = np.zeros_like(s)
        return (np.concatenate([c, c], -1), np.concatenate([-s, z], -1), np.concatenate([z, s], -1))

    parts = [axis_tables(row), axis_tables(col)]
    out = []
    for k in range(3):
        comp = np.concatenate([parts[0][k], parts[1][k]], -1)
        out.append(jnp.asarray(np.concatenate([comp, comp], -1), dtype=F32))
    return out


def _attn_kernel(q_ref, k_ref, v_ref, kc_ref, vc_ref, ga_ref, lam_ref, gcol_ref, o_ref,
                 qt_scr, kall_scr, vt_scr, s_scr, *, rows):
    seq = k_ref.shape[0]
    keys = seq + kc_ref.shape[0]
    nblk = q_ref.shape[0] // rows
    nch = keys // KEY_CHUNK
    sub = KEY_CHUNK // SUBLANES
    lamv = lam_ref[...]
    lam = (jnp.exp(jnp.sum(lamv[0:1] * lamv[1:2], axis=-1, keepdims=True))
           - jnp.exp(jnp.sum(lamv[2:3] * lamv[3:4], axis=-1, keepdims=True)) + LAM_INIT)

    kall_scr[0:seq, :] = k_ref[...]
    kall_scr[seq:keys, :] = kc_ref[...]
    vt_scr[0:HEAD_W, 0:seq] = v_ref[...].T
    vt_scr[0:HEAD_W, seq:keys] = vc_ref[...].T
    pad_row = lax.broadcasted_iota(jnp.int32, (VT_PAD, keys), 0)
    vt_scr[HEAD_W:HEAD_W + VT_PAD, :] = jnp.where(pad_row == 0, 1.0, 0.0).astype(BF16)
    drow = lax.broadcasted_iota(jnp.int32, (HEAD_W, rows), 0)
    for i in range(nblk):
        qt = q_ref[i * rows:(i + 1) * rows, :].T
        qt_scr[0, i] = jnp.where(drow < ATT_DIM, qt, jnp.zeros_like(qt))
        qt_scr[1, i] = jnp.where(drow >= ATT_DIM, qt, jnp.zeros_like(qt))

    def colmax(x):
        return jnp.max(x.reshape(sub, SUBLANES, rows), axis=0)

    def step(t, slot, do_qk, do_sm, mfin):
        mpart = [None, None]
        acc = [None, None]
        for c in range(nch):
            ks = slice(c * KEY_CHUNK, (c + 1) * KEY_CHUNK)
            if do_qk:
                kc = kall_scr[ks, :]
                for comp in range(2):
                    s = jnp.dot(kc, qt_scr[comp, t], preferred_element_type=F32)
                    s_scr[slot, comp, c] = s
                    cm = colmax(s)
                    mpart[comp] = cm if mpart[comp] is None else jnp.maximum(mpart[comp], cm)
            if do_sm:
                vt = vt_scr[:, ks]
                for comp in range(2):
                    p = jnp.exp2(s_scr[1 - slot, comp, c] - mfin[comp]).astype(BF16)
                    d = jnp.dot(vt, p, preferred_element_type=F32)
                    acc[comp] = d if acc[comp] is None else acc[comp] + d
        if do_sm:
            r0 = pl.multiple_of((t - 1) * rows, rows)
            l1 = acc[0][HEAD_W:HEAD_W + 1]
            l2 = acc[1][HEAD_W:HEAD_W + 1]
            o = acc[0][0:HEAD_W] * (1.0 / l1) - acc[1][0:HEAD_W] * (lam / l2)
            ms = jnp.mean(o * o, axis=0, keepdims=True)
            on = (o * lax.rsqrt(ms + NORM_EPS) * gcol_ref[...] * (1.0 - LAM_INIT)).T
            o_ref[pl.ds(r0, rows), :] = (on * _silu(ga_ref[pl.ds(r0, rows), :].astype(F32))).astype(o_ref.dtype)
        return [jnp.max(m, axis=0, keepdims=True) for m in mpart] if do_qk else mfin

    m = step(0, 0, True, False, None)

    def body(tt, carry):
        t = 2 * tt + 1
        mm = step(t, 1, True, True, list(carry))
        mm = step(t + 1, 0, True, True, mm)
        return mm[0], mm[1]

    m = lax.fori_loop(0, (nblk - 2) // 2, body, (m[0], m[1]))
    m = step(nblk - 1, 1, True, True, list(m))
    step(nblk, 0, False, True, m)


def _attention(qkv, kvc, lam_vecs, subln_g, batch, seq, ctx_len, rows=256):
    kcol = ATT_WIDTH // HEAD_W
    keys = seq + ctx_len
    nch = keys // KEY_CHUNK
    assert keys % KEY_CHUNK == 0 and seq % (2 * rows) == 0
    gcol = jnp.broadcast_to(subln_g.reshape(HEAD_W, 1), (HEAD_W, rows))
    est = (2 * 2 * keys * rows * 4 + 2 * seq * HEAD_W * 2 + 2 * keys * HEAD_W * 2
           + 2 * (4 * seq + 2 * ctx_len) * HEAD_W * 2 + 64 * KEY_CHUNK * rows * 4)
    return pl.pallas_call(
        functools.partial(_attn_kernel, rows=rows),
        grid=(batch, ATT_HEADS),
        in_specs=[pl.BlockSpec((seq, HEAD_W), lambda b, h: (b, h)),
                  pl.BlockSpec((seq, HEAD_W), lambda b, h: (b, kcol + h)),
                  pl.BlockSpec((seq, HEAD_W), lambda b, h: (b, 2 * kcol + h)),
                  pl.BlockSpec((ctx_len, HEAD_W), lambda b, h: (b, h)),
                  pl.BlockSpec((ctx_len, HEAD_W), lambda b, h: (b, kcol + h)),
                  pl.BlockSpec((seq, HEAD_W), lambda b, h: (b, 3 * kcol + h)),
                  pl.BlockSpec((4, ATT_DIM), lambda b, h: (0, 0)),
                  pl.BlockSpec((HEAD_W, rows), lambda b, h: (0, 0))],
        out_specs=pl.BlockSpec((seq, HEAD_W), lambda b, h: (b, h)),
        out_shape=jax.ShapeDtypeStruct((batch * seq, ATT_WIDTH), BF16),
        scratch_shapes=[pltpu.VMEM((2, seq // rows, HEAD_W, rows), BF16),
                        pltpu.VMEM((keys, HEAD_W), BF16),
                        pltpu.VMEM((HEAD_W + VT_PAD, keys), BF16),
                        pltpu.VMEM((2, 2, nch, KEY_CHUNK, rows), F32)],
        compiler_params=pltpu.CompilerParams(dimension_semantics=("arbitrary",) * 2,
                                             vmem_limit_bytes=_vmem_limit(est)),
        name="diff_attention",
    )(qkv, qkv, qkv, kvc, kvc, qkv, lam_vecs, gcol)


def _ssm_prep_kernel(lre_ref, lim_ref, lstep_ref, bre_ref, bim_ref, cre_ref, cim_ref,
                     a_ref, bblk_ref, cblk_ref):
    lre = lre_ref[0]
    lim = lim_ref[0]
    dt = jnp.exp(lstep_ref[0])
    mag = jnp.exp(lre * dt)
    ar = mag * jnp.cos(lim * dt)
    ai = mag * jnp.sin(lim * dt)
    den = lre * lre + lim * lim
    fr = ((ar - 1.0) * lre + ai * lim) / den
    fi = (ai * lre - (ar - 1.0) * lim) / den
    a_ref[0, 0:1, :] = ar
    a_ref[0, 1:2, :] = ai
    br = bre_ref[0]
    bi = bim_ref[0]
    bbr = fr * br - fi * bi
    bbi = fr * bi + fi * br
    row_g = lax.broadcasted_iota(jnp.int32, (LANES, BLOCK_MODES), 0) // SSM_GROUP
    col_g = lax.broadcasted_iota(jnp.int32, (LANES, BLOCK_MODES), 1) // SSM_STATE
    keep_b = row_g == col_g
    row_g2 = lax.broadcasted_iota(jnp.int32, (BLOCK_MODES, LANES), 0) // SSM_STATE
    col_g2 = lax.broadcasted_iota(jnp.int32, (BLOCK_MODES, LANES), 1) // SSM_GROUP
    keep_c = row_g2 == col_g2
    for k in range(SSM_BLOCKS):
        sl = slice(k * BLOCK_MODES, (k + 1) * BLOCK_MODES)
        bblk_ref[0, k, :, 0:BLOCK_MODES] = jnp.where(keep_b, bbr[:, sl], 0.0).astype(BF16)
        bblk_ref[0, k, :, BLOCK_MODES:BLOCK_STATE] = jnp.where(keep_b, bbi[:, sl], 0.0).astype(BF16)
        cblk_ref[0, k, 0:BLOCK_MODES, :] = jnp.where(keep_c, cre_ref[0, sl, :], 0.0).astype(BF16)
        cblk_ref[0, k, BLOCK_MODES:BLOCK_STATE, :] = jnp.where(keep_c, -cim_ref[0, sl, :], 0.0).astype(BF16)


def _ssm_prep(lre, lim, lstep, bre_t, bim_t, cre_t, cim_t):
    nd = lre.shape[0]
    vec = pl.BlockSpec((1, 1, SSM_MODES), lambda d: (d, 0, 0))
    bsp = pl.BlockSpec((1, LANES, SSM_MODES), lambda d: (d, 0, 0))
    csp = pl.BlockSpec((1, SSM_MODES, LANES), lambda d: (d, 0, 0))
    est = 2 * (4 * LANES * SSM_MODES * 4 + 2 * SSM_BLOCKS * LANES * BLOCK_STATE * 2) + 8 * LANES * SSM_MODES * 4
    return pl.pallas_call(
        _ssm_prep_kernel,
        grid=(nd,),
        in_specs=[vec, vec, vec, bsp, bsp, csp, csp],
        out_specs=[pl.BlockSpec((1, 2, SSM_MODES), lambda d: (d, 0, 0)),
                   pl.BlockSpec((1, SSM_BLOCKS, LANES, BLOCK_STATE), lambda d: (d, 0, 0, 0)),
                   pl.BlockSpec((1, SSM_BLOCKS, BLOCK_STATE, LANES), lambda d: (d, 0, 0, 0))],
        out_shape=[jax.ShapeDtypeStruct((nd, 2, SSM_MODES), F32),
                   jax.ShapeDtypeStruct((nd, SSM_BLOCKS, LANES, BLOCK_STATE), BF16),
                   jax.ShapeDtypeStruct((nd, SSM_BLOCKS, BLOCK_STATE, LANES), BF16)],
        compiler_params=pltpu.CompilerParams(dimension_semantics=("arbitrary",),
                                             vmem_limit_bytes=_vmem_limit(est)),
        name="ssm_prep",
    )(lre, lim, lstep, bre_t, bim_t, cre_t, cim_t)


def _ssm_scan_kernel(*refs, reverse, batch, ctx_chunks):
    if reverse:
        uc_ref, ul_ref, a_ref, bblk_ref, cblk_ref, yf_ref, d_ref, y_ref, s_ref, hb_ref, h_ref = refs
    else:
        uc_ref, ul_ref, a_ref, bblk_ref, cblk_ref, y_ref, s_ref, hb_ref, h_ref = refs
    steps = ul_ref.shape[1]
    rows = steps * batch
    step = pl.program_id(0)

    @pl.when(step == 0)
    def _():
        h_ref[...] = jnp.zeros_like(h_ref)

    u_bt = jnp.where(step < ctx_chunks, uc_ref[...], ul_ref[...]).astype(F32)
    u32 = jnp.swapaxes(u_bt, 0, 1).reshape(rows, SSM_WIDTH)
    u = u32.astype(BF16)
    for k in range(SSM_BLOCKS):
        s_ref[:, k * BLOCK_STATE:(k + 1) * BLOCK_STATE] = jnp.dot(
            u[:, k * LANES:(k + 1) * LANES], bblk_ref[0, k], preferred_element_type=F32)

    for k in range(SSM_BLOCKS):
        c_re = k * BLOCK_STATE
        c_im = c_re + BLOCK_MODES
        ar = jnp.broadcast_to(a_ref[0, 0:1, k * BLOCK_MODES:(k + 1) * BLOCK_MODES], (batch, BLOCK_MODES))
        ai = jnp.broadcast_to(a_ref[0, 1:2, k * BLOCK_MODES:(k + 1) * BLOCK_MODES], (batch, BLOCK_MODES))
        hr = h_ref[:, c_re:c_re + BLOCK_MODES]
        hi = h_ref[:, c_im:c_im + BLOCK_MODES]
        for i in range(0, steps, 2):
            pair = []
            for t in ((steps - 1 - i, steps - 2 - i) if reverse else (i, i + 1)):
                r0 = t * batch
                nhr = ar * hr - ai * hi + s_ref[r0:r0 + batch, c_re:c_re + BLOCK_MODES]
                nhi = ar * hi + ai * hr + s_ref[r0:r0 + batch, c_im:c_im + BLOCK_MODES]
                pair.append((nhr, nhi))
                hr, hi = nhr, nhi
            lo, hi2 = (pair[1], pair[0]) if reverse else (pair[0], pair[1])
            p0 = min(steps - 2 - i, steps - 1 - i) * batch if reverse else i * batch
            hb_ref[p0:p0 + 2 * batch, c_re:c_re + BLOCK_MODES] = jnp.concatenate([lo[0], hi2[0]], axis=0).astype(BF16)
            hb_ref[p0:p0 + 2 * batch, c_im:c_im + BLOCK_MODES] = jnp.concatenate([lo[1], hi2[1]], axis=0).astype(BF16)
        h_ref[:, c_re:c_re + BLOCK_MODES] = hr
        h_ref[:, c_im:c_im + BLOCK_MODES] = hi

    ys = []
    for k in range(SSM_BLOCKS):
        yk = jnp.dot(hb_ref[:, k * BLOCK_STATE:(k + 1) * BLOCK_STATE], cblk_ref[0, k],
                     preferred_element_type=F32)
        cols = slice(k * LANES, (k + 1) * LANES)
        if reverse:
            yk = yk + yf_ref[:, cols] + d_ref[:, cols] * u32[:, cols]
        ys.append(yk)
    y = jnp.concatenate(ys, axis=1)
    if reverse:
        y_ref[...] = jnp.swapaxes(y.reshape(steps, batch, SSM_WIDTH), 0, 1)
    else:
        y_ref[...] = y


def _ssm_scan(u_ctx, u_lat, u_ctx_col, u_lat_col, a, bblk, cblk, direction, yf=None, dvec=None):
    batch = u_lat.shape[0]
    rows = SSM_CHUNK * batch
    cc = u_ctx.shape[1] // SSM_CHUNK
    lc = u_lat.shape[1] // SSM_CHUNK
    reverse = direction == 1
    if reverse:
        cidx = lambda s: jnp.maximum(cc - 1 - s, 0)
        lidx = lambda s: jnp.minimum(lc - 1, lc - 1 + cc - s)
    else:
        cidx = lambda s: jnp.minimum(s, cc - 1)
        lidx = lambda s: jnp.maximum(s - cc, 0)
    in_specs = [pl.BlockSpec((batch, SSM_CHUNK, SSM_WIDTH), lambda s: (0, cidx(s), u_ctx_col)),
                pl.BlockSpec((batch, SSM_CHUNK, SSM_WIDTH), lambda s: (0, lidx(s), u_lat_col)),
                pl.BlockSpec((1, 2, SSM_MODES), lambda s: (direction, 0, 0)),
                pl.BlockSpec((1, SSM_BLOCKS, LANES, BLOCK_STATE), lambda s: (direction, 0, 0, 0)),
                pl.BlockSpec((1, SSM_BLOCKS, BLOCK_STATE, LANES), lambda s: (direction, 0, 0, 0))]
    args = [u_ctx, u_lat, a, bblk, cblk]
    if reverse:
        in_specs += [pl.BlockSpec((rows, SSM_WIDTH), lambda s: (lidx(s), 0)),
                     pl.BlockSpec((1, SSM_WIDTH), lambda s: (0, 0))]
        args += [yf, dvec]
        out_spec = pl.BlockSpec((batch, SSM_CHUNK, SSM_WIDTH), lambda s: (0, lidx(s), 0))
        out_shape = jax.ShapeDtypeStruct((batch, lc * SSM_CHUNK, SSM_WIDTH), F32)
    else:
        out_spec = pl.BlockSpec((rows, SSM_WIDTH), lambda s: (lidx(s), 0))
        out_shape = jax.ShapeDtypeStruct((lc * rows, SSM_WIDTH), F32)
    state_w = SSM_BLOCKS * BLOCK_STATE
    est = (rows * state_w * 6 + 2 * rows * state_w // SSM_BLOCKS * 4 + 10 * rows * SSM_WIDTH * 4
           + 4 * SSM_BLOCKS * LANES * BLOCK_STATE * 2)
    return pl.pallas_call(
        functools.partial(_ssm_scan_kernel, reverse=reverse, batch=batch, ctx_chunks=cc),
        grid=(cc + lc,),
        in_specs=in_specs,
        out_specs=out_spec,
        out_shape=out_shape,
        scratch_shapes=[pltpu.VMEM((rows, state_w), F32), pltpu.VMEM((rows, state_w), BF16),
                        pltpu.VMEM((batch, state_w), F32)],
        compiler_params=pltpu.CompilerParams(dimension_semantics=("arbitrary",),
                                             vmem_limit_bytes=_vmem_limit(est)),
        name="ssm_scan_bwd" if reverse else "ssm_scan_fwd",
    )(*args)


def _gelu_tanh(v):
    return 0.5 * v * (1.0 + jnp.tanh(math.sqrt(2.0 / math.pi) * (v + 0.044715 * v * v * v)))


def _merge_kernel(a_ref, y_ref, gs_ref, gma0_ref, gma1_ref, gms0_ref, gms1_ref, x_ref, gate_ref,
                  wglu_ref, bglu_ref, wpa_ref, wps_ref, wout_ref, fg_ref, o_ref):
    yg = _gelu_tanh(y_ref[...])
    z = yg * _sigmoid(jnp.dot(yg.astype(BF16), wglu_ref[...], preferred_element_type=F32) + bglu_ref[...])
    s_br = (z * _silu(gs_ref[...].astype(F32))).astype(BF16)
    ta = jnp.dot(a_ref[...], wpa_ref[...], preferred_element_type=F32)
    ts = jnp.dot(s_br, wps_ref[...], preferred_element_type=F32)
    half = D_MODEL // 2
    parts = []
    for p, (ga_r, gs_r) in enumerate(((gma0_ref, gms0_ref), (gma1_ref, gms1_ref))):
        cols = slice(p * half, (p + 1) * half)
        parts.append((_sigmoid(ga_r[...].astype(F32)) * ta[:, cols]
                      + _sigmoid(gs_r[...].astype(F32)) * ts[:, cols]).astype(BF16))
    t = jnp.concatenate(parts, axis=1)
    out = jnp.dot(t, wout_ref[...], preferred_element_type=F32)
    xn = x_ref[...] + gate_ref[0] * out
    ms = jnp.mean(xn * xn, axis=-1, keepdims=True)
    o_ref[...] = xn * lax.rsqrt(ms + NORM_EPS) * fg_ref[...]


def _merge_out(a_br, y_bm, qkv, x2d, gate, w_glu, b_glu, w_pa, w_ps, w_out, final_g, seq, tm=256):
    m, d = x2d.shape
    half = d // 2
    per_batch = seq // tm
    gs_col = (4 * ATT_WIDTH + SSM_WIDTH) // SSM_WIDTH
    gm_col = (4 * ATT_WIDTH + 2 * SSM_WIDTH) // half
    const = lambda i: (0, 0)
    est = (2 * (tm * ATT_WIDTH * 2 + tm * SSM_WIDTH * 6 + 4 * tm * half * 2 + 2 * tm * d * 4)
           + 2 * (SSM_WIDTH * SSM_WIDTH + ATT_WIDTH * d + SSM_WIDTH * d + d * d) * 2 + 6 * tm * d * 4)
    return pl.pallas_call(
        _merge_kernel,
        grid=(m // tm,),
        in_specs=[pl.BlockSpec((tm, ATT_WIDTH), lambda i: (i, 0)),
                  pl.BlockSpec((tm, SSM_WIDTH), lambda i: (i, 0)),
                  pl.BlockSpec((tm, SSM_WIDTH), lambda i: (i, gs_col)),
                  pl.BlockSpec((tm, half), lambda i: (i, gm_col)),
                  pl.BlockSpec((tm, half), lambda i: (i, gm_col + 1)),
                  pl.BlockSpec((tm, half), lambda i: (i, gm_col + 2)),
                  pl.BlockSpec((tm, half), lambda i: (i, gm_col + 3)),
                  pl.BlockSpec((tm, d), lambda i: (i, 0)),
                  pl.BlockSpec((1, 1, d), lambda i: (i // per_batch, 0, 0)),
                  pl.BlockSpec((SSM_WIDTH, SSM_WIDTH), const),
                  pl.BlockSpec((1, SSM_WIDTH), const),
                  pl.BlockSpec((ATT_WIDTH, d), const),
                  pl.BlockSpec((SSM_WIDTH, d), const),
                  pl.BlockSpec((d, d), const),
                  pl.BlockSpec((1, d), const)],
        out_specs=pl.BlockSpec((tm, d), lambda i: (i, 0)),
        out_shape=jax.ShapeDtypeStruct((m, d), F32),
        compiler_params=pltpu.CompilerParams(dimension_semantics=("arbitrary",),
                                             vmem_limit_bytes=_vmem_limit(est)),
        name="merge_out",
    )(a_br, y_bm, qkv, qkv, qkv, qkv, qkv, x2d, gate, w_glu, b_glu, w_pa, w_ps, w_out, final_g)


def kernel(x, c, ctx, c_ctx, w_ada, b_ada, norm_g, w_in, lambda_q1, lambda_k1, lambda_q2, lambda_k2,
           subln_g, ssm_lambda_re, ssm_lambda_im, ssm_log_step, ssm_b_re, ssm_b_im, ssm_c_re, ssm_c_im,
           ssm_d, w_glu, b_glu, w_pa, w_ps, w_out, final_g):
    batch, seq, d = x.shape
    ctx_len = ctx.shape[1]
    assert w_ada.shape[0] == 1, "single-layer trunk"
    assert (d, seq % 1024, ctx_len % SSM_CHUNK, seq % SSM_CHUNK) == (D_MODEL, 0, 0, 0)

    pad = (-(batch + 1)) % SUBLANES
    cc = jnp.concatenate([c, c_ctx[None, :], jnp.zeros((pad, d), F32)], axis=0)
    mod = _adaln(cc, w_ada[0], b_ada[0][None, :])
    shift, scale, gate = (mod[:, k * d:(k + 1) * d] for k in range(3))
    as_mod = lambda v, lo, hi: v[lo:hi].reshape(hi - lo, 1, d)

    w_in_bf = w_in[0].astype(BF16)
    x2d = x.reshape(batch * seq, d)
    ctx2d = ctx.reshape(batch * ctx_len, d)
    ng = norm_g[0][None, :]

    tn = 1024
    u_col = 4 * ATT_WIDTH
    qkv = _in_proj_latent(x2d, as_mod(scale, 0, batch), as_mod(shift, 0, batch), ng, w_in_bf,
                          tn, seq, _rope_tables(seq))
    tnc = SSM_WIDTH
    k0 = ATT_WIDTH // tnc
    ctx_cols = list(range(k0, 3 * k0)) + [u_col // tnc]
    kvc = _in_proj_ctx(ctx2d, as_mod(scale, batch, batch + 1), as_mod(shift, batch, batch + 1), ng, w_in_bf,
                       ctx_cols, tnc)

    lam_vecs = jnp.stack([lambda_q1[0], lambda_k1[0], lambda_q2[0], lambda_k2[0]]).astype(F32)
    a_br = _attention(qkv, kvc, lam_vecs, subln_g[0].astype(F32), batch, seq, ctx_len)

    nd = ssm_lambda_re.shape[1]
    modes = lambda v: v[0].reshape(nd, 1, SSM_MODES)
    lstep = jnp.repeat(ssm_log_step[0], SSM_STATE, axis=-1).reshape(nd, 1, SSM_MODES)
    b_t = lambda v: jnp.tile(v[0].reshape(nd, SSM_MODES, SSM_GROUP).transpose(0, 2, 1), (1, GROUPS_PER_BLOCK, 1))
    c_t = lambda v: jnp.tile(v[0].transpose(0, 1, 3, 2).reshape(nd, SSM_MODES, SSM_GROUP), (1, 1, GROUPS_PER_BLOCK))
    a_disc, bblk, cblk = _ssm_prep(modes(ssm_lambda_re), modes(ssm_lambda_im), lstep,
                                   b_t(ssm_b_re), b_t(ssm_b_im), c_t(ssm_c_re), c_t(ssm_c_im))

    qkv3 = qkv.reshape(batch, seq, IN_WIDTH)
    kvc3 = kvc.reshape(batch, ctx_len, len(ctx_cols) * tnc)
    scan = functools.partial(_ssm_scan, kvc3, qkv3, len(ctx_cols) - 1, u_col // SSM_WIDTH, a_disc, bblk, cblk)
    yf = scan(0)
    y_bm = scan(1, yf=yf, dvec=ssm_d[0].reshape(1, SSM_WIDTH).astype(F32))

    out = _merge_out(a_br, y_bm.reshape(batch * seq, SSM_WIDTH), qkv, x2d, as_mod(gate, 0, batch),
                     w_glu[0].astype(BF16), b_glu[0][None, :], w_pa[0].astype(BF16), w_ps[0].astype(BF16),
                     w_out[0].astype(BF16), final_g[None, :], seq)
    return out.reshape(batch, seq, d)
```

```python
import functools
import math

import jax
import jax.numpy as jnp
import numpy as np
from jax import lax
from jax.experimental import pallas as pl
from jax.experimental.pallas import tpu as pltpu

F32 = jnp.float32
BF16 = jnp.bfloat16

D_MODEL = 2048
GRID_W = 64
ATT_HEADS = 8
ATT_DIM = 64
HEAD_W = 2 * ATT_DIM
ATT_WIDTH = ATT_HEADS * HEAD_W
SSM_GROUP = 16
SSM_GROUPS = 32
SSM_WIDTH = SSM_GROUP * SSM_GROUPS
SSM_STATE = 64
SSM_MODES = SSM_GROUPS * SSM_STATE
IN_WIDTH = 4 * ATT_WIDTH + 2 * SSM_WIDTH + 2 * D_MODEL
ROPE_BASE = 10000.0
NORM_EPS = 1e-6
LAM_INIT = 0.8 - 0.6 * math.exp(0.0)

LANES = 128
SUBLANES = 8
VMEM_BYTES_V7X = 64 * 1024 * 1024
MIB = 1024 * 1024

GROUPS_PER_BLOCK = LANES // SSM_GROUP
SSM_BLOCKS = SSM_GROUPS // GROUPS_PER_BLOCK
BLOCK_MODES = GROUPS_PER_BLOCK * SSM_STATE
BLOCK_STATE = 2 * BLOCK_MODES
SSM_CHUNK = 128
KEY_CHUNK = 256
VT_PAD = 16


def _vmem_limit(nbytes):
    return int(min(nbytes + nbytes // 4, VMEM_BYTES_V7X - 8 * MIB))


def _sigmoid(v):
    return 1.0 / (1.0 + jnp.exp(-v))


def _silu(v):
    return v * _sigmoid(v)


def _adaln_kernel(c_ref, w_ref, b_ref, o_ref):
    o_ref[...] = jnp.dot(_silu(c_ref[...]).astype(BF16), w_ref[...].astype(BF16),
                         preferred_element_type=F32) + b_ref[...]


def _adaln(cc, w_ada, b_ada):
    rows, d = cc.shape
    n = w_ada.shape[1]
    tn = 768
    est = 2 * (d * tn * 4) + 2 * rows * tn * 4 + rows * d * 4 * 2
    return pl.pallas_call(
        _adaln_kernel,
        grid=(n // tn,),
        in_specs=[pl.BlockSpec((rows, d), lambda j: (0, 0)),
                  pl.BlockSpec((d, tn), lambda j: (0, j)),
                  pl.BlockSpec((1, tn), lambda j: (0, j))],
        out_specs=pl.BlockSpec((rows, tn), lambda j: (0, j)),
        out_shape=jax.ShapeDtypeStruct((rows, n), F32),
        compiler_params=pltpu.CompilerParams(dimension_semantics=("arbitrary",),
                                             vmem_limit_bytes=_vmem_limit(est)),
        name="adaln",
    )(cc, w_ada, b_ada)


def _in_proj_ctx_kernel(x_ref, scale_ref, shift_ref, g_ref, w_ref, o_ref, h_ref, *, norm_rows):
    j = pl.program_id(1)
    tm = x_ref.shape[0]

    @pl.when(j == 0)
    def _():
        gmod = g_ref[...] * (1.0 + scale_ref[0])
        shift = shift_ref[0]

        def body(r, carry):
            r0 = pl.multiple_of(r * norm_rows, norm_rows)
            xf = x_ref[pl.ds(r0, norm_rows), :]
            ms = jnp.mean(xf * xf, axis=-1, keepdims=True)
            h_ref[pl.ds(r0, norm_rows), :] = (xf * lax.rsqrt(ms + NORM_EPS) * gmod + shift).astype(BF16)
            return carry

        lax.fori_loop(0, tm // norm_rows, body, 0)

    o_ref[...] = jnp.dot(h_ref[...], w_ref[...], preferred_element_type=F32).astype(o_ref.dtype)


def _in_proj_lat_kernel(x_ref, scale_ref, shift_ref, nscale_ref, nshift_ref, g_ref, w_ref,
                        cos_ref, sina_ref, sinb_ref, o_ref, h_ref, *, norm_rows, fill_rows):
    i = pl.program_id(0)
    j = pl.program_id(1)
    tm = x_ref.shape[0]

    def normalise(r0, nrows, gmod, shift, slot):
        xf = x_ref[pl.ds(r0, nrows), :]
        ms = jnp.mean(xf * xf, axis=-1, keepdims=True)
        h_ref[slot, pl.ds(r0, nrows), :] = (xf * lax.rsqrt(ms + NORM_EPS) * gmod + shift).astype(BF16)

    @pl.when((i == 0) & (j == 0))
    def _():
        gmod = g_ref[...] * (1.0 + scale_ref[0])
        shift = shift_ref[0]

        def body(r, carry):
            normalise(pl.multiple_of(r * norm_rows, norm_rows), norm_rows, gmod, shift, 0)
            return carry

        lax.fori_loop(0, tm // norm_rows, body, 0)

    def step(cur, rotary):
        acc = jnp.dot(h_ref[cur], w_ref[...], preferred_element_type=F32)
        if rotary:
            qscale = jnp.where(j == 0, ATT_DIM ** -0.5 * math.log2(math.e), 1.0).astype(F32)
            cos = (cos_ref[...] * qscale).astype(BF16)
            sina = (sina_ref[...] * qscale).astype(BF16)
            sinb = (sinb_ref[...] * qscale).astype(BF16)
            for cb in range(acc.shape[1] // HEAD_W):
                v = acc[:, cb * HEAD_W:(cb + 1) * HEAD_W].astype(BF16)
                up = pltpu.roll(v, HEAD_W - ATT_DIM // 4, 1)
                dn = pltpu.roll(v, ATT_DIM // 4, 1)
                o_ref[:, cb * HEAD_W:(cb + 1) * HEAD_W] = v * cos + up * sina + dn * sinb
        else:
            o_ref[...] = acc.astype(o_ref.dtype)
        r0 = pl.multiple_of(jnp.maximum(j - 1, 0) * fill_rows, fill_rows)
        normalise(r0, fill_rows, g_ref[...] * (1.0 + nscale_ref[0]), nshift_ref[0], 1 - cur)

    for par in range(2):
        for rotary in (True, False):
            pl.when((i % 2 == par) & ((j < 2) == rotary))(functools.partial(step, par, rotary))


def _in_proj_latent(x2d, scale, shift, norm_g, w_bf, tn, seq_len, rope_tables, tm=1024):
    m, d = x2d.shape
    nj = w_bf.shape[1] // tn
    nblk = m // tm
    seq_blocks = seq_len // tm
    assert tm % (nj - 1) == 0
    nxt = lambda i: jnp.minimum(i + 1, nblk - 1)
    in_specs = [pl.BlockSpec((tm, d), lambda i, j: (jnp.minimum(i + jnp.minimum(j, 1), nblk - 1), 0)),
                pl.BlockSpec((1, 1, d), lambda i, j: (i // seq_blocks, 0, 0)),
                pl.BlockSpec((1, 1, d), lambda i, j: (i // seq_blocks, 0, 0)),
                pl.BlockSpec((1, 1, d), lambda i, j: (nxt(i) // seq_blocks, 0, 0)),
                pl.BlockSpec((1, 1, d), lambda i, j: (nxt(i) // seq_blocks, 0, 0)),
                pl.BlockSpec((1, d), lambda i, j: (0, 0)),
                pl.BlockSpec((d, tn), lambda i, j: (0, j))]
    in_specs += [pl.BlockSpec((tm, HEAD_W), lambda i, j: (i % seq_blocks, 0)) for _ in rope_tables]
    est = (2 * tm * d * 4 + 2 * tm * d * 2 + 2 * d * tn * 2 + 2 * tm * tn * 2 + 2 * tm * tn * 4
           + 6 * tm * HEAD_W * 4)
    return pl.pallas_call(
        functools.partial(_in_proj_lat_kernel, norm_rows=256, fill_rows=tm // (nj - 1)),
        grid=(nblk, nj),
        in_specs=in_specs,
        out_specs=pl.BlockSpec((tm, tn), lambda i, j: (i, j)),
        out_shape=jax.ShapeDtypeStruct((m, nj * tn), BF16),
        scratch_shapes=[pltpu.VMEM((2, tm, d), BF16)],
        compiler_params=pltpu.CompilerParams(dimension_semantics=("arbitrary", "arbitrary"),
                                             vmem_limit_bytes=_vmem_limit(est)),
        name="in_proj_rope",
    )(x2d, scale, shift, scale, shift, norm_g, w_bf, *rope_tables)


def _in_proj_ctx(x2d, scale, shift, norm_g, w_bf, col_blocks, tn, tm=1024):
    m, d = x2d.shape
    nj = len(col_blocks)

    def w_map(i, j):
        idx = jnp.int32(col_blocks[-1])
        for k in range(nj - 2, -1, -1):
            idx = jnp.where(j == k, jnp.int32(col_blocks[k]), idx)
        return (0, idx)

    est = 2 * tm * d * 4 + tm * d * 2 + 2 * d * tn * 2 + 2 * tm * tn * 2 + 2 * tm * tn * 4
    return pl.pallas_call(
        functools.partial(_in_proj_ctx_kernel, norm_rows=256),
        grid=(m // tm, nj),
        in_specs=[pl.BlockSpec((tm, d), lambda i, j: (i, 0)),
                  pl.BlockSpec((1, 1, d), lambda i, j: (0, 0, 0)),
                  pl.BlockSpec((1, 1, d), lambda i, j: (0, 0, 0)),
                  pl.BlockSpec((1, d), lambda i, j: (0, 0)),
                  pl.BlockSpec((d, tn), w_map)],
        out_specs=pl.BlockSpec((tm, tn), lambda i, j: (i, j)),
        out_shape=jax.ShapeDtypeStruct((m, nj * tn), BF16),
        scratch_shapes=[pltpu.VMEM((tm, d), BF16)],
        compiler_params=pltpu.CompilerParams(dimension_semantics=("arbitrary", "arbitrary"),
                                             vmem_limit_bytes=_vmem_limit(est)),
        name="in_proj_ctx",
    )(x2d, scale, shift, norm_g, w_bf)


def _rope_tables(seq):
    rows = seq // GRID_W
    row = np.repeat(np.arange(rows), GRID_W).astype(np.float64)
    col = np.tile(np.arange(GRID_W), rows).astype(np.float64)
    nf = ATT_DIM // 4
    inv = ROPE_BASE ** (-np.arange(nf, dtype=np.float64) / nf)

    def axis_tables(pos):
        ang = pos[:, None] * inv[None, :]
        c, s = np.cos(ang), np.sin(ang)
        z = np.zeros_like(s)
        return (np.concatenate([c, c], -1), np.concatenate([-s, z], -1), np.concatenate([z, s], -1))

    parts = [axis_tables(row), axis_tables(col)]
    out = []
    for k in range(3):
        comp = np.concatenate([parts[0][k], parts[1][k]], -1)
        out.append(jnp.asarray(np.concatenate([comp, comp], -1), dtype=F32))
    return out


def _attn_kernel(q_ref, k_ref, v_ref, kc_ref, vc_ref, ga_ref, lam_ref, gcol_ref, o_ref,
                 qt_scr, kall_scr, vt_scr, s_scr, *, rows):
    seq = k_ref.shape[0]
    keys = seq + kc_ref.shape[0]
    nblk = q_ref.shape[0] // rows
    nch = keys // KEY_CHUNK
    sub = KEY_CHUNK // SUBLANES
    lamv = lam_ref[...]
    lam = (jnp.exp(jnp.sum(lamv[0:1] * lamv[1:2], axis=-1, keepdims=True))
           - jnp.exp(jnp.sum(lamv[2:3] * lamv[3:4], axis=-1, keepdims=True)) + LAM_INIT)

    kall_scr[0:seq, :] = k_ref[...]
    kall_scr[seq:keys, :] = kc_ref[...]
    vt_scr[0:HEAD_W, 0:seq] = v_ref[...].T
    vt_scr[0:HEAD_W, seq:keys] = vc_ref[...].T
    pad_row = lax.broadcasted_iota(jnp.int32, (VT_PAD, keys), 0)
    vt_scr[HEAD_W:HEAD_W + VT_PAD, :] = jnp.where(pad_row == 0, 1.0, 0.0).astype(BF16)
    drow = lax.broadcasted_iota(jnp.int32, (HEAD_W, rows), 0)
    for i in range(nblk):
        qt = q_ref[i * rows:(i + 1) * rows, :].T
        qt_scr[0, i] = jnp.where(drow < ATT_DIM, qt, jnp.zeros_like(qt))
        qt_scr[1, i] = jnp.where(drow >= ATT_DIM, qt, jnp.zeros_like(qt))

    def colmax(x):
        return jnp.max(x.reshape(sub, SUBLANES, rows), axis=0)

    def step(t, slot, do_qk, do_sm, mfin):
        mpart = [None, None]
        acc = [None, None]
        for c in range(nch):
            ks = slice(c * KEY_CHUNK, (c + 1) * KEY_CHUNK)
            if do_qk:
                kc = kall_scr[ks, :]
                for comp in range(2):
                    s = jnp.dot(kc, qt_scr[comp, t], preferred_element_type=F32)
                    s_scr[slot, comp, c] = s
                    cm = colmax(s)
                    mpart[comp] = cm if mpart[comp] is None else jnp.maximum(mpart[comp], cm)
            if do_sm:
                vt = vt_scr[:, ks]
                for comp in range(2):
                    p = jnp.exp2(s_scr[1 - slot, comp, c] - mfin[comp]).astype(BF16)
                    d = jnp.dot(vt, p, preferred_element_type=F32)
                    acc[comp] = d if acc[comp] is None else acc[comp] + d
        if do_sm:
            r0 = pl.multiple_of((t - 1) * rows, rows)
            l1 = acc[0][HEAD_W:HEAD_W + 1]
            l2 = acc[1][HEAD_W:HEAD_W + 1]
            o = acc[0][0:HEAD_W] * (1.0 / l1) - acc[1][0:HEAD_W] * (lam / l2)
            ms = jnp.mean(o * o, axis=0, keepdims=True)
            on = (o * lax.rsqrt(ms + NORM_EPS) * gcol_ref[...] * (1.0 - LAM_INIT)).T
            o_ref[pl.ds(r0, rows), :] = (on * _silu(ga_ref[pl.ds(r0, rows), :].astype(F32))).astype(o_ref.dtype)
        return [jnp.max(m, axis=0, keepdims=True) for m in mpart] if do_qk else mfin

    m = step(0, 0, True, False, None)

    def body(tt, carry):
        t = 2 * tt + 1
        mm = step(t, 1, True, True, list(carry))
        mm = step(t + 1, 0, True, True, mm)
        return mm[0], mm[1]

    m = lax.fori_loop(0, (nblk - 2) // 2, body, (m[0], m[1]))
    m = step(nblk - 1, 1, True, True, list(m))
    step(nblk, 0, False, True, m)


def _attention(qkv, kvc, lam_vecs, subln_g, batch, seq, ctx_len, rows=256):
    kcol = ATT_WIDTH // HEAD_W
    keys = seq + ctx_len
    nch = keys // KEY_CHUNK
    assert keys % KEY_CHUNK == 0 and seq % (2 * rows) == 0
    gcol = jnp.broadcast_to(subln_g.reshape(HEAD_W, 1), (HEAD_W, rows))
    est = (2 * 2 * keys * rows * 4 + 2 * seq * HEAD_W * 2 + 2 * keys * HEAD_W * 2
           + 2 * (4 * seq + 2 * ctx_len) * HEAD_W * 2 + 64 * KEY_CHUNK * rows * 4)
    return pl.pallas_call(
        functools.partial(_attn_kernel, rows=rows),
        grid=(batch, ATT_HEADS),
        in_specs=[pl.BlockSpec((seq, HEAD_W), lambda b, h: (b, h)),
                  pl.BlockSpec((seq, HEAD_W), lambda b, h: (b, kcol + h)),
                  pl.BlockSpec((seq, HEAD_W), lambda b, h: (b, 2 * kcol + h)),
                  pl.BlockSpec((ctx_len, HEAD_W), lambda b, h: (b, h)),
                  pl.BlockSpec((ctx_len, HEAD_W), lambda b, h: (b, kcol + h)),
                  pl.BlockSpec((seq, HEAD_W), lambda b, h: (b, 3 * kcol + h)),
                  pl.BlockSpec((4, ATT_DIM), lambda b, h: (0, 0)),
                  pl.BlockSpec((HEAD_W, rows), lambda b, h: (0, 0))],
        out_specs=pl.BlockSpec((seq, HEAD_W), lambda b, h: (b, h)),
        out_shape=jax.ShapeDtypeStruct((batch * seq, ATT_WIDTH), BF16),
        scratch_shapes=[pltpu.VMEM((2, seq // rows, HEAD_W, rows), BF16),
                        pltpu.VMEM((keys, HEAD_W), BF16),
                        pltpu.VMEM((HEAD_W + VT_PAD, keys), BF16),
                        pltpu.VMEM((2, 2, nch, KEY_CHUNK, rows), F32)],
        compiler_params=pltpu.CompilerParams(dimension_semantics=("arbitrary",) * 2,
                                             vmem_limit_bytes=_vmem_limit(est)),
        name="diff_attention",
    )(qkv, qkv, qkv, kvc, kvc, qkv, lam_vecs, gcol)


def _ssm_prep_kernel(lre_ref, lim_ref, lstep_ref, bre_ref, bim_ref, cre_ref, cim_ref,
                     a_ref, bblk_ref, cblk_ref):
    lre = lre_ref[0]
    lim = lim_ref[0]
    dt = jnp.exp(lstep_ref[0])
    mag = jnp.exp(lre * dt)
    ar = mag * jnp.cos(lim * dt)
    ai = mag * jnp.sin(lim * dt)
    den = lre * lre + lim * lim
    fr = ((ar - 1.0) * lre + ai * lim) / den
    fi = (ai * lre - (ar - 1.0) * lim) / den
    a_ref[0, 0:1, :] = ar
    a_ref[0, 1:2, :] = ai
    br = bre_ref[0]
    bi = bim_ref[0]
    bbr = fr * br - fi * bi
    bbi = fr * bi + fi * br
    row_g = lax.broadcasted_iota(jnp.int32, (LANES, BLOCK_MODES), 0) // SSM_GROUP
    col_g = lax.broadcasted_iota(jnp.int32, (LANES, BLOCK_MODES), 1) // SSM_STATE
    keep_b = row_g == col_g
    row_g2 = lax.broadcasted_iota(jnp.int32, (BLOCK_MODES, LANES), 0) // SSM_STATE
    col_g2 = lax.broadcasted_iota(jnp.int32, (BLOCK_MODES, LANES), 1) // SSM_GROUP
    keep_c = row_g2 == col_g2
    for k in range(SSM_BLOCKS):
        sl = slice(k * BLOCK_MODES, (k + 1) * BLOCK_MODES)
        bblk_ref[0, k, :, 0:BLOCK_MODES] = jnp.where(keep_b, bbr[:, sl], 0.0).astype(BF16)
        bblk_ref[0, k, :, BLOCK_MODES:BLOCK_STATE] = jnp.where(keep_b, bbi[:, sl], 0.0).astype(BF16)
        cblk_ref[0, k, 0:BLOCK_MODES, :] = jnp.where(keep_c, cre_ref[0, sl, :], 0.0).astype(BF16)
        cblk_ref[0, k, BLOCK_MODES:BLOCK_STATE, :] = jnp.where(keep_c, -cim_ref[0, sl, :], 0.0).astype(BF16)


def _ssm_prep(lre, lim, lstep, bre_t, bim_t, cre_t, cim_t):
    nd = lre.shape[0]
    vec = pl.BlockSpec((1, 1, SSM_MODES), lambda d: (d, 0, 0))
    bsp = pl.BlockSpec((1, LANES, SSM_MODES), lambda d: (d, 0, 0))
    csp = pl.BlockSpec((1, SSM_MODES, LANES), lambda d: (d, 0, 0))
    est = 2 * (4 * LANES * SSM_MODES * 4 + 2 * SSM_BLOCKS * LANES * BLOCK_STATE * 2) + 8 * LANES * SSM_MODES * 4
    return pl.pallas_call(
        _ssm_prep_kernel,
        grid=(nd,),
        in_specs=[vec, vec, vec, bsp, bsp, csp, csp],
        out_specs=[pl.BlockSpec((1, 2, SSM_MODES), lambda d: (d, 0, 0)),
                   pl.BlockSpec((1, SSM_BLOCKS, LANES, BLOCK_STATE), lambda d: (d, 0, 0, 0)),
                   pl.BlockSpec((1, SSM_BLOCKS, BLOCK_STATE, LANES), lambda d: (d, 0, 0, 0))],
        out_shape=[jax.ShapeDtypeStruct((nd, 2, SSM_MODES), F32),
                   jax.ShapeDtypeStruct((nd, SSM_BLOCKS, LANES, BLOCK_STATE), BF16),
                   jax.ShapeDtypeStruct((nd, SSM_BLOCKS, BLOCK_STATE, LANES), BF16)],
        compiler_params=pltpu.CompilerParams(dimension_semantics=("arbitrary",),
                                             vmem_limit_bytes=_vmem_limit(est)),
        name="ssm_prep",
    )(lre, lim, lstep, bre_t, bim_t, cre_t, cim_t)


def _ssm_scan_kernel(*refs, reverse, batch, ctx_chunks):
    if reverse:
        uc_ref, ul_ref, a_ref, bblk_ref, cblk_ref, yf_ref, d_ref, y_ref, *s_refs, hb_ref, h_ref = refs
    else:
        uc_ref, ul_ref, a_ref, bblk_ref, cblk_ref, y_ref, *s_refs, hb_ref, h_ref = refs
    steps = ul_ref.shape[1]
    rows = steps * batch
    step = pl.program_id(0)
    zoff = lax.shift_right_logical(step, 30) * batch

    @pl.when(step == 0)
    def _():
        h_ref[...] = jnp.zeros_like(h_ref)

    u_bt = jnp.where(step < ctx_chunks, uc_ref[...], ul_ref[...]).astype(F32)
    u32 = jnp.swapaxes(u_bt, 0, 1).reshape(rows, SSM_WIDTH)
    u = u32.astype(BF16)
    for k in range(SSM_BLOCKS):
        s_refs[k][...] = jnp.dot(
            u[:, k * LANES:(k + 1) * LANES], bblk_ref[0, k], preferred_element_type=F32)

    for k in range(SSM_BLOCKS):
        c_re = k * BLOCK_STATE
        c_im = c_re + BLOCK_MODES
        ar = jnp.broadcast_to(a_ref[0, 0:1, k * BLOCK_MODES:(k + 1) * BLOCK_MODES], (batch, BLOCK_MODES))
        ai = jnp.broadcast_to(a_ref[0, 1:2, k * BLOCK_MODES:(k + 1) * BLOCK_MODES], (batch, BLOCK_MODES))
        hr = h_ref[:, c_re:c_re + BLOCK_MODES]
        hi = h_ref[:, c_im:c_im + BLOCK_MODES]
        for i in range(0, steps, 2):
            pair = []
            for t in ((steps - 1 - i, steps - 2 - i) if reverse else (i, i + 1)):
                r0 = t * batch
                rr = pl.ds(pl.multiple_of(r0 + zoff, batch), batch)
                nhr = ar * hr - ai * hi + s_refs[k][rr, 0:BLOCK_MODES]
                nhi = ar * hi + ai * hr + s_refs[k][rr, BLOCK_MODES:BLOCK_STATE]
                pair.append((nhr, nhi))
                hr, hi = nhr, nhi
            lo, hi2 = (pair[1], pair[0]) if reverse else (pair[0], pair[1])
            p0 = min(steps - 2 - i, steps - 1 - i) * batch if reverse else i * batch
            hb_ref[p0:p0 + 2 * batch, c_re:c_re + BLOCK_MODES] = jnp.concatenate([lo[0], hi2[0]], axis=0).astype(BF16)
            hb_ref[p0:p0 + 2 * batch, c_im:c_im + BLOCK_MODES] = jnp.concatenate([lo[1], hi2[1]], axis=0).astype(BF16)
        h_ref[:, c_re:c_re + BLOCK_MODES] = hr
        h_ref[:, c_im:c_im + BLOCK_MODES] = hi

    ys = []
    for k in range(SSM_BLOCKS):
        yk = jnp.dot(hb_ref[:, k * BLOCK_STATE:(k + 1) * BLOCK_STATE], cblk_ref[0, k],
                     preferred_element_type=F32)
        cols = slice(k * LANES, (k + 1) * LANES)
        if reverse:
            yk = yk + yf_ref[:, cols] + d_ref[:, cols] * u32[:, cols]
        ys.append(yk)
    y = jnp.concatenate(ys, axis=1)
    if reverse:
        y_ref[...] = jnp.swapaxes(y.reshape(steps, batch, SSM_WIDTH), 0, 1)
    else:
        y_ref[...] = y


def _ssm_scan(u_ctx, u_lat, u_ctx_col, u_lat_col, a, bblk, cblk, direction, yf=None, dvec=None):
    batch = u_lat.shape[0]
    rows = SSM_CHUNK * batch
    cc = u_ctx.shape[1] // SSM_CHUNK
    lc = u_lat.shape[1] // SSM_CHUNK
    reverse = direction == 1
    if reverse:
        cidx = lambda s: jnp.maximum(cc - 1 - s, 0)
        lidx = lambda s: jnp.minimum(lc - 1, lc - 1 + cc - s)
    else:
        cidx = lambda s: jnp.minimum(s, cc - 1)
        lidx = lambda s: jnp.maximum(s - cc, 0)
    in_specs = [pl.BlockSpec((batch, SSM_CHUNK, SSM_WIDTH), lambda s: (0, cidx(s), u_ctx_col)),
                pl.BlockSpec((batch, SSM_CHUNK, SSM_WIDTH), lambda s: (0, lidx(s), u_lat_col)),
                pl.BlockSpec((1, 2, SSM_MODES), lambda s: (direction, 0, 0)),
                pl.BlockSpec((1, SSM_BLOCKS, LANES, BLOCK_STATE), lambda s: (direction, 0, 0, 0)),
                pl.BlockSpec((1, SSM_BLOCKS, BLOCK_STATE, LANES), lambda s: (direction, 0, 0, 0))]
    args = [u_ctx, u_lat, a, bblk, cblk]
    if reverse:
        in_specs += [pl.BlockSpec((rows, SSM_WIDTH), lambda s: (lidx(s), 0)),
                     pl.BlockSpec((1, SSM_WIDTH), lambda s: (0, 0))]
        args += [yf, dvec]
        out_spec = pl.BlockSpec((batch, SSM_CHUNK, SSM_WIDTH), lambda s: (0, lidx(s), 0))
        out_shape = jax.ShapeDtypeStruct((batch, lc * SSM_CHUNK, SSM_WIDTH), F32)
    else:
        out_spec = pl.BlockSpec((rows, SSM_WIDTH), lambda s: (lidx(s), 0))
        out_shape = jax.ShapeDtypeStruct((lc * rows, SSM_WIDTH), F32)
    state_w = SSM_BLOCKS * BLOCK_STATE
    est = (rows * state_w * 6 + 2 * rows * state_w // SSM_BLOCKS * 4 + 10 * rows * SSM_WIDTH * 4
           + 4 * SSM_BLOCKS * LANES * BLOCK_STATE * 2)
    return pl.pallas_call(
        functools.partial(_ssm_scan_kernel, reverse=reverse, batch=batch, ctx_chunks=cc),
        grid=(cc + lc,),
        in_specs=in_specs,
        out_specs=out_spec,
        out_shape=out_shape,
        scratch_shapes=[pltpu.VMEM((rows, BLOCK_STATE), F32)] * SSM_BLOCKS + [pltpu.VMEM((rows, state_w), BF16),
                        pltpu.VMEM((batch, state_w), F32)],
        compiler_params=pltpu.CompilerParams(dimension_semantics=("arbitrary",),
                                             vmem_limit_bytes=_vmem_limit(est)),
        name="ssm_scan_bwd" if reverse else "ssm_scan_fwd",
    )(*args)


def _gelu_tanh(v):
    return 0.5 * v * (1.0 + jnp.tanh(math.sqrt(2.0 / math.pi) * (v + 0.044715 * v * v * v)))


def _merge_kernel(a_ref, y_ref, gs_ref, gma0_ref, gma1_ref, gms0_ref, gms1_ref, x_ref, gate_ref,
                  wglu_ref, bglu_ref, wpa_ref, wps_ref, wout_ref, fg_ref, o_ref):
    yg = _gelu_tanh(y_ref[...])
    z = yg * _sigmoid(jnp.dot(yg.astype(BF16), wglu_ref[...], preferred_element_type=F32) + bglu_ref[...])
    s_br = (z * _silu(gs_ref[...].astype(F32))).astype(BF16)
    ta = jnp.dot(a_ref[...], wpa_ref[...], preferred_element_type=F32)
    ts = jnp.dot(s_br, wps_ref[...], preferred_element_type=F32)
    half = D_MODEL // 2
    parts = []
    for p, (ga_r, gs_r) in enumerate(((gma0_ref, gms0_ref), (gma1_ref, gms1_ref))):
        cols = slice(p * half, (p + 1) * half)
        parts.append((_sigmoid(ga_r[...].astype(F32)) * ta[:, cols]
                      + _sigmoid(gs_r[...].astype(F32)) * ts[:, cols]).astype(BF16))
    t = jnp.concatenate(parts, axis=1)
    out = jnp.dot(t, wout_ref[...], preferred_element_type=F32)
    xn = x_ref[...] + gate_ref[0] * out
    ms = jnp.mean(xn * xn, axis=-1, keepdims=True)
    o_ref[...] = xn * lax.rsqrt(ms + NORM_EPS) * fg_ref[...]


def _merge_out(a_br, y_bm, qkv, x2d, gate, w_glu, b_glu, w_pa, w_ps, w_out, final_g, seq, tm=256):
    m, d = x2d.shape
    half = d // 2
    per_batch = seq // tm
    gs_col = (4 * ATT_WIDTH + SSM_WIDTH) // SSM_WIDTH
    gm_col = (4 * ATT_WIDTH + 2 * SSM_WIDTH) // half
    const = lambda i: (0, 0)
    est = (2 * (tm * ATT_WIDTH * 2 + tm * SSM_WIDTH * 6 + 4 * tm * half * 2 + 2 * tm * d * 4)
           + 2 * (SSM_WIDTH * SSM_WIDTH + ATT_WIDTH * d + SSM_WIDTH * d + d * d) * 2 + 6 * tm * d * 4)
    return pl.pallas_call(
        _merge_kernel,
        grid=(m // tm,),
        in_specs=[pl.BlockSpec((tm, ATT_WIDTH), lambda i: (i, 0)),
                  pl.BlockSpec((tm, SSM_WIDTH), lambda i: (i, 0)),
                  pl.BlockSpec((tm, SSM_WIDTH), lambda i: (i, gs_col)),
                  pl.BlockSpec((tm, half), lambda i: (i, gm_col)),
                  pl.BlockSpec((tm, half), lambda i: (i, gm_col + 1)),
                  pl.BlockSpec((tm, half), lambda i: (i, gm_col + 2)),
                  pl.BlockSpec((tm, half), lambda i: (i, gm_col + 3)),
                  pl.BlockSpec((tm, d), lambda i: (i, 0)),
                  pl.BlockSpec((1, 1, d), lambda i: (i // per_batch, 0, 0)),
                  pl.BlockSpec((SSM_WIDTH, SSM_WIDTH), const),
                  pl.BlockSpec((1, SSM_WIDTH), const),
                  pl.BlockSpec((ATT_WIDTH, d), const),
                  pl.BlockSpec((SSM_WIDTH, d), const),
                  pl.BlockSpec((d, d), const),
                  pl.BlockSpec((1, d), const)],
        out_specs=pl.BlockSpec((tm, d), lambda i: (i, 0)),
        out_shape=jax.ShapeDtypeStruct((m, d), F32),
        compiler_params=pltpu.CompilerParams(dimension_semantics=("arbitrary",),
                                             vmem_limit_bytes=_vmem_limit(est)),
        name="merge_out",
    )(a_br, y_bm, qkv, qkv, qkv, qkv, qkv, x2d, gate, w_glu, b_glu, w_pa, w_ps, w_out, final_g)


def kernel(x, c, ctx, c_ctx, w_ada, b_ada, norm_g, w_in, lambda_q1, lambda_k1, lambda_q2, lambda_k2,
           subln_g, ssm_lambda_re, ssm_lambda_im, ssm_log_step, ssm_b_re, ssm_b_im, ssm_c_re, ssm_c_im,
           ssm_d, w_glu, b_glu, w_pa, w_ps, w_out, final_g):
    batch, seq, d = x.shape
    ctx_len = ctx.shape[1]
    assert w_ada.shape[0] == 1, "single-layer trunk"
    assert (d, seq % 1024, ctx_len % SSM_CHUNK, seq % SSM_CHUNK) == (D_MODEL, 0, 0, 0)

    pad = (-(batch + 1)) % SUBLANES
    cc = jnp.concatenate([c, c_ctx[None, :], jnp.zeros((pad, d), F32)], axis=0)
    mod = _adaln(cc, w_ada[0], b_ada[0][None, :])
    shift, scale, gate = (mod[:, k * d:(k + 1) * d] for k in range(3))
    as_mod = lambda v, lo, hi: v[lo:hi].reshape(hi - lo, 1, d)

    w_in_bf = w_in[0].astype(BF16)
    x2d = x.reshape(batch * seq, d)
    ctx2d = ctx.reshape(batch * ctx_len, d)
    ng = norm_g[0][None, :]

    tn = 1024
    u_col = 4 * ATT_WIDTH
    qkv = _in_proj_latent(x2d, as_mod(scale, 0, batch), as_mod(shift, 0, batch), ng, w_in_bf,
                          tn, seq, _rope_tables(seq))
    tnc = SSM_WIDTH
    k0 = ATT_WIDTH // tnc
    ctx_cols = list(range(k0, 3 * k0)) + [u_col // tnc]
    kvc = _in_proj_ctx(ctx2d, as_mod(scale, batch, batch + 1), as_mod(shift, batch, batch + 1), ng, w_in_bf,
                       ctx_cols, tnc)

    lam_vecs = jnp.stack([lambda_q1[0], lambda_k1[0], lambda_q2[0], lambda_k2[0]]).astype(F32)
    a_br = _attention(qkv, kvc, lam_vecs, subln_g[0].astype(F32), batch, seq, ctx_len)

    nd = ssm_lambda_re.shape[1]
    modes = lambda v: v[0].reshape(nd, 1, SSM_MODES)
    lstep = jnp.repeat(ssm_log_step[0], SSM_STATE, axis=-1).reshape(nd, 1, SSM_MODES)
    b_t = lambda v: jnp.tile(v[0].reshape(nd, SSM_MODES, SSM_GROUP).transpose(0, 2, 1), (1, GROUPS_PER_BLOCK, 1))
    c_t = lambda v: jnp.tile(v[0].transpose(0, 1, 3, 2).reshape(nd, SSM_MODES, SSM_GROUP), (1, 1, GROUPS_PER_BLOCK))
    a_disc, bblk, cblk = _ssm_prep(modes(ssm_lambda_re), modes(ssm_lambda_im), lstep,
                                   b_t(ssm_b_re), b_t(ssm_b_im), c_t(ssm_c_re), c_t(ssm_c_im))

    qkv3 = qkv.reshape(batch, seq, IN_WIDTH)
    kvc3 = kvc.reshape(batch, ctx_len, len(ctx_cols) * tnc)
    scan = functools.partial(_ssm_scan, kvc3, qkv3, len(ctx_cols) - 1, u_col // SSM_WIDTH, a_disc, bblk, cblk)
    yf = scan(0)
    y_bm = scan(1, yf=yf, dvec=ssm_d[0].reshape(1, SSM_WIDTH).astype(F32))

    out = _merge_out(a_br, y_bm.reshape(batch * seq, SSM_WIDTH), qkv, x2d, as_mod(gate, 0, batch),
                     w_glu[0].astype(BF16), b_glu[0][None, :], w_pa[0].astype(BF16), w_ps[0].astype(BF16),
                     w_out[0].astype(BF16), final_g[None, :], seq)
    return out.reshape(batch, seq, d)
```
